```python
import math, functools
import jax, jax.numpy as jnp
from jax import lax
import numpy as np

D_MODEL = 2048
BATCH = 2
SEQ = 4096
DEPTH = 1
DEC_BATCH = 32
DEC_SEQ = 4
PAST_LEN = 8192
PAGE_SIZE = 128

HEAD_DIM = 64
GLA_HEADS = 10
GLA_DK = 64
GLA_DV = 128
GLA_RANK = 16
GLA_TAU = 16.0
GLA_CHUNK = 64
DIL_HEADS = 12
DIL_PAIRS = ((128, 1), (512, 4), (2048, 16))
DIL_WINDOW = 2048
QBLK = 128
ROPE_THETA = 10000.0
D_FF = 5504
CONV_W = 3
EPS = 1e-6

GLA_QK = GLA_HEADS * GLA_DK
GLA_VW = GLA_HEADS * GLA_DV
DIL_W = DIL_HEADS * HEAD_DIM
MIX_WIDTH = GLA_VW + DIL_W
SPLIT_SIZES = (GLA_QK, GLA_QK, GLA_VW, GLA_VW, GLA_RANK, DIL_W, DIL_W, DIL_W)
N_IN = GLA_QK * 2 + GLA_VW * 2 + GLA_RANK + DIL_W * 3

kernel_name = 'hymba_gla_dilated_convffn_step'


def rmsnorm(x, g):
    xf = x.astype(jnp.float32)
    y = xf * lax.rsqrt(jnp.mean(xf * xf, axis=-1, keepdims=True) + EPS)
    return (y * g.astype(jnp.float32)).astype(x.dtype)


def split_in_proj(proj):
    parts, start = [], 0
    for size in SPLIT_SIZES:
        parts.append(proj[..., start:start + size])
        start += size
    return parts


def rope(x, pos):
    half = HEAD_DIM // 2
    inv_freq = ROPE_THETA ** (-2.0 * jnp.arange(half, dtype=jnp.float32) / HEAD_DIM)
    ang = pos.astype(jnp.float32)[:, None] * inv_freq[None, :]
    cos = jnp.cos(ang)[None, :, None, :]
    sin = jnp.sin(ang)[None, :, None, :]
    xf = x.astype(jnp.float32)
    x1, x2 = xf[..., :half], xf[..., half:]
    return jnp.concatenate([x1 * cos - x2 * sin, x2 * cos + x1 * sin], axis=-1).astype(x.dtype)


def gla_recurrence(q, k, v, log_a, s0):
    b_, t_, h_, dk = q.shape
    dv = v.shape[-1]
    c = math.gcd(t_, GLA_CHUNK)
    n = t_ // c

    def chunks(a):
        return a.reshape(b_, n, c, h_, a.shape[-1]).transpose(1, 0, 3, 2, 4)

    causal = jnp.tril(jnp.ones((c, c), dtype=bool))[:, :, None]

    def step(s, inp):
        qc, kc, vc, gc = inp
        cum = jnp.cumsum(gc, axis=2)
        diff = cum[:, :, :, None, :] - cum[:, :, None, :, :]
        decay = jnp.exp(jnp.where(causal, diff, -jnp.inf))
        scores = jnp.einsum('bhik,bhjk,bhijk->bhij', qc, kc, decay)
        o = (jnp.einsum('bhij,bhjv->bhiv', scores, vc)
             + jnp.einsum('bhik,bhkv->bhiv', qc * jnp.exp(cum), s))
        last = cum[:, :, -1:, :]
        s_new = (jnp.exp(last[:, :, 0, :])[..., None] * s
                 + jnp.einsum('bhjk,bhjv->bhkv', kc * jnp.exp(last - cum), vc))
        return s_new, o

    s_fin, o = lax.scan(step, s0, (chunks(q), chunks(k), chunks(v), chunks(log_a)))
    o = o.transpose(1, 0, 3, 2, 4).reshape(b_, t_, h_, dv)
    return o, s_fin


def dilated_branch(q, k_all, v_all, q_idx, window, dilation):
    n_keys = window // dilation + 1
    key_idx = q_idx[:, None] - dilation * jnp.arange(n_keys)[None, :]
    valid = key_idx >= 0
    key_idx = jnp.maximum(key_idx, 0)
    kg = jnp.take(k_all, key_idx, axis=1)
    vg = jnp.take(v_all, key_idx, axis=1)
    s = jnp.einsum('bthd,btnhd->bthn', q, kg, preferred_element_type=jnp.float32) * (HEAD_DIM ** -0.5)
    s = jnp.where(valid[None, :, None, :], s, -jnp.inf)
    m = jnp.max(s, axis=-1, keepdims=True)
    p = jnp.exp(s - m)
    den = jnp.sum(p, axis=-1, keepdims=True)
    num = jnp.einsum('bthn,btnhd->bthd', p, vg.astype(jnp.float32))
    return num, den, m


def dilated_mixture(q, k_all, v_all, q_idx):
    parts = [dilated_branch(q, k_all, v_all, q_idx, w, r) for (w, r) in DIL_PAIRS]
    m_all = functools.reduce(jnp.maximum, [m for _, _, m in parts])
    num = functools.reduce(jnp.add, [nm * jnp.exp(m - m_all) for nm, _, m in parts])
    den = functools.reduce(jnp.add, [d * jnp.exp(m - m_all) for _, d, m in parts])
    return (num / den).astype(q.dtype)


def dilated_attention(q, k_all, v_all, q_idx):
    b_, t_, h_, d_ = q.shape
    if t_ % QBLK != 0 or t_ <= QBLK:
        return dilated_mixture(q, k_all, v_all, q_idx)
    nb = t_ // QBLK
    qb = q.reshape(b_, nb, QBLK, h_, d_).transpose(1, 0, 2, 3, 4)
    ib = q_idx.reshape(nb, QBLK)
    ob = lax.map(lambda a: dilated_mixture(a[0], k_all, v_all, a[1]), (qb, ib))
    return ob.transpose(1, 0, 2, 3, 4).reshape(b_, t_, h_, d_)


def mixer_block(h, pos, gla_s0, k_hist, v_hist, w_in, w_gate_up, b_gate, gla_norm, w_out):
    b_, t_, _ = h.shape
    f32 = jnp.float32
    q_a, k_a, v_a, r_a, z_a, q_b, k_b, v_b = split_in_proj(h @ w_in)
    log_a = jax.nn.log_sigmoid((z_a @ w_gate_up + b_gate).astype(f32)) / GLA_TAU
    o_a, s_fin = gla_recurrence(
        q_a.reshape(b_, t_, GLA_HEADS, GLA_DK).astype(f32) * (GLA_DK ** -0.5),
        k_a.reshape(b_, t_, GLA_HEADS, GLA_DK).astype(f32),
        v_a.reshape(b_, t_, GLA_HEADS, GLA_DV).astype(f32),
        log_a.reshape(b_, t_, GLA_HEADS, GLA_DK),
        gla_s0.astype(f32))
    gate = jax.nn.silu(r_a.reshape(b_, t_, GLA_HEADS, GLA_DV).astype(f32))
    o_a = (rmsnorm(o_a, gla_norm) * gate).astype(h.dtype).reshape(b_, t_, GLA_VW)
    q = rope(q_b.reshape(b_, t_, DIL_HEADS, HEAD_DIM), pos)
    k = rope(k_b.reshape(b_, t_, DIL_HEADS, HEAD_DIM), pos)
    v = v_b.reshape(b_, t_, DIL_HEADS, HEAD_DIM)
    k_all = jnp.concatenate([k_hist.astype(k.dtype), k], axis=1)
    v_all = jnp.concatenate([v_hist.astype(v.dtype), v], axis=1)
    hist = k_hist.shape[1]
    o_b = dilated_attention(q, k_all, v_all, hist + jnp.arange(t_)).reshape(b_, t_, DIL_W)
    buf = hist if hist > 0 else min(DIL_WINDOW, t_)
    y = jnp.concatenate([o_a, o_b], axis=-1) @ w_out
    start = k_all.shape[1] - buf
    return y, s_fin.astype(h.dtype), k_all[:, start:], v_all[:, start:]


def conv_ffn(h, conv_hist, w_up, conv_w, conv_b, w_down):
    t_ = h.shape[1]
    up = h @ w_up
    full = jnp.concatenate([conv_hist.astype(up.dtype), up], axis=1)
    conv = conv_b + functools.reduce(jnp.add, [conv_w[j] * full[:, j:j + t_] for j in range(CONV_W)])
    u, g = conv[..., :D_FF], conv[..., D_FF:]
    return (jax.nn.silu(g) * u) @ w_down, full[:, full.shape[1] - (CONV_W - 1):]


def setup_inputs(seed: int = 0) -> dict:
    key = jax.random.key(seed)
    ks = jax.random.split(key, 18)
    f32 = jnp.float32
    win_buf = min(DIL_WINDOW, PAST_LEN)

    def nrm(k, shape, scale):
        return jax.random.normal(k, shape, f32) * scale

    return {
        'x_prompt': nrm(ks[0], (BATCH, SEQ, D_MODEL), 1.0),
        'x_sample': nrm(ks[1], (DEC_BATCH, DEC_SEQ, D_MODEL), 1.0),
        'state_gla': nrm(ks[2], (DEPTH, DEC_BATCH, GLA_HEADS, GLA_DK, GLA_DV), 1.0),
        'cache_dil_k': nrm(ks[3], (DEPTH, DEC_BATCH, win_buf, DIL_HEADS, HEAD_DIM), 1.0),
        'cache_dil_v': nrm(ks[4], (DEPTH, DEC_BATCH, win_buf, DIL_HEADS, HEAD_DIM), 1.0),
        'state_ffn_conv': nrm(ks[5], (DEPTH, DEC_BATCH, CONV_W - 1, 2 * D_FF), 1.0),
        'norm_mix': 1.0 + nrm(ks[6], (DEPTH, D_MODEL), 0.02),
        'w_in': nrm(ks[7], (DEPTH, D_MODEL, N_IN), D_MODEL ** -0.5),
        'w_gate_up': nrm(ks[8], (DEPTH, GLA_RANK, GLA_QK), GLA_RANK ** -0.5),
        'b_gate': nrm(ks[9], (DEPTH, GLA_QK), 0.1),
        'gla_norm': 1.0 + nrm(ks[10], (DEPTH, GLA_DV), 0.02),
        'w_out': nrm(ks[11], (DEPTH, MIX_WIDTH, D_MODEL), MIX_WIDTH ** -0.5),
        'norm_ffn': 1.0 + nrm(ks[12], (DEPTH, D_MODEL), 0.02),
        'w_ffn_up': nrm(ks[13], (DEPTH, D_MODEL, 2 * D_FF), D_MODEL ** -0.5),
        'ffn_conv_w': nrm(ks[14], (DEPTH, CONV_W, 2 * D_FF), CONV_W ** -0.5),
        'ffn_conv_b': nrm(ks[15], (DEPTH, 2 * D_FF), 0.02),
        'w_ffn_down': nrm(ks[16], (DEPTH, D_FF, D_MODEL), D_FF ** -0.5),
        'norm_final': 1.0 + nrm(ks[17], (D_MODEL,), 0.02),
    }


def reference(x_prompt, x_sample, state_gla, cache_dil_k, cache_dil_v, state_ffn_conv,
              norm_mix, w_in, w_gate_up, b_gate, gla_norm, w_out,
              norm_ffn, w_ffn_up, ffn_conv_w, ffn_conv_b, w_ffn_down, norm_final):
    bp, tp, _ = x_prompt.shape
    ts = x_sample.shape[1]
    pos_p = jnp.arange(tp)
    pos_s = PAST_LEN + jnp.arange(ts)
    hp, hs = x_prompt, x_sample
    gla_p, gla_s, kp_l, ks_l, vp_l, vs_l, cp_l, cs_l = [], [], [], [], [], [], [], []
    for l in range(DEPTH):
        zero_gla = jnp.zeros((bp, GLA_HEADS, GLA_DK, GLA_DV), jnp.float32)
        empty_kv = jnp.zeros((bp, 0, DIL_HEADS, HEAD_DIM), x_prompt.dtype)
        a_p, sg_p, k_p, v_p = mixer_block(rmsnorm(hp, norm_mix[l]), pos_p, zero_gla, empty_kv, empty_kv,
                                          w_in[l], w_gate_up[l], b_gate[l], gla_norm[l], w_out[l])
        a_s, sg_s, k_s, v_s = mixer_block(rmsnorm(hs, norm_mix[l]), pos_s, state_gla[l], cache_dil_k[l],
                                          cache_dil_v[l], w_in[l], w_gate_up[l], b_gate[l], gla_norm[l], w_out[l])
        hp = hp + a_p
        hs = hs + a_s
        zero_conv = jnp.zeros((bp, CONV_W - 1, 2 * D_FF), x_prompt.dtype)
        f_p, c_p = conv_ffn(rmsnorm(hp, norm_ffn[l]), zero_conv,
                            w_ffn_up[l], ffn_conv_w[l], ffn_conv_b[l], w_ffn_down[l])
        f_s, c_s = conv_ffn(rmsnorm(hs, norm_ffn[l]), state_ffn_conv[l],
                            w_ffn_up[l], ffn_conv_w[l], ffn_conv_b[l], w_ffn_down[l])
        hp = hp + f_p
        hs = hs + f_s
        gla_p.append(sg_p); gla_s.append(sg_s)
        kp_l.append(k_p); ks_l.append(k_s)
        vp_l.append(v_p); vs_l.append(v_s)
        cp_l.append(c_p); cs_l.append(c_s)
    y_prompt = rmsnorm(hp, norm_final)
    y_sample = rmsnorm(hs, norm_final)
    new_gla_p = jnp.stack(gla_p, 0)
    new_gla_s = jnp.stack(gla_s, 0)
    new_k_p = jnp.stack(kp_l, 0)
    new_k_s = jnp.stack(ks_l, 0)
    new_v_p = jnp.stack(vp_l, 0)
    new_v_s = jnp.stack(vs_l, 0)
    new_conv_p = jnp.stack(cp_l, 0)
    new_conv_s = jnp.stack(cs_l, 0)
    return (y_prompt, y_sample, new_gla_p, new_gla_s, new_k_p, new_k_s, new_v_p, new_v_s, new_conv_p, new_conv_s)
```

```python
import functools

import jax
import jax.numpy as jnp
from jax import lax
from jax.experimental import pallas as pl
from jax.experimental.pallas import tpu as pltpu

F32 = jnp.float32
BF16 = jnp.bfloat16

D_MODEL = 2048
HEAD_DIM = 64
GLA_HEADS = 10
GLA_DK = 64
GLA_DV = 128
GLA_RANK = 16
GLA_TAU = 16.0
GLA_CHUNK = 64
DIL_HEADS = 12
DIL_PAIRS = ((128, 1), (512, 4), (2048, 16))
DIL_WINDOW = 2048
ROPE_THETA = 10000.0
D_FF = 5504
PAST_LEN = 8192
CONV_W = 3
EPS = 1e-6

GLA_QK = GLA_HEADS * GLA_DK
GLA_VW = GLA_HEADS * GLA_DV
DIL_W = DIL_HEADS * HEAD_DIM

LANE = 128
SUBLANE = 8
VMEM_LIMIT = 56 * 1024 * 1024

Z_PAD = LANE
N_IN_P = 2 * GLA_QK + 2 * GLA_VW + 3 * DIL_W + Z_PAD
COL_Z = N_IN_P - Z_PAD
D_FF_P = 5632
FF_TN = 512
FF_NT = D_FF_P // FF_TN

NT_DIMS = (((1,), (1,)), ((), ()))


def _params(sem, vmem=VMEM_LIMIT):
    return pltpu.CompilerParams(dimension_semantics=sem, vmem_limit_bytes=vmem)


def _rms_matmul_kernel(x_ref, g_ref, w_ref, o_ref, xn_ref):
    @pl.when(pl.program_id(1) == 0)
    def _():
        x = x_ref[...]
        ms = jnp.mean(x * x, axis=-1, keepdims=True)
        xn_ref[...] = (x * lax.rsqrt(ms + EPS) * g_ref[...]).astype(BF16)

    o_ref[...] = jnp.dot(xn_ref[...], w_ref[...], preferred_element_type=F32)


def _rms_matmul(x, g, w, tm, tn):
    m, d = x.shape
    n = w.shape[1]
    return pl.pallas_call(
        _rms_matmul_kernel,
        grid=(m // tm, n // tn),
        in_specs=[
            pl.BlockSpec((tm, d), lambda i, j: (i, 0)),
            pl.BlockSpec((1, d), lambda i, j: (0, 0)),
            pl.BlockSpec((d, tn), lambda i, j: (0, j)),
        ],
        out_specs=pl.BlockSpec((tm, tn), lambda i, j: (i, j)),
        out_shape=jax.ShapeDtypeStruct((m, n), F32),
        scratch_shapes=[pltpu.VMEM((tm, d), BF16)],
        compiler_params=_params(("arbitrary", "arbitrary")),
        name="rms_matmul",
    )(x, g.reshape(1, d), w)


def _rope_kernel(q_ref, k_ref, v_ref, cos_ref, sin_ref, qo_ref, ko_ref, kb_ref, vb_ref):
    reps = DIL_W // LANE
    cos = jnp.concatenate([cos_ref[...]] * reps, axis=1)
    sin = jnp.concatenate([sin_ref[...]] * reps, axis=1)
    lane = lax.broadcasted_iota(jnp.int32, (1, DIL_W), 1)
    first_half = (lane % HEAD_DIM) < (HEAD_DIM // 2)

    def rot(x):
        partner = jnp.where(first_half,
                            pltpu.roll(x, DIL_W - HEAD_DIM // 2, axis=1),
                            pltpu.roll(x, HEAD_DIM // 2, axis=1))
        return x * cos + partner * sin

    qo_ref[...] = (rot(q_ref[...]) * (HEAD_DIM ** -0.5)).astype(BF16)
    k = rot(k_ref[...])
    ko_ref[...] = k
    kb_ref[...] = k.astype(BF16)
    vb_ref[...] = v_ref[...].astype(BF16)


def _rope(proj, cos_t, sin_t, tm):
    m = proj.shape[0]
    qb = (2 * GLA_QK + 2 * GLA_VW) // DIL_W
    blk = lambda c: pl.BlockSpec((tm, DIL_W), lambda i, c=c: (i, c))
    tab = pl.BlockSpec((tm, LANE), lambda i: (i, 0))
    out = pl.BlockSpec((tm, DIL_W), lambda i: (i, 0))
    return pl.pallas_call(
        _rope_kernel,
        grid=(m // tm,),
        in_specs=[blk(qb), blk(qb + 1), blk(qb + 2), tab, tab],
        out_specs=[out, out, out, out],
        out_shape=[jax.ShapeDtypeStruct((m, DIL_W), BF16),
                   jax.ShapeDtypeStruct((m, DIL_W), F32),
                   jax.ShapeDtypeStruct((m, DIL_W), BF16),
                   jax.ShapeDtypeStruct((m, DIL_W), BF16)],
        compiler_params=_params(("arbitrary",)),
        name="rope",
    )(proj, proj, proj, cos_t, sin_t)


def _rope_tables(pos):
    half = HEAD_DIM // 2
    inv_freq = ROPE_THETA ** (-2.0 * jnp.arange(half, dtype=F32) / HEAD_DIM)
    ang = pos.astype(F32)[:, None] * inv_freq[None, :]
    cos = jnp.cos(ang)
    sin = jnp.sin(ang)
    cos_t = jnp.concatenate([cos, cos, cos, cos], axis=1)
    sin_t = jnp.concatenate([-sin, sin, -sin, sin], axis=1)
    return cos_t, sin_t


def _gla_kernel(q_ref, k_ref, v_ref, r_ref, z_ref, wg_ref, bg_ref, gn_ref, s0_ref,
                o_ref, sfin_ref, st_ref, *, chunk, n_chunks, t_valid):
    i = pl.program_id(1)

    @pl.when(i == 0)
    def _():
        for h in range(GLA_HEADS):
            st_ref[h] = s0_ref[h].T

    row = lax.broadcasted_iota(jnp.int32, (chunk, chunk), 0)
    col = lax.broadcasted_iota(jnp.int32, (chunk, chunk), 1)
    causal = row >= col
    tril = causal.astype(F32)
    t_in = lax.broadcasted_iota(jnp.int32, (chunk, 1), 0)

    for c in range(n_chunks):
        sl = pl.ds(c * chunk, chunk)
        x = jnp.dot(z_ref[sl, :], wg_ref[...], precision=lax.Precision.HIGHEST,
                    preferred_element_type=F32) + bg_ref[...]
        log_a = -(jnp.maximum(-x, 0.0) + jnp.log1p(jnp.exp(-jnp.abs(x)))) / GLA_TAU
        t_abs = (i * n_chunks + c) * chunk + t_in
        log_a = jnp.where(t_abs < t_valid, log_a, 0.0)
        cum = jnp.dot(tril, log_a, precision=lax.Precision.HIGHEST, preferred_element_type=F32)
        last = cum[chunk - 1:chunk, :]
        q = q_ref[sl, :] * (GLA_DK ** -0.5)
        k = k_ref[sl, :]
        q_dec = q * jnp.exp(cum)
        k_inv = k * jnp.exp(-cum)
        k_end = k * jnp.exp(last - cum)
        a_end = jnp.exp(last)
        for h in range(GLA_HEADS):
            ks = slice(h * GLA_DK, (h + 1) * GLA_DK)
            vs = slice(h * GLA_DV, (h + 1) * GLA_DV)
            qh = q_dec[:, ks].astype(BF16)
            v_f = v_ref[sl, vs]
            vh = v_f.astype(BF16)
            scores = lax.dot_general(qh, k_inv[:, ks].astype(BF16), NT_DIMS, preferred_element_type=F32)
            scores = jnp.where(causal, scores, 0.0)
            st = st_ref[h]
            o = (jnp.dot(scores.astype(BF16), vh, preferred_element_type=F32)
                 + lax.dot_general(qh, st.astype(BF16), NT_DIMS, preferred_element_type=F32))
            st_ref[h] = st * a_end[:, ks] + jnp.dot(v_f.T.astype(BF16), k_end[:, ks].astype(BF16),
                                                    preferred_element_type=F32)
            ms = jnp.mean(o * o, axis=-1, keepdims=True)
            on = o * lax.rsqrt(ms + EPS) * gn_ref[...]
            r = r_ref[sl, vs]
            gate = r / (1.0 + jnp.exp(-r))
            o_ref[sl, vs] = (on * gate).astype(BF16)

    @pl.when(i == pl.num_programs(1) - 1)
    def _():
        for h in range(GLA_HEADS):
            sfin_ref[h] = st_ref[h].T


def _gla(src, z_blk, wg, bg, gn, s0, *, chunk, n_chunks, t_valid):
    b, t, _ = src.shape
    tb = chunk * n_chunks
    col = lambda w, c: pl.BlockSpec((None, tb, w), lambda bi, i, c=c: (bi, i, c))
    full = lambda shape: pl.BlockSpec(shape, lambda bi, i: (0,) * len(shape))
    state = pl.BlockSpec((None, GLA_HEADS, GLA_DK, GLA_DV), lambda bi, i: (bi, 0, 0, 0))
    kern = functools.partial(_gla_kernel, chunk=chunk, n_chunks=n_chunks, t_valid=t_valid)
    return pl.pallas_call(
        kern,
        grid=(b, t // tb),
        in_specs=[col(GLA_QK, 0), col(GLA_QK, 1), col(GLA_VW, 1), col(GLA_VW, 2), col(Z_PAD, z_blk),
                  full((Z_PAD, GLA_QK)), full((1, GLA_QK)), full((1, GLA_DV)), state],
        out_specs=[pl.BlockSpec((None, tb, GLA_VW), lambda bi, i: (bi, i, 0)), state],
        out_shape=[jax.ShapeDtypeStruct((b, t, GLA_VW), BF16),
                   jax.ShapeDtypeStruct((b, GLA_HEADS, GLA_DK, GLA_DV), F32)],
        scratch_shapes=[pltpu.VMEM((GLA_HEADS, GLA_DV, GLA_DK), F32)],
        compiler_params=_params(("arbitrary", "arbitrary")),
        name="gla",
    )(src, src, src, src, src, wg, bg, gn, s0)


QB = 128


def _dil_branch_kernel(*refs, first, final):
    if first:
        q_ref, kp_ref, kc_ref, vp_ref, vc_ref = refs[:5]
        outs = refs[5:]
    else:
        q_ref, kp_ref, kc_ref, vp_ref, vc_ref, num_ref, m_ref, d_ref = refs[:8]
        outs = refs[8:]
    i = pl.program_id(2)
    k2 = jnp.concatenate([kp_ref[...], kc_ref[...]], axis=0)
    v2 = jnp.concatenate([vp_ref[...], vc_ref[...]], axis=0)
    q = q_ref[...]
    qi = QB + lax.broadcasted_iota(jnp.int32, (QB, 1), 0)
    kj = lax.broadcasted_iota(jnp.int32, (1, 2 * QB), 1)
    off = qi - kj
    valid = (off >= 0) & (off <= QB) & ((kj >= QB) | (i > 0))
    lane = lax.broadcasted_iota(jnp.int32, (1, LANE), 1)
    lo = lane < HEAD_DIM
    m_tile = jnp.zeros((QB, LANE), F32)
    d_tile = jnp.zeros((QB, LANE), F32)
    if not first:
        m_in = m_ref[...]
        d_in = d_ref[...]
    pairs = []
    for p in range(DIL_HEADS // 2):
        cs = slice(p * LANE, (p + 1) * LANE)
        qp, kp, vp = q[:, cs], k2[:, cs], v2[:, cs]
        pv, al, dn = [], [], []
        for hh in range(2):
            h = 2 * p + hh
            sel = lo if hh == 0 else jnp.logical_not(lo)
            qm = jnp.where(sel, qp, jnp.zeros_like(qp))
            s = lax.dot_general(qm, kp, NT_DIMS, preferred_element_type=F32)
            s = jnp.where(valid, s, -jnp.inf)
            m_new = jnp.max(s, axis=-1, keepdims=True)
            if not first:
                m_old = jnp.sum(jnp.where(lane == h, m_in, 0.0), axis=-1, keepdims=True)
                d_old = jnp.sum(jnp.where(lane == h, d_in, 0.0), axis=-1, keepdims=True)
                m_new = jnp.maximum(m_new, m_old)
            pr = jnp.exp(s - m_new)
            d_new = jnp.sum(pr, axis=-1, keepdims=True)
            if not first:
                alpha = jnp.exp(m_old - m_new)
                d_new = d_new + d_old * alpha
                al.append(alpha)
            pv.append(jnp.dot(pr.astype(BF16), vp, preferred_element_type=F32))
            dn.append(d_new)
            m_tile = m_tile + jnp.where(lane == h, m_new, 0.0)
            d_tile = d_tile + jnp.where(lane == h, d_new, 0.0)
        num = jnp.where(lo, pv[0], pv[1])
        if not first:
            num = num + num_ref[:, cs] * jnp.where(lo, al[0], al[1])
        if final:
            num = num / jnp.where(lo, dn[0], dn[1])
        pairs.append(num)
    res = jnp.concatenate(pairs, axis=1)
    if final:
        outs[0][...] = res.astype(BF16)
    else:
        outs[0][...] = res
        outs[1][...] = m_tile
        outs[2][...] = d_tile


def _dil_branch(q, k, v, state, dilation, *, first, final):
    b, t, _ = q.shape
    r = dilation
    ts = t // r
    nq = ts // QB
    view = lambda a: a.reshape(b, ts, r * a.shape[-1])
    cur = lambda w: pl.BlockSpec((None, QB, w), lambda bi, c, i: (bi, i, c))
    prev = lambda w: pl.BlockSpec((None, QB, w), lambda bi, c, i: (bi, jnp.maximum(i - 1, 0), c))
    args = [view(q), view(k), view(k), view(v), view(v)]
    in_specs = [cur(DIL_W), prev(DIL_W), cur(DIL_W), prev(DIL_W), cur(DIL_W)]
    if not first:
        args += [view(a) for a in state]
        in_specs += [cur(DIL_W), cur(LANE), cur(LANE)]
    if final:
        out_specs = [cur(DIL_W)]
        out_shape = [jax.ShapeDtypeStruct((b, ts, r * DIL_W), BF16)]
    else:
        out_specs = [cur(DIL_W), cur(LANE), cur(LANE)]
        out_shape = [jax.ShapeDtypeStruct((b, ts, r * DIL_W), F32),
                     jax.ShapeDtypeStruct((b, ts, r * LANE), F32),
                     jax.ShapeDtypeStruct((b, ts, r * LANE), F32)]
    outs = pl.pallas_call(
        functools.partial(_dil_branch_kernel, first=first, final=final),
        grid=(b, r, nq),
        in_specs=in_specs,
        out_specs=out_specs,
        out_shape=out_shape,
        compiler_params=_params(("arbitrary", "arbitrary", "arbitrary")),
        name=f"dil_branch_r{r}",
    )(*args)
    return [o.reshape(b, t, o.shape[-1] // r) for o in outs]


SAMPLE_HB = 6
SAMPLE_W = SAMPLE_HB * HEAD_DIM
ROLL_ROWS = 256


def _dil_sample_kernel(q_ref, kn_ref, vn_ref, kc_ref, vc_ref, o_ref, ko_ref, vo_ref, *, t_new, hist):
    lane_head = lax.broadcasted_iota(jnp.int32, (1, SAMPLE_W), 1) // HEAD_DIM
    row8 = lax.broadcasted_iota(jnp.int32, (SUBLANE, 1), 0)
    q = q_ref[...].astype(F32)
    qbd = jnp.concatenate([jnp.where(row8 == lane_head, q[t:t + 1, :], 0.0) for t in range(t_new)], axis=0)
    nr = t_new * SUBLANE
    t_row = lax.broadcasted_iota(jnp.int32, (nr, 1), 0) // SUBLANE

    def weight(off, ok):
        w = jnp.zeros(off.shape, F32)
        for win, dil in DIL_PAIRS:
            w = w + jnp.where(ok & (off >= 0) & (off <= win) & (off % dil == 0), 1.0, 0.0)
        return w

    kc = kc_ref[...]
    vc = vc_ref[...]
    s_c = lax.dot_general(qbd.astype(BF16), kc.astype(BF16), NT_DIMS, preferred_element_type=F32)
    j = lax.broadcasted_iota(jnp.int32, (1, hist), 1)
    w_c = weight(hist + t_row - j, j >= 0)
    s_c = jnp.where(w_c > 0, s_c, -jnp.inf)
    kn = kn_ref[...]
    vn = vn_ref[...]
    s_n, w_n = [], []
    for t in range(t_new):
        s_t = jnp.sum(qbd * kn[t:t + 1, :], axis=-1, keepdims=True)
        w_t = weight(t_row - t, t_row >= 0)
        s_n.append(jnp.where(w_t > 0, s_t, -jnp.inf))
        w_n.append(w_t)
    m = jnp.max(s_c, axis=-1, keepdims=True)
    for s_t in s_n:
        m = jnp.maximum(m, s_t)
    p_c = w_c * jnp.exp(s_c - m)
    den = jnp.sum(p_c, axis=-1, keepdims=True)
    num = jnp.dot(p_c.astype(BF16), vc.astype(BF16), preferred_element_type=F32)
    for t in range(t_new):
        p_t = w_n[t] * jnp.exp(s_n[t] - m)
        den = den + p_t
        num = num + p_t * vn[t:t + 1, :]
    row_head = lax.broadcasted_iota(jnp.int32, (nr, 1), 0) % SUBLANE
    sel = jnp.where(row_head == lane_head, num / den, 0.0)
    o_ref[...] = jnp.sum(sel.reshape(t_new, SUBLANE, SAMPLE_W), axis=1).astype(BF16)

    new_rows = SUBLANE - t_new
    for src, new, dst in ((kc_ref, kn, ko_ref), (vc_ref, vn, vo_ref)):
        for c in range(hist // ROLL_ROWS - 1):
            blk = src[pl.ds(c * ROLL_ROWS, ROLL_ROWS + SUBLANE), :]
            dst[pl.ds(c * ROLL_ROWS, ROLL_ROWS), :] = pltpu.roll(
                blk, ROLL_ROWS + SUBLANE - t_new, axis=0)[:ROLL_ROWS]
        base = hist - ROLL_ROWS - SUBLANE
        blk = src[pl.ds(base, ROLL_ROWS + SUBLANE), :]
        blk = pltpu.roll(blk, ROLL_ROWS + SUBLANE - t_new, axis=0)
        dst[pl.ds(base + SUBLANE, ROLL_ROWS - SUBLANE), :] = blk[SUBLANE:ROLL_ROWS]
        tail = jnp.where(row8 < new_rows, blk[ROLL_ROWS:], pltpu.roll(new, new_rows, axis=0))
        dst[pl.ds(hist - SUBLANE, SUBLANE), :] = tail


def _dil_sample(q8, kn8, vn8, cache_k, cache_v, t_new):
    b, hist, _ = cache_k.shape
    small = pl.BlockSpec((None, SUBLANE, SAMPLE_W), lambda bi, c: (bi, 0, c))
    big = pl.BlockSpec((None, hist, SAMPLE_W), lambda bi, c: (bi, 0, c))
    return pl.pallas_call(
        functools.partial(_dil_sample_kernel, t_new=t_new, hist=hist),
        grid=(b, DIL_W // SAMPLE_W),
        in_specs=[small, small, small, big, big],
        out_specs=[pl.BlockSpec((None, t_new, SAMPLE_W), lambda bi, c: (bi, 0, c)), big, big],
        out_shape=[jax.ShapeDtypeStruct((b, t_new, DIL_W), BF16),
                   jax.ShapeDtypeStruct((b, hist, DIL_W), F32),
                   jax.ShapeDtypeStruct((b, hist, DIL_W), F32)],
        compiler_params=_params(("arbitrary", "arbitrary")),
        name="dil_sample",
    )(q8, kn8, vn8, cache_k, cache_v)


def _out_proj_kernel(oa_ref, ob_ref, wa_ref, wb_ref, x_ref, g_ref, h_ref, xn_ref):
    y = (jnp.dot(oa_ref[...], wa_ref[...], preferred_element_type=F32)
         + jnp.dot(ob_ref[...], wb_ref[...], preferred_element_type=F32))
    h = x_ref[...] + y
    h_ref[...] = h
    ms = jnp.mean(h * h, axis=-1, keepdims=True)
    xn_ref[...] = (h * lax.rsqrt(ms + EPS) * g_ref[...]).astype(BF16)


def _out_proj(oa, ob, wa, wb, x, g, tm):
    m, d = x.shape
    row = lambda w: pl.BlockSpec((tm, w), lambda i: (i, 0))
    full = lambda a: pl.BlockSpec(a.shape, lambda i: (0, 0))
    g2 = g.reshape(1, d)
    return pl.pallas_call(
        _out_proj_kernel,
        grid=(m // tm,),
        in_specs=[row(GLA_VW), row(DIL_W), full(wa), full(wb), row(d), full(g2)],
        out_specs=[row(d), row(d)],
        out_shape=[jax.ShapeDtypeStruct((m, d), F32), jax.ShapeDtypeStruct((m, d), BF16)],
        compiler_params=_params(("arbitrary",)),
        name="out_proj",
    )(oa, ob, wa, wb, x, g2)


def _ffn_up_kernel(*refs, tiles_per_seq, t_seq, has_hist):
    if has_hist:
        (x_ref, wu_ref, wg_ref, cwu_ref, cwg_ref, cbu_ref, cbg_ref,
         h1u_ref, h1g_ref, h2u_ref, h2g_ref, act_ref, upu_ref, upg_ref) = refs
    else:
        (x_ref, wu_ref, wg_ref, cwu_ref, cwg_ref, cbu_ref, cbg_ref,
         act_ref, tu_ref, tg_ref, cu_ref, cg_ref) = refs
    i = pl.program_id(1)
    x = x_ref[...]
    tm = x.shape[0]
    conv = []
    for idx, (w_ref, cw_ref, cb_ref) in enumerate(((wu_ref, cwu_ref, cbu_ref), (wg_ref, cwg_ref, cbg_ref))):
        up = jnp.dot(x, w_ref[...], preferred_element_type=F32)
        if has_hist:
            t_in = lax.broadcasted_iota(jnp.int32, (tm, 1), 0) % t_seq
            prev1 = jnp.where(t_in >= 1, pltpu.roll(up, 1, axis=0), 0.0) + (h1u_ref, h1g_ref)[idx][...]
            prev2 = jnp.where(t_in >= 2, pltpu.roll(up, 2, axis=0), 0.0) + (h2u_ref, h2g_ref)[idx][...]
            (upu_ref, upg_ref)[idx][...] = up
        else:
            carry_ref = (cu_ref, cg_ref)[idx]

            @pl.when(i % tiles_per_seq == 0)
            def _():
                carry_ref[...] = jnp.zeros_like(carry_ref)

            ext = jnp.concatenate([carry_ref[...], up], axis=0)
            prev1 = pltpu.roll(ext, 1, axis=0)[SUBLANE:]
            prev2 = pltpu.roll(ext, 2, axis=0)[SUBLANE:]
            carry_ref[...] = up[tm - SUBLANE:]
            (tu_ref, tg_ref)[idx][...] = up[tm - SUBLANE:]
        cw = cw_ref[...]
        conv.append(cb_ref[...] + cw[0:1] * prev2 + cw[1:2] * prev1 + cw[2:3] * up)
    u, g = conv
    act_ref[...] = (g / (1.0 + jnp.exp(-g)) * u).astype(BF16)


def _ffn_up(xn, w_up, conv_w, conv_b, tm, *, t_seq, hist=None):
    m, d = xn.shape
    has_hist = hist is not None
    tiles_per_seq = max(t_seq // tm, 1)
    n_seq = m // t_seq
    xs = pl.BlockSpec((tm, d), lambda j, i: (i, 0))
    wu = pl.BlockSpec((d, FF_TN), lambda j, i: (0, j))
    wg = pl.BlockSpec((d, FF_TN), lambda j, i: (0, FF_NT + j))
    cu = lambda r: pl.BlockSpec((r, FF_TN), lambda j, i: (0, j))
    cg = lambda r: pl.BlockSpec((r, FF_TN), lambda j, i: (0, FF_NT + j))
    act = pl.BlockSpec((tm, FF_TN), lambda j, i: (i, j))
    in_specs = [xs, wu, wg, cu(CONV_W), cg(CONV_W), cu(1), cg(1)]
    args = [xn, w_up, w_up, conv_w, conv_w, conv_b, conv_b]
    out_specs = [act]
    out_shape = [jax.ShapeDtypeStruct((m, D_FF_P), BF16)]
    scratch = []
    if has_hist:
        tu = pl.BlockSpec((tm, FF_TN), lambda j, i: (i, j))
        tg = pl.BlockSpec((tm, FF_TN), lambda j, i: (i, FF_NT + j))
        in_specs += [tu, tg, tu, tg]
        args += [hist[0], hist[0], hist[1], hist[1]]
        out_specs += [tu, tu]
        out_shape += [jax.ShapeDtypeStruct((m, D_FF_P), F32)] * 2
    else:
        tail = pl.BlockSpec((None, SUBLANE, FF_TN), lambda j, i: (i // tiles_per_seq, 0, j))
        out_specs += [tail, tail]
        out_shape += [jax.ShapeDtypeStruct((n_seq, SUBLANE, D_FF_P), F32)] * 2
        scratch = [pltpu.VMEM((SUBLANE, FF_TN), F32)] * 2
    return pl.pallas_call(
        functools.partial(_ffn_up_kernel, tiles_per_seq=tiles_per_seq, t_seq=t_seq, has_hist=has_hist),
        grid=(FF_NT, m // tm),
        in_specs=in_specs,
        out_specs=out_specs,
        out_shape=out_shape,
        scratch_shapes=scratch,
        compiler_params=_params(("arbitrary", "arbitrary")),
        name="ffn_up",
    )(*args)


def _ffn_down_kernel(a_ref, w_ref, h_ref, g_ref, y_ref, acc_ref):
    kk = pl.program_id(1)

    @pl.when(kk == 0)
    def _():
        acc_ref[...] = jnp.zeros_like(acc_ref)

    acc_ref[...] += jnp.dot(a_ref[...], w_ref[...], preferred_element_type=F32)

    @pl.when(kk == pl.num_programs(1) - 1)
    def _():
        h = h_ref[...] + acc_ref[...]
        ms = jnp.mean(h * h, axis=-1, keepdims=True)
        y_ref[...] = h * lax.rsqrt(ms + EPS) * g_ref[...]


def _ffn_down(act, w, h, g, tm, tk):
    m, d = h.shape
    kdim = act.shape[1]
    return pl.pallas_call(
        _ffn_down_kernel,
        grid=(m // tm, kdim // tk),
        in_specs=[pl.BlockSpec((tm, tk), lambda i, kk: (i, kk)),
                  pl.BlockSpec((tk, d), lambda i, kk: (kk, 0)),
                  pl.BlockSpec((tm, d), lambda i, kk: (i, 0)),
                  pl.BlockSpec((1, d), lambda i, kk: (0, 0))],
        out_specs=pl.BlockSpec((tm, d), lambda i, kk: (i, 0)),
        out_shape=jax.ShapeDtypeStruct((m, d), F32),
        scratch_shapes=[pltpu.VMEM((tm, d), F32)],
        compiler_params=_params(("arbitrary", "arbitrary")),
        name="ffn_down",
    )(act, w, h, g.reshape(1, d))


def _pad_cols(a, width):
    return jnp.pad(a, ((0, 0), (0, width - a.shape[1])))


def _split_pad_ff(a):
    return jnp.concatenate([_pad_cols(a[:, :D_FF], D_FF_P), _pad_cols(a[:, D_FF:], D_FF_P)], axis=1)


def _unpad_ff(u, g):
    return jnp.concatenate([u[..., :D_FF], g[..., :D_FF]], axis=-1)


def kernel(x_prompt, x_sample, state_gla, cache_dil_k, cache_dil_v, state_ffn_conv, norm_mix, w_in, w_gate_up,
           b_gate, gla_norm, w_out, norm_ffn, w_ffn_up, ffn_conv_w, ffn_conv_b, w_ffn_down, norm_final):
    bp, tp, d = x_prompt.shape
    bs, ts, _ = x_sample.shape
    hist = cache_dil_k.shape[2]
    l = 0

    z0 = 2 * GLA_QK + 2 * GLA_VW
    w_in_p = jnp.concatenate([w_in[l][:, :z0], w_in[l][:, z0 + GLA_RANK:], w_in[l][:, z0:z0 + GLA_RANK],
                              jnp.zeros((d, Z_PAD - GLA_RANK), F32)], axis=1).astype(BF16)
    wg_p = jnp.pad(w_gate_up[l], ((0, Z_PAD - GLA_RANK), (0, 0)))
    bg = b_gate[l].reshape(1, GLA_QK)
    gn = gla_norm[l].reshape(1, GLA_DV)
    wa = w_out[l][:GLA_VW].astype(BF16)
    wb = w_out[l][GLA_VW:].astype(BF16)
    w_up_p = _split_pad_ff(w_ffn_up[l]).astype(BF16)
    cw_p = _split_pad_ff(ffn_conv_w[l])
    cb_p = _split_pad_ff(ffn_conv_b[l].reshape(1, -1))
    w_dn_p = jnp.pad(w_ffn_down[l], ((0, D_FF_P - D_FF), (0, 0))).astype(BF16)

    mp = bp * tp
    xp = x_prompt.reshape(mp, d)
    proj_p = _rms_matmul(xp, norm_mix[l], w_in_p, 1024, 896)
    cos_p, sin_p = _rope_tables(jnp.arange(tp))
    cos_p = jnp.tile(cos_p, (bp, 1))
    sin_p = jnp.tile(sin_p, (bp, 1))
    q_rot, k_rot, k_bf, v_bf = _rope(proj_p, cos_p, sin_p, 512)

    oa_p, gla_p = _gla(proj_p.reshape(bp, tp, N_IN_P), COL_Z // LANE, wg_p, bg, gn,
                       jnp.zeros((bp, GLA_HEADS, GLA_DK, GLA_DV), F32),
                       chunk=GLA_CHUNK, n_chunks=4, t_valid=tp)

    q3, k3, v3 = (a.reshape(bp, tp, DIL_W) for a in (q_rot, k_bf, v_bf))
    state = _dil_branch(q3, k3, v3, None, 16, first=True, final=False)
    state = _dil_branch(q3, k3, v3, state, 4, first=False, final=False)
    (ob_p,) = _dil_branch(q3, k3, v3, state, 1, first=False, final=True)

    h_p, xn_p = _out_proj(oa_p.reshape(mp, GLA_VW), ob_p.reshape(mp, DIL_W), wa, wb, xp, norm_ffn[l], 512)
    act_p, tail_u, tail_g = _ffn_up(xn_p, w_up_p, cw_p, cb_p, 1024, t_seq=tp)
    y_p = _ffn_down(act_p, w_dn_p, h_p, norm_final, 512, 1408)

    buf_p = min(DIL_WINDOW, tp)
    y_prompt = y_p.reshape(bp, tp, d)
    new_gla_p = gla_p[None]
    new_k_p = k_rot.reshape(bp, tp, DIL_HEADS, HEAD_DIM)[None, :, tp - buf_p:]
    v_f32 = proj_p[:, z0 + 2 * DIL_W:z0 + 3 * DIL_W]
    new_v_p = v_f32.reshape(bp, tp, DIL_HEADS, HEAD_DIM)[None, :, tp - buf_p:]
    new_conv_p = _unpad_ff(tail_u[:, SUBLANE - (CONV_W - 1):], tail_g[:, SUBLANE - (CONV_W - 1):])[None]

    ms_ = bs * ts
    xs = x_sample.reshape(ms_, d)
    proj_s = _rms_matmul(xs, norm_mix[l], w_in_p, ms_, 896)
    cos_s, sin_s = _rope_tables(PAST_LEN + jnp.arange(ts))
    cos_s = jnp.tile(cos_s, (bs, 1))
    sin_s = jnp.tile(sin_s, (bs, 1))
    q_rs, k_rs, _, _ = _rope(proj_s, cos_s, sin_s, ms_)

    gla_src = jnp.concatenate([proj_s[:, :z0], proj_s[:, COL_Z:]], axis=1).reshape(bs, ts, z0 + Z_PAD)
    gla_src = jnp.pad(gla_src, ((0, 0), (0, GLA_CHUNK - ts), (0, 0)))
    oa_s, gla_s = _gla(gla_src, z0 // LANE, wg_p, bg, gn, state_gla[l],
                       chunk=GLA_CHUNK, n_chunks=1, t_valid=ts)
    oa_s = oa_s[:, :ts].reshape(ms_, GLA_VW)

    pad8 = lambda a: jnp.pad(a.reshape(bs, ts, DIL_W), ((0, 0), (0, SUBLANE - ts), (0, 0)))
    v_s = proj_s[:, z0 + 2 * DIL_W:z0 + 3 * DIL_W]
    ob_s, new_k_s, new_v_s = _dil_sample(pad8(q_rs), pad8(k_rs), pad8(v_s),
                                         cache_dil_k[l].reshape(bs, hist, DIL_W),
                                         cache_dil_v[l].reshape(bs, hist, DIL_W), ts)

    h_s, xn_s = _out_proj(oa_s, ob_s.reshape(ms_, DIL_W), wa, wb, xs, norm_ffn[l], ms_)
    conv_hist = _split_pad_ff(state_ffn_conv[l].reshape(bs * (CONV_W - 1), 2 * D_FF))
    conv_hist = conv_hist.reshape(bs, CONV_W - 1, 2 * D_FF_P)
    zero_row = jnp.zeros((bs, 1, 2 * D_FF_P), F32)
    h1 = jnp.concatenate([conv_hist[:, 1:2]] + [zero_row] * (ts - 1), axis=1).reshape(ms_, 2 * D_FF_P)
    h2 = jnp.concatenate([conv_hist[:, 0:1], conv_hist[:, 1:2]] + [zero_row] * (ts - 2),
                         axis=1).reshape(ms_, 2 * D_FF_P)
    act_s, up_u, up_g = _ffn_up(xn_s, w_up_p, cw_p, cb_p, ms_, t_seq=ts, hist=(h1, h2))
    y_s = _ffn_down(act_s, w_dn_p, h_s, norm_final, ms_, 1408)

    y_sample = y_s.reshape(bs, ts, d)
    new_gla_s = gla_s[None]
    new_k_s = new_k_s.reshape(bs, hist, DIL_HEADS, HEAD_DIM)[None]
    new_v_s = new_v_s.reshape(bs, hist, DIL_HEADS, HEAD_DIM)[None]
    up_full = _unpad_ff(up_u, up_g).reshape(bs, ts, 2 * D_FF)
    new_conv_s = up_full[:, ts - (CONV_W - 1):][None]

    return (y_prompt, y_sample, new_gla_p, new_gla_s, new_k_p, new_k_s, new_v_p, new_v_s,
            new_conv_p, new_conv_s)
```

```python
import functools

import jax
import jax.numpy as jnp
from jax import lax
from jax.experimental import pallas as pl
from jax.experimental.pallas import tpu as pltpu

F32 = jnp.float32
BF16 = jnp.bfloat16

D_MODEL = 2048
HEAD_DIM = 64
GLA_HEADS = 10
GLA_DK = 64
GLA_DV = 128
GLA_RANK = 16
GLA_TAU = 16.0
GLA_CHUNK = 64
DIL_HEADS = 12
DIL_PAIRS = ((128, 1), (512, 4), (2048, 16))
DIL_WINDOW = 2048
ROPE_THETA = 10000.0
D_FF = 5504
PAST_LEN = 8192
CONV_W = 3
EPS = 1e-6

GLA_QK = GLA_HEADS * GLA_DK
GLA_VW = GLA_HEADS * GLA_DV
DIL_W = DIL_HEADS * HEAD_DIM

LANE = 128
SUBLANE = 8
VMEM_LIMIT = 56 * 1024 * 1024

Z_PAD = LANE
N_IN_P = 2 * GLA_QK + 2 * GLA_VW + 3 * DIL_W + Z_PAD
COL_Z = N_IN_P - Z_PAD
D_FF_P = 5632
FF_TN = 512
FF_NT = D_FF_P // FF_TN

NT_DIMS = (((1,), (1,)), ((), ()))


def _params(sem, vmem=VMEM_LIMIT):
    return pltpu.CompilerParams(dimension_semantics=sem, vmem_limit_bytes=vmem)


def _rms_matmul_kernel(x_ref, g_ref, w_ref, o_ref, xn_ref):
    @pl.when(pl.program_id(1) == 0)
    def _():
        x = x_ref[...]
        ms = jnp.mean(x * x, axis=-1, keepdims=True)
        xn_ref[...] = (x * lax.rsqrt(ms + EPS) * g_ref[...]).astype(BF16)

    o_ref[...] = jnp.dot(xn_ref[...], w_ref[...], preferred_element_type=F32)


def _rms_matmul(x, g, w, tm, tn):
    m, d = x.shape
    n = w.shape[1]
    return pl.pallas_call(
        _rms_matmul_kernel,
        grid=(m // tm, n // tn),
        in_specs=[
            pl.BlockSpec((tm, d), lambda i, j: (i, 0)),
            pl.BlockSpec((1, d), lambda i, j: (0, 0)),
            pl.BlockSpec((d, tn), lambda i, j: (0, j)),
        ],
        out_specs=pl.BlockSpec((tm, tn), lambda i, j: (i, j)),
        out_shape=jax.ShapeDtypeStruct((m, n), F32),
        scratch_shapes=[pltpu.VMEM((tm, d), BF16)],
        compiler_params=_params(("arbitrary", "arbitrary")),
        name="rms_matmul",
    )(x, g.reshape(1, d), w)


CLASS_DILATIONS = tuple(r for _, r in DIL_PAIRS if r > 1)


def _rope_kernel(q_ref, k_ref, v_ref, cos_ref, sin_ref, qo_ref, ko_ref, *rest, by_class):
    reps = DIL_W // LANE
    cos = jnp.concatenate([cos_ref[...]] * reps, axis=1)
    sin = jnp.concatenate([sin_ref[...]] * reps, axis=1)
    lane = lax.broadcasted_iota(jnp.int32, (1, DIL_W), 1)
    first_half = (lane % HEAD_DIM) < (HEAD_DIM // 2)

    def rot(x):
        partner = jnp.where(first_half,
                            pltpu.roll(x, DIL_W - HEAD_DIM // 2, axis=1),
                            pltpu.roll(x, HEAD_DIM // 2, axis=1))
        return x * cos + partner * sin

    q = rot(q_ref[...]) * (HEAD_DIM ** -0.5)
    k = rot(k_ref[...])
    ko_ref[...] = k
    if not by_class:
        qo_ref[...] = q
        return
    qo_ref[...] = q.astype(BF16)
    kb_ref, vb_ref = rest[0], rest[1]
    class_refs = rest[2:2 + 3 * len(CLASS_DILATIONS)]
    sq_ref, sk_ref, sv_ref = rest[2 + 3 * len(CLASS_DILATIONS):]
    v = v_ref[...]
    kb_ref[...] = k.astype(BF16)
    vb_ref[...] = v.astype(BF16)
    for src, val in ((sq_ref, q), (sk_ref, k), (sv_ref, v)):
        for j in range(reps):
            src[j] = val[:, j * LANE:(j + 1) * LANE]
    tm = q.shape[0]
    for di, r in enumerate(CLASS_DILATIONS):
        for src, dst in zip((sq_ref, sk_ref, sv_ref), class_refs[3 * di:3 * di + 3]):
            for c in range(r):
                rows = pl.ds(c, tm // r, stride=r)
                dst[c] = jnp.concatenate([src[j, rows, :] for j in range(reps)], axis=1).astype(BF16)


def _rope(proj, cos_t, sin_t, tm, n_seq=None):
    m = proj.shape[0]
    by_class = n_seq is not None
    qb = (2 * GLA_QK + 2 * GLA_VW) // DIL_W
    blk = lambda c: pl.BlockSpec((tm, DIL_W), lambda i, c=c: (i, c))
    tab = pl.BlockSpec((tm, LANE), lambda i: (i, 0))
    out = pl.BlockSpec((tm, DIL_W), lambda i: (i, 0))
    out_specs = [out, out]
    out_shape = [jax.ShapeDtypeStruct((m, DIL_W), BF16 if by_class else F32),
                 jax.ShapeDtypeStruct((m, DIL_W), F32)]
    scratch = []
    if by_class:
        t_seq = m // n_seq
        tiles = t_seq // tm
        out_specs += [out, out]
        out_shape += [jax.ShapeDtypeStruct((m, DIL_W), BF16)] * 2
        for r in CLASS_DILATIONS:
            spec = pl.BlockSpec((None, r, tm // r, DIL_W), lambda i: (i // tiles, 0, i % tiles, 0))
            out_specs += [spec] * 3
            out_shape += [jax.ShapeDtypeStruct((n_seq, r, t_seq // r, DIL_W), BF16)] * 3
        scratch = [pltpu.VMEM((DIL_W // LANE, tm, LANE), F32)] * 3
    return pl.pallas_call(
        functools.partial(_rope_kernel, by_class=by_class),
        grid=(m // tm,),
        in_specs=[blk(qb), blk(qb + 1), blk(qb + 2), tab, tab],
        out_specs=out_specs,
        out_shape=out_shape,
        scratch_shapes=scratch,
        compiler_params=_params(("arbitrary",)),
        name="rope",
    )(proj, proj, proj, cos_t, sin_t)


def _rope_tables(pos):
    half = HEAD_DIM // 2
    inv_freq = ROPE_THETA ** (-2.0 * jnp.arange(half, dtype=F32) / HEAD_DIM)
    ang = pos.astype(F32)[:, None] * inv_freq[None, :]
    cos = jnp.cos(ang)
    sin = jnp.sin(ang)
    cos_t = jnp.concatenate([cos, cos, cos, cos], axis=1)
    sin_t = jnp.concatenate([-sin, sin, -sin, sin], axis=1)
    return cos_t, sin_t


def _gla_kernel(q_ref, k_ref, v_ref, r_ref, z_ref, wg_ref, bg_ref, gn_ref, s0_ref,
                o_ref, sfin_ref, st_ref, *, chunk, n_chunks, t_valid):
    i = pl.program_id(1)

    @pl.when(i == 0)
    def _():
        for h in range(GLA_HEADS):
            st_ref[h] = s0_ref[h].T

    row = lax.broadcasted_iota(jnp.int32, (chunk, chunk), 0)
    col = lax.broadcasted_iota(jnp.int32, (chunk, chunk), 1)
    causal = row >= col
    tril = causal.astype(F32)
    t_in = lax.broadcasted_iota(jnp.int32, (chunk, 1), 0)

    for c in range(n_chunks):
        sl = pl.ds(c * chunk, chunk)
        x = jnp.dot(z_ref[sl, :], wg_ref[...], precision=lax.Precision.HIGHEST,
                    preferred_element_type=F32) + bg_ref[...]
        log_a = -(jnp.maximum(-x, 0.0) + jnp.log1p(jnp.exp(-jnp.abs(x)))) / GLA_TAU
        t_abs = (i * n_chunks + c) * chunk + t_in
        log_a = jnp.where(t_abs < t_valid, log_a, 0.0)
        cum = jnp.dot(tril, log_a, precision=lax.Precision.HIGHEST, preferred_element_type=F32)
        last = cum[chunk - 1:chunk, :]
        q = q_ref[sl, :] * (GLA_DK ** -0.5)
        k = k_ref[sl, :]
        q_dec = q * jnp.exp(cum)
        k_inv = k * jnp.exp(-cum)
        k_end = k * jnp.exp(last - cum)
        a_end = jnp.exp(last)
        for h in range(GLA_HEADS):
            ks = slice(h * GLA_DK, (h + 1) * GLA_DK)
            vs = slice(h * GLA_DV, (h + 1) * GLA_DV)
            qh = q_dec[:, ks].astype(BF16)
            v_f = v_ref[sl, vs]
            vh = v_f.astype(BF16)
            scores = lax.dot_general(qh, k_inv[:, ks].astype(BF16), NT_DIMS, preferred_element_type=F32)
            scores = jnp.where(causal, scores, 0.0)
            st = st_ref[h]
            o = (jnp.dot(scores.astype(BF16), vh, preferred_element_type=F32)
                 + lax.dot_general(qh, st.astype(BF16), NT_DIMS, preferred_element_type=F32))
            st_ref[h] = st * a_end[:, ks] + jnp.dot(v_f.T.astype(BF16), k_end[:, ks].astype(BF16),
                                                    preferred_element_type=F32)
            ms = jnp.mean(o * o, axis=-1, keepdims=True)
            on = o * lax.rsqrt(ms + EPS) * gn_ref[...]
            r = r_ref[sl, vs]
            gate = r / (1.0 + jnp.exp(-r))
            o_ref[sl, vs] = (on * gate).astype(BF16)

    @pl.when(i == pl.num_programs(1) - 1)
    def _():
        for h in range(GLA_HEADS):
            sfin_ref[h] = st_ref[h].T


def _gla(src, z_blk, wg, bg, gn, s0, *, chunk, n_chunks, t_valid):
    b, t, _ = src.shape
    tb = chunk * n_chunks
    col = lambda w, c: pl.BlockSpec((None, tb, w), lambda bi, i, c=c: (bi, i, c))
    full = lambda shape: pl.BlockSpec(shape, lambda bi, i: (0,) * len(shape))
    state = pl.BlockSpec((None, GLA_HEADS, GLA_DK, GLA_DV), lambda bi, i: (bi, 0, 0, 0))
    kern = functools.partial(_gla_kernel, chunk=chunk, n_chunks=n_chunks, t_valid=t_valid)
    return pl.pallas_call(
        kern,
        grid=(b, t // tb),
        in_specs=[col(GLA_QK, 0), col(GLA_QK, 1), col(GLA_VW, 1), col(GLA_VW, 2), col(Z_PAD, z_blk),
                  full((Z_PAD, GLA_QK)), full((1, GLA_QK)), full((1, GLA_DV)), state],
        out_specs=[pl.BlockSpec((None, tb, GLA_VW), lambda bi, i: (bi, i, 0)), state],
        out_shape=[jax.ShapeDtypeStruct((b, t, GLA_VW), BF16),
                   jax.ShapeDtypeStruct((b, GLA_HEADS, GLA_DK, GLA_DV), F32)],
        scratch_shapes=[pltpu.VMEM((GLA_HEADS, GLA_DV, GLA_DK), F32)],
        compiler_params=_params(("arbitrary", "arbitrary")),
        name="gla",
    )(src, src, src, src, src, wg, bg, gn, s0)


QB = 128


def _dil_branch_kernel(*refs, first, final, dilation):
    if first:
        q_ref, kp_ref, kc_ref, vp_ref, vc_ref = refs[:5]
        outs = refs[5:]
    else:
        q_ref, kp_ref, kc_ref, vp_ref, vc_ref, num_ref, m_ref, d_ref = refs[:8]
        outs = refs[8:]
    i = pl.program_id(1)
    rows = pl.ds(pl.program_id(2), QB, stride=dilation) if dilation > 1 else slice(None)
    k2 = jnp.concatenate([kp_ref[...], kc_ref[...]], axis=0)
    v2 = jnp.concatenate([vp_ref[...], vc_ref[...]], axis=0)
    q = q_ref[...]
    qi = QB + lax.broadcasted_iota(jnp.int32, (QB, 1), 0)
    kj = lax.broadcasted_iota(jnp.int32, (1, 2 * QB), 1)
    off = qi - kj
    valid = (off >= 0) & (off <= QB) & ((kj >= QB) | (i > 0))
    lane = lax.broadcasted_iota(jnp.int32, (1, LANE), 1)
    lo = lane < HEAD_DIM
    m_tile = jnp.zeros((QB, LANE), F32)
    d_tile = jnp.zeros((QB, LANE), F32)
    if not first:
        m_in = m_ref[rows, :]
        d_in = d_ref[rows, :]
    pairs = []
    for p in range(DIL_HEADS // 2):
        cs = slice(p * LANE, (p + 1) * LANE)
        qp, kp, vp = q[:, cs], k2[:, cs], v2[:, cs]
        pv, al, dn = [], [], []
        for hh in range(2):
            h = 2 * p + hh
            sel = lo if hh == 0 else jnp.logical_not(lo)
            qm = jnp.where(sel, qp, jnp.zeros_like(qp))
            s = lax.dot_general(qm, kp, NT_DIMS, preferred_element_type=F32)
            s = jnp.where(valid, s, -jnp.inf)
            m_new = jnp.max(s, axis=-1, keepdims=True)
            if not first:
                m_old = jnp.sum(jnp.where(lane == h, m_in, 0.0), axis=-1, keepdims=True)
                d_old = jnp.sum(jnp.where(lane == h, d_in, 0.0), axis=-1, keepdims=True)
                m_new = jnp.maximum(m_new, m_old)
            pr = jnp.exp(s - m_new)
            d_new = jnp.sum(pr, axis=-1, keepdims=True)
            if not first:
                alpha = jnp.exp(m_old - m_new)
                d_new = d_new + d_old * alpha
                al.append(alpha)
            pv.append(jnp.dot(pr.astype(BF16), vp, preferred_element_type=F32))
            dn.append(d_new)
            m_tile = m_tile + jnp.where(lane == h, m_new, 0.0)
            d_tile = d_tile + jnp.where(lane == h, d_new, 0.0)
        num = jnp.where(lo, pv[0], pv[1])
        if not first:
            num = num + num_ref[p, rows, :] * jnp.where(lo, al[0], al[1])
        if final:
            num = num / jnp.where(lo, dn[0], dn[1])
        else:
            outs[0][p, rows, :] = num
        pairs.append(num)
    if final:
        outs[0][...] = jnp.concatenate(pairs, axis=1).astype(BF16)
    else:
        outs[1][rows, :] = m_tile
        outs[2][rows, :] = d_tile


def _dil_branch(q, k, v, state, *, first, final):
    b, r, ts, _ = q.shape
    t = r * ts
    nq = ts // QB
    cur = pl.BlockSpec((None, None, QB, DIL_W), lambda bi, i, c: (bi, c, i, 0))
    prev = pl.BlockSpec((None, None, QB, DIL_W), lambda bi, i, c: (bi, c, jnp.maximum(i - 1, 0), 0))
    nat = lambda w: pl.BlockSpec((None, r * QB, w), lambda bi, i, c: (bi, i, 0))
    nat_num = pl.BlockSpec((None, DIL_W // LANE, r * QB, LANE), lambda bi, i, c: (bi, 0, i, 0))
    args = [q, k, k, v, v]
    in_specs = [cur, prev, cur, prev, cur]
    if not first:
        args += list(state)
        in_specs += [nat_num, nat(LANE), nat(LANE)]
    if final:
        assert r == 1
        out_specs = [nat(DIL_W)]
        out_shape = [jax.ShapeDtypeStruct((b, t, DIL_W), BF16)]
    else:
        out_specs = [nat_num, nat(LANE), nat(LANE)]
        out_shape = [jax.ShapeDtypeStruct((b, DIL_W // LANE, t, LANE), F32),
                     jax.ShapeDtypeStruct((b, t, LANE), F32),
                     jax.ShapeDtypeStruct((b, t, LANE), F32)]
    return pl.pallas_call(
        functools.partial(_dil_branch_kernel, first=first, final=final, dilation=r),
        grid=(b, nq, r),
        in_specs=in_specs,
        out_specs=out_specs,
        out_shape=out_shape,
        compiler_params=_params(("arbitrary", "arbitrary", "arbitrary")),
        name=f"dil_branch_r{r}",
    )(*args)


SAMPLE_HB = 6
SAMPLE_ROWS = 16


def _dil_sample_kernel(q_ref, kn_ref, vn_ref, knt_ref, vnt_ref, kc_ref, vc_ref, o_ref, ko_ref, vo_ref,
                       *, t_new, hist):
    t_row = lax.broadcasted_iota(jnp.int32, (SAMPLE_ROWS, 1), 0)
    j = lax.broadcasted_iota(jnp.int32, (1, hist), 1)
    lane = lax.broadcasted_iota(jnp.int32, (1, LANE), 1)
    keep = lane < LANE - t_new

    def weight(off):
        w = jnp.zeros(off.shape, F32)
        for win, dil in DIL_PAIRS:
            w = w + jnp.where((off >= 0) & (off <= win) & (off % dil == 0), 1.0, 0.0)
        return w

    w_c = weight(hist + t_row - j)
    w_n = [weight(t_row - t) for t in range(t_new)]
    outs = []
    for h in range(SAMPLE_HB):
        hs = slice(h * HEAD_DIM, (h + 1) * HEAD_DIM)
        qh = q_ref[:, hs]
        s_c = jnp.dot(qh.astype(BF16), kc_ref[h].astype(BF16), preferred_element_type=F32)
        s_c = jnp.where(w_c > 0, s_c, -jnp.inf)
        m = jnp.max(s_c, axis=-1, keepdims=True)
        s_n = []
        for t in range(t_new):
            s_t = jnp.sum(qh * kn_ref[t:t + 1, hs], axis=-1, keepdims=True)
            s_t = jnp.where(w_n[t] > 0, s_t, -jnp.inf)
            s_n.append(s_t)
            m = jnp.maximum(m, s_t)
        p_c = w_c * jnp.exp(s_c - m)
        den = jnp.sum(p_c, axis=-1, keepdims=True)
        num = lax.dot_general(p_c.astype(BF16), vc_ref[h].astype(BF16), NT_DIMS, preferred_element_type=F32)
        for t in range(t_new):
            p_t = w_n[t] * jnp.exp(s_n[t] - m)
            den = den + p_t
            num = num + p_t * vn_ref[t:t + 1, hs]
        outs.append(num / den)
        for src, new_ref, dst in ((kc_ref, knt_ref, ko_ref), (vc_ref, vnt_ref, vo_ref)):
            nxt = pltpu.roll(src[h, :, 0:LANE], LANE - t_new, axis=1)
            for c in range(hist // LANE):
                cur_t = nxt
                if c + 1 < hist // LANE:
                    nxt = pltpu.roll(src[h, :, (c + 1) * LANE:(c + 2) * LANE], LANE - t_new, axis=1)
                else:
                    nxt = new_ref[h]
                dst[h, :, c * LANE:(c + 1) * LANE] = jnp.where(keep, cur_t, nxt)
    o_ref[...] = jnp.concatenate(outs, axis=1)


def _dil_sample(q, kn, vn, knt, vnt, cache_kt, cache_vt, t_new):
    b, nh, _, hist = cache_kt.shape
    w = SAMPLE_HB * HEAD_DIM
    small = pl.BlockSpec((None, SAMPLE_ROWS, w), lambda bi, c: (bi, 0, c))
    newt = pl.BlockSpec((None, SAMPLE_HB, HEAD_DIM, LANE), lambda bi, c: (bi, c, 0, 0))
    big = pl.BlockSpec((None, SAMPLE_HB, HEAD_DIM, hist), lambda bi, c: (bi, c, 0, 0))
    return pl.pallas_call(
        functools.partial(_dil_sample_kernel, t_new=t_new, hist=hist),
        grid=(b, nh // SAMPLE_HB),
        in_specs=[small, small, small, newt, newt, big, big],
        out_specs=[small, big, big],
        out_shape=[jax.ShapeDtypeStruct((b, SAMPLE_ROWS, nh * HEAD_DIM), F32),
                   jax.ShapeDtypeStruct(cache_kt.shape, F32),
                   jax.ShapeDtypeStruct(cache_vt.shape, F32)],
        compiler_params=_params(("arbitrary", "arbitrary")),
        name="dil_sample",
    )(q, kn, vn, knt, vnt, cache_kt, cache_vt)


def _out_proj_kernel(oa_ref, ob_ref, wa_ref, wb_ref, x_ref, g_ref, h_ref, xn_ref):
    y = (jnp.dot(oa_ref[...], wa_ref[...], preferred_element_type=F32)
         + jnp.dot(ob_ref[...], wb_ref[...], preferred_element_type=F32))
    h = x_ref[...] + y
    h_ref[...] = h
    ms = jnp.mean(h * h, axis=-1, keepdims=True)
    xn_ref[...] = (h * lax.rsqrt(ms + EPS) * g_ref[...]).astype(BF16)


def _out_proj(oa, ob, wa, wb, x, g, tm):
    m, d = x.shape
    row = lambda w: pl.BlockSpec((tm, w), lambda i: (i, 0))
    full = lambda a: pl.BlockSpec(a.shape, lambda i: (0, 0))
    g2 = g.reshape(1, d)
    return pl.pallas_call(
        _out_proj_kernel,
        grid=(m // tm,),
        in_specs=[row(GLA_VW), row(DIL_W), full(wa), full(wb), row(d), full(g2)],
        out_specs=[row(d), row(d)],
        out_shape=[jax.ShapeDtypeStruct((m, d), F32), jax.ShapeDtypeStruct((m, d), BF16)],
        compiler_params=_params(("arbitrary",)),
        name="out_proj",
    )(oa, ob, wa, wb, x, g2)


FF_ROWS = 256


def _ffn_up_kernel(*refs, tiles_per_seq, t_seq, has_hist):
    if has_hist:
        (x_ref, wu_ref, wg_ref, cwu_ref, cwg_ref, cbu_ref, cbg_ref,
         h1u_ref, h1g_ref, h2u_ref, h2g_ref, act_ref, upu_ref, upg_ref) = refs
    else:
        (x_ref, wu_ref, wg_ref, cwu_ref, cwg_ref, cbu_ref, cbg_ref,
         act_ref, tu_ref, tg_ref, cu_ref, cg_ref) = refs
    i = pl.program_id(1)
    tm = x_ref.shape[0]
    w_refs = (wu_ref, wg_ref)
    cws = (cwu_ref[...], cwg_ref[...])
    cbs = (cbu_ref[...], cbg_ref[...])

    def taps(idx, up, prev1, prev2):
        cw = cws[idx]
        return cbs[idx] + cw[0:1] * prev2 + cw[1:2] * prev1 + cw[2:3] * up

    def swiglu(u, g):
        return (g / (1.0 + jnp.exp(-g)) * u).astype(BF16)

    if has_hist:
        x = x_ref[...]
        t_in = lax.broadcasted_iota(jnp.int32, (tm, 1), 0) % t_seq
        conv = []
        for idx in range(2):
            up = jnp.dot(x, w_refs[idx][...], preferred_element_type=F32)
            prev1 = jnp.where(t_in >= 1, pltpu.roll(up, 1, axis=0), 0.0) + (h1u_ref, h1g_ref)[idx][...]
            prev2 = jnp.where(t_in >= 2, pltpu.roll(up, 2, axis=0), 0.0) + (h2u_ref, h2g_ref)[idx][...]
            (upu_ref, upg_ref)[idx][...] = up
            conv.append(taps(idx, up, prev1, prev2))
        act_ref[...] = swiglu(*conv)
        return

    carry_refs = (cu_ref, cg_ref)

    @pl.when(i % tiles_per_seq == 0)
    def _():
        for ref in carry_refs:
            ref[...] = jnp.zeros_like(ref)

    carry = [ref[...] for ref in carry_refs]
    for r0 in range(0, tm, FF_ROWS):
        x = x_ref[r0:r0 + FF_ROWS, :]
        conv = []
        for idx in range(2):
            up = jnp.dot(x, w_refs[idx][...], preferred_element_type=F32)
            ext = jnp.concatenate([carry[idx], up], axis=0)
            prev1 = pltpu.roll(ext, 1, axis=0)[SUBLANE:]
            prev2 = pltpu.roll(ext, 2, axis=0)[SUBLANE:]
            carry[idx] = up[FF_ROWS - SUBLANE:]
            conv.append(taps(idx, up, prev1, prev2))
        act_ref[r0:r0 + FF_ROWS, :] = swiglu(*conv)
    for idx, tail_ref in enumerate((tu_ref, tg_ref)):
        carry_refs[idx][...] = carry[idx]
        tail_ref[...] = carry[idx]


def _ffn_up(xn, w_up, conv_w, conv_b, tm, *, t_seq, hist=None):
    m, d = xn.shape
    has_hist = hist is not None
    tiles_per_seq = max(t_seq // tm, 1)
    n_seq = m // t_seq
    xs = pl.BlockSpec((tm, d), lambda j, i: (i, 0))
    wu = pl.BlockSpec((d, FF_TN), lambda j, i: (0, j))
    wg = pl.BlockSpec((d, FF_TN), lambda j, i: (0, FF_NT + j))
    cu = lambda r: pl.BlockSpec((r, FF_TN), lambda j, i: (0, j))
    cg = lambda r: pl.BlockSpec((r, FF_TN), lambda j, i: (0, FF_NT + j))
    act = pl.BlockSpec((tm, FF_TN), lambda j, i: (i, j))
    in_specs = [xs, wu, wg, cu(CONV_W), cg(CONV_W), cu(1), cg(1)]
    args = [xn, w_up, w_up, conv_w, conv_w, conv_b, conv_b]
    out_specs = [act]
    out_shape = [jax.ShapeDtypeStruct((m, D_FF_P), BF16)]
    scratch = []
    if has_hist:
        tu = pl.BlockSpec((tm, FF_TN), lambda j, i: (i, j))
        tg = pl.BlockSpec((tm, FF_TN), lambda j, i: (i, FF_NT + j))
        in_specs += [tu, tg, tu, tg]
        args += [hist[0], hist[0], hist[1], hist[1]]
        out_specs += [tu, tu]
        out_shape += [jax.ShapeDtypeStruct((m, D_FF_P), F32)] * 2
    else:
        tail = pl.BlockSpec((None, SUBLANE, FF_TN), lambda j, i: (i // tiles_per_seq, 0, j))
        out_specs += [tail, tail]
        out_shape += [jax.ShapeDtypeStruct((n_seq, SUBLANE, D_FF_P), F32)] * 2
        scratch = [pltpu.VMEM((SUBLANE, FF_TN), F32)] * 2
    return pl.pallas_call(
        functools.partial(_ffn_up_kernel, tiles_per_seq=tiles_per_seq, t_seq=t_seq, has_hist=has_hist),
        grid=(FF_NT, m // tm),
        in_specs=in_specs,
        out_specs=out_specs,
        out_shape=out_shape,
        scratch_shapes=scratch,
        compiler_params=_params(("arbitrary", "arbitrary")),
        name="ffn_up",
    )(*args)


def _ffn_down_kernel(a_ref, w_ref, h_ref, g_ref, y_ref, acc_ref):
    kk = pl.program_id(1)

    @pl.when(kk == 0)
    def _():
        acc_ref[...] = jnp.zeros_like(acc_ref)

    acc_ref[...] += jnp.dot(a_ref[...], w_ref[...], preferred_element_type=F32)

    @pl.when(kk == pl.num_programs(1) - 1)
    def _():
        h = h_ref[...] + acc_ref[...]
        ms = jnp.mean(h * h, axis=-1, keepdims=True)
        y_ref[...] = h * lax.rsqrt(ms + EPS) * g_ref[...]


def _ffn_down(act, w, h, g, tm, tk):
    m, d = h.shape
    kdim = act.shape[1]
    return pl.pallas_call(
        _ffn_down_kernel,
        grid=(m // tm, kdim // tk),
        in_specs=[pl.BlockSpec((tm, tk), lambda i, kk: (i, kk)),
                  pl.BlockSpec((tk, d), lambda i, kk: (kk, 0)),
                  pl.BlockSpec((tm, d), lambda i, kk: (i, 0)),
                  pl.BlockSpec((1, d), lambda i, kk: (0, 0))],
        out_specs=pl.BlockSpec((tm, d), lambda i, kk: (i, 0)),
        out_shape=jax.ShapeDtypeStruct((m, d), F32),
        scratch_shapes=[pltpu.VMEM((tm, d), F32)],
        compiler_params=_params(("arbitrary", "arbitrary")),
        name="ffn_down",
    )(act, w, h, g.reshape(1, d))


def _pad_cols(a, width):
    return jnp.pad(a, ((0, 0), (0, width - a.shape[1])))


def _split_pad_ff(a):
    return jnp.concatenate([_pad_cols(a[:, :D_FF], D_FF_P), _pad_cols(a[:, D_FF:], D_FF_P)], axis=1)


def _unpad_ff(u, g):
    return jnp.concatenate([u[..., :D_FF], g[..., :D_FF]], axis=-1)


def kernel(x_prompt, x_sample, state_gla, cache_dil_k, cache_dil_v, state_ffn_conv, norm_mix, w_in, w_gate_up,
           b_gate, gla_norm, w_out, norm_ffn, w_ffn_up, ffn_conv_w, ffn_conv_b, w_ffn_down, norm_final):
    bp, tp, d = x_prompt.shape
    bs, ts, _ = x_sample.shape
    l = 0

    z0 = 2 * GLA_QK + 2 * GLA_VW
    w_in_p = jnp.concatenate([w_in[l][:, :z0], w_in[l][:, z0 + GLA_RANK:], w_in[l][:, z0:z0 + GLA_RANK],
                              jnp.zeros((d, Z_PAD - GLA_RANK), F32)], axis=1).astype(BF16)
    wg_p = jnp.pad(w_gate_up[l], ((0, Z_PAD - GLA_RANK), (0, 0)))
    bg = b_gate[l].reshape(1, GLA_QK)
    gn = gla_norm[l].reshape(1, GLA_DV)
    wa = w_out[l][:GLA_VW].astype(BF16)
    wb = w_out[l][GLA_VW:].astype(BF16)
    w_up_p = _split_pad_ff(w_ffn_up[l]).astype(BF16)
    cw_p = _split_pad_ff(ffn_conv_w[l])
    cb_p = _split_pad_ff(ffn_conv_b[l].reshape(1, -1))
    w_dn_p = jnp.pad(w_ffn_down[l], ((0, D_FF_P - D_FF), (0, 0))).astype(BF16)

    mp = bp * tp
    xp = x_prompt.reshape(mp, d)
    proj_p = _rms_matmul(xp, norm_mix[l], w_in_p, 1024, 896)
    cos_p, sin_p = _rope_tables(jnp.arange(tp))
    cos_p = jnp.tile(cos_p, (bp, 1))
    sin_p = jnp.tile(sin_p, (bp, 1))
    rope_out = _rope(proj_p, cos_p, sin_p, 512, n_seq=bp)
    q_nat, k_rot, k_nat, v_nat = rope_out[:4]
    by_class = {1: tuple(a.reshape(bp, 1, tp, DIL_W) for a in (q_nat, k_nat, v_nat))}
    for di, r in enumerate(CLASS_DILATIONS):
        by_class[r] = tuple(rope_out[4 + 3 * di:7 + 3 * di])

    oa_p, gla_p = _gla(proj_p.reshape(bp, tp, N_IN_P), COL_Z // LANE, wg_p, bg, gn,
                       jnp.zeros((bp, GLA_HEADS, GLA_DK, GLA_DV), F32),
                       chunk=GLA_CHUNK, n_chunks=4, t_valid=tp)

    state = _dil_branch(*by_class[16], None, first=True, final=False)
    state = _dil_branch(*by_class[4], state, first=False, final=False)
    (ob_p,) = _dil_branch(*by_class[1], state, first=False, final=True)

    h_p, xn_p = _out_proj(oa_p.reshape(mp, GLA_VW), ob_p.reshape(mp, DIL_W), wa, wb, xp, norm_ffn[l], 512)
    act_p, tail_u, tail_g = _ffn_up(xn_p, w_up_p, cw_p, cb_p, 1024, t_seq=tp)
    y_p = _ffn_down(act_p, w_dn_p, h_p, norm_final, 512, 1408)

    buf_p = min(DIL_WINDOW, tp)
    y_prompt = y_p.reshape(bp, tp, d)
    new_gla_p = gla_p[None]
    new_k_p = k_rot.reshape(bp, tp, DIL_HEADS, HEAD_DIM)[None, :, tp - buf_p:]
    v_f32 = proj_p[:, z0 + 2 * DIL_W:z0 + 3 * DIL_W]
    new_v_p = v_f32.reshape(bp, tp, DIL_HEADS, HEAD_DIM)[None, :, tp - buf_p:]
    new_conv_p = _unpad_ff(tail_u[:, SUBLANE - (CONV_W - 1):], tail_g[:, SUBLANE - (CONV_W - 1):])[None]

    ms_ = bs * ts
    xs = x_sample.reshape(ms_, d)
    proj_s = _rms_matmul(xs, norm_mix[l], w_in_p, ms_, 896)
    cos_s, sin_s = _rope_tables(PAST_LEN + jnp.arange(ts))
    cos_s = jnp.tile(cos_s, (bs, 1))
    sin_s = jnp.tile(sin_s, (bs, 1))
    q_rs, k_rs = _rope(proj_s, cos_s, sin_s, ms_)

    gla_src = jnp.concatenate([proj_s[:, :z0], proj_s[:, COL_Z:]], axis=1).reshape(bs, ts, z0 + Z_PAD)
    gla_src = jnp.pad(gla_src, ((0, 0), (0, GLA_CHUNK - ts), (0, 0)))
    oa_s, gla_s = _gla(gla_src, z0 // LANE, wg_p, bg, gn, state_gla[l],
                       chunk=GLA_CHUNK, n_chunks=1, t_valid=ts)
    oa_s = oa_s[:, :ts].reshape(ms_, GLA_VW)

    v_s = proj_s[:, z0 + 2 * DIL_W:z0 + 3 * DIL_W]
    pad_rows = lambda a: jnp.pad(a.reshape(bs, ts, DIL_W), ((0, 0), (0, SAMPLE_ROWS - ts), (0, 0)))
    new_cols = lambda a: jnp.pad(a.reshape(bs, ts, DIL_HEADS, HEAD_DIM).transpose(0, 2, 3, 1),
                                 ((0, 0), (0, 0), (0, 0), (LANE - ts, 0)))
    time_minor = lambda a: a.transpose(0, 2, 3, 1)
    ob_s, k_s_t, v_s_t = _dil_sample(pad_rows(q_rs), pad_rows(k_rs), pad_rows(v_s), new_cols(k_rs), new_cols(v_s),
                                     time_minor(cache_dil_k[l]), time_minor(cache_dil_v[l]), ts)
    ob_s = ob_s[:, :ts].astype(BF16)

    h_s, xn_s = _out_proj(oa_s, ob_s.reshape(ms_, DIL_W), wa, wb, xs, norm_ffn[l], ms_)
    conv_hist = _split_pad_ff(state_ffn_conv[l].reshape(bs * (CONV_W - 1), 2 * D_FF))
    conv_hist = conv_hist.reshape(bs, CONV_W - 1, 2 * D_FF_P)
    zero_row = jnp.zeros((bs, 1, 2 * D_FF_P), F32)
    h1 = jnp.concatenate([conv_hist[:, 1:2]] + [zero_row] * (ts - 1), axis=1).reshape(ms_, 2 * D_FF_P)
    h2 = jnp.concatenate([conv_hist[:, 0:1], conv_hist[:, 1:2]] + [zero_row] * (ts - 2),
                         axis=1).reshape(ms_, 2 * D_FF_P)
    act_s, up_u, up_g = _ffn_up(xn_s, w_up_p, cw_p, cb_p, ms_, t_seq=ts, hist=(h1, h2))
    y_s = _ffn_down(act_s, w_dn_p, h_s, norm_final, ms_, 1408)

    y_sample = y_s.reshape(bs, ts, d)
    new_gla_s = gla_s[None]
    new_k_s = k_s_t.transpose(0, 3, 1, 2)[None]
    new_v_s = v_s_t.transpose(0, 3, 1, 2)[None]
    up_full = _unpad_ff(up_u, up_g).reshape(bs, ts, 2 * D_FF)
    new_conv_s = up_full[:, ts - (CONV_W - 1):][None]

    return (y_prompt, y_sample, new_gla_p, new_gla_s, new_k_p, new_k_s, new_v_p, new_v_s,
            new_conv_p, new_conv_s)
```

```python
import functools

import jax
import jax.numpy as jnp
from jax import lax
from jax.experimental import pallas as pl
from jax.experimental.pallas import tpu as pltpu

F32 = jnp.float32
BF16 = jnp.bfloat16

D_MODEL = 2048
HEAD_DIM = 64
GLA_HEADS = 10
GLA_DK = 64
GLA_DV = 128
GLA_RANK = 16
GLA_TAU = 16.0
GLA_CHUNK = 64
DIL_HEADS = 12
DIL_PAIRS = ((128, 1), (512, 4), (2048, 16))
DIL_WINDOW = 2048
ROPE_THETA = 10000.0
D_FF = 5504
PAST_LEN = 8192
CONV_W = 3
EPS = 1e-6

GLA_QK = GLA_HEADS * GLA_DK
GLA_VW = GLA_HEADS * GLA_DV
DIL_W = DIL_HEADS * HEAD_DIM

LANE = 128
SUBLANE = 8
VMEM_LIMIT = 56 * 1024 * 1024

Z_PAD = LANE
D_FF_P = 5632
FF_TN = 512
FF_NT = D_FF_P // FF_TN

NT_DIMS = (((1,), (1,)), ((), ()))


def _params(sem, vmem=VMEM_LIMIT, flags=None):
    return pltpu.CompilerParams(dimension_semantics=sem, vmem_limit_bytes=vmem, flags=flags)


def _rms_kernel(x_ref, g_ref, o_ref):
    x = x_ref[...]
    ms = jnp.mean(x * x, axis=-1, keepdims=True)
    o_ref[...] = (x * lax.rsqrt(ms + EPS) * g_ref[...]).astype(BF16)


def _rms(x, g, tm):
    m, d = x.shape
    return pl.pallas_call(
        _rms_kernel,
        grid=(m // tm,),
        in_specs=[pl.BlockSpec((tm, d), lambda i: (i, 0)), pl.BlockSpec((1, d), lambda i: (0, 0))],
        out_specs=pl.BlockSpec((tm, d), lambda i: (i, 0)),
        out_shape=jax.ShapeDtypeStruct((m, d), BF16),
        compiler_params=_params(("arbitrary",)),
        name="rms",
    )(x, g.reshape(1, d))


def _proj_kernel(x_ref, w_ref, o_ref, wbf_ref):
    @pl.when(pl.program_id(1) == 0)
    def _():
        wbf_ref[...] = w_ref[...].astype(BF16)

    o_ref[...] = lax.dot_general(x_ref[...], wbf_ref[...], NT_DIMS, preferred_element_type=F32)


def _proj(xn, wt, tm, tn, n=None):
    m, k = xn.shape
    n = wt.shape[0] if n is None else n
    return pl.pallas_call(
        _proj_kernel,
        grid=(n // tn, m // tm),
        in_specs=[pl.BlockSpec((tm, k), lambda j, i: (i, 0)), pl.BlockSpec((tn, k), lambda j, i: (j, 0))],
        out_specs=pl.BlockSpec((tm, tn), lambda j, i: (i, j)),
        out_shape=jax.ShapeDtypeStruct((m, n), F32),
        scratch_shapes=[pltpu.VMEM((tn, k), BF16)],
        compiler_params=_params(("arbitrary", "arbitrary")),
        name="proj",
    )(xn, wt)


CLASS_DILATIONS = tuple(r for _, r in DIL_PAIRS if r > 1)


def _rope_kernel(q_ref, k_ref, v_ref, cos_ref, sin_ref, qo_ref, ko_ref, *rest, by_class):
    reps = DIL_W // LANE
    cos = jnp.concatenate([cos_ref[...]] * reps, axis=1)
    sin = jnp.concatenate([sin_ref[...]] * reps, axis=1)
    lane = lax.broadcasted_iota(jnp.int32, (1, DIL_W), 1)
    first_half = (lane % HEAD_DIM) < (HEAD_DIM // 2)

    def rot(x):
        partner = jnp.where(first_half,
                            pltpu.roll(x, DIL_W - HEAD_DIM // 2, axis=1),
                            pltpu.roll(x, HEAD_DIM // 2, axis=1))
        return x * cos + partner * sin

    q = rot(q_ref[...]) * (HEAD_DIM ** -0.5)
    k = rot(k_ref[...])
    ko_ref[...] = k
    if not by_class:
        qo_ref[...] = q
        return
    qo_ref[...] = q.astype(BF16)
    kb_ref, vb_ref = rest[0], rest[1]
    class_refs = rest[2:2 + 3 * len(CLASS_DILATIONS)]
    sq_ref, sk_ref, sv_ref = rest[2 + 3 * len(CLASS_DILATIONS):]
    v = v_ref[...]
    kb_ref[...] = k.astype(BF16)
    vb_ref[...] = v.astype(BF16)
    for src, val in ((sq_ref, q), (sk_ref, k), (sv_ref, v)):
        for j in range(reps):
            src[j] = val[:, j * LANE:(j + 1) * LANE]
    tm = q.shape[0]
    for di, r in enumerate(CLASS_DILATIONS):
        for src, dst in zip((sq_ref, sk_ref, sv_ref), class_refs[3 * di:3 * di + 3]):
            for c in range(r):
                rows = pl.ds(c, tm // r, stride=r)
                dst[c] = jnp.concatenate([src[j, rows, :] for j in range(reps)], axis=1).astype(BF16)


def _rope(proj, cos_t, sin_t, tm, n_seq=None):
    m = proj.shape[0]
    by_class = n_seq is not None
    blk = lambda c: pl.BlockSpec((tm, DIL_W), lambda i, c=c: (i, c))
    tab = pl.BlockSpec((tm, LANE), lambda i: (i, 0))
    out = pl.BlockSpec((tm, DIL_W), lambda i: (i, 0))
    out_specs = [out, out]
    out_shape = [jax.ShapeDtypeStruct((m, DIL_W), BF16 if by_class else F32),
                 jax.ShapeDtypeStruct((m, DIL_W), F32)]
    scratch = []
    if by_class:
        t_seq = m // n_seq
        tiles = t_seq // tm
        out_specs += [out, out]
        out_shape += [jax.ShapeDtypeStruct((m, DIL_W), BF16)] * 2
        for r in CLASS_DILATIONS:
            spec = pl.BlockSpec((None, r, tm // r, DIL_W), lambda i: (i // tiles, 0, i % tiles, 0))
            out_specs += [spec] * 3
            out_shape += [jax.ShapeDtypeStruct((n_seq, r, t_seq // r, DIL_W), BF16)] * 3
        scratch = [pltpu.VMEM((DIL_W // LANE, tm, LANE), F32)] * 3
    return pl.pallas_call(
        functools.partial(_rope_kernel, by_class=by_class),
        grid=(m // tm,),
        in_specs=[blk(0), blk(1), blk(2), tab, tab],
        out_specs=out_specs,
        out_shape=out_shape,
        scratch_shapes=scratch,
        compiler_params=_params(("arbitrary",)),
        name="rope",
    )(proj, proj, proj, cos_t, sin_t)


def _rope_tables(pos):
    half = HEAD_DIM // 2
    inv_freq = ROPE_THETA ** (-2.0 * jnp.arange(half, dtype=F32) / HEAD_DIM)
    ang = pos.astype(F32)[:, None] * inv_freq[None, :]
    cos = jnp.cos(ang)
    sin = jnp.sin(ang)
    cos_t = jnp.concatenate([cos, cos, cos, cos], axis=1)
    sin_t = jnp.concatenate([-sin, sin, -sin, sin], axis=1)
    return cos_t, sin_t


def _gla_kernel(q_ref, k_ref, v_ref, r_ref, z_ref, wg_ref, bg_ref, gn_ref, s0_ref,
                o_ref, sfin_ref, st_ref, *, chunk, n_chunks, t_valid):
    i = pl.program_id(1)

    @pl.when(i == 0)
    def _():
        for h in range(GLA_HEADS):
            st_ref[h] = s0_ref[h].T

    row = lax.broadcasted_iota(jnp.int32, (chunk, chunk), 0)
    col = lax.broadcasted_iota(jnp.int32, (chunk, chunk), 1)
    causal = row >= col
    tril = causal.astype(F32)
    t_in = lax.broadcasted_iota(jnp.int32, (chunk, 1), 0)

    for c in range(n_chunks):
        sl = pl.ds(c * chunk, chunk)
        x = jnp.dot(z_ref[sl, :], wg_ref[...], precision=lax.Precision.HIGHEST,
                    preferred_element_type=F32) + bg_ref[...]
        log_a = -(jnp.maximum(-x, 0.0) + jnp.log1p(jnp.exp(-jnp.abs(x)))) / GLA_TAU
        t_abs = (i * n_chunks + c) * chunk + t_in
        log_a = jnp.where(t_abs < t_valid, log_a, 0.0)
        cum = jnp.dot(tril, log_a, precision=lax.Precision.HIGHEST, preferred_element_type=F32)
        last = cum[chunk - 1:chunk, :]
        q = q_ref[sl, :] * (GLA_DK ** -0.5)
        k = k_ref[sl, :]
        q_dec = q * jnp.exp(cum)
        k_inv = k * jnp.exp(-cum)
        k_end = k * jnp.exp(last - cum)
        a_end = jnp.exp(last)
        for h in range(GLA_HEADS):
            ks = slice(h * GLA_DK, (h + 1) * GLA_DK)
            vs = slice(h * GLA_DV, (h + 1) * GLA_DV)
            qh = q_dec[:, ks].astype(BF16)
            v_f = v_ref[sl, vs]
            vh = v_f.astype(BF16)
            scores = lax.dot_general(qh, k_inv[:, ks].astype(BF16), NT_DIMS, preferred_element_type=F32)
            scores = jnp.where(causal, scores, 0.0)
            st = st_ref[h]
            o = (jnp.dot(scores.astype(BF16), vh, preferred_element_type=F32)
                 + lax.dot_general(qh, st.astype(BF16), NT_DIMS, preferred_element_type=F32))
            st_ref[h] = st * a_end[:, ks] + jnp.dot(v_f.T.astype(BF16), k_end[:, ks].astype(BF16),
                                                    preferred_element_type=F32)
            ms = jnp.mean(o * o, axis=-1, keepdims=True)
            on = o * lax.rsqrt(ms + EPS) * gn_ref[...]
            r = r_ref[sl, vs]
            gate = r / (1.0 + jnp.exp(-r))
            o_ref[sl, vs] = (on * gate).astype(BF16)

    @pl.when(i == pl.num_programs(1) - 1)
    def _():
        for h in range(GLA_HEADS):
            sfin_ref[h] = st_ref[h].T


def _gla(src, zsrc, wg, bg, gn, s0, *, chunk, n_chunks, t_valid):
    b, t, _ = src.shape
    tb = chunk * n_chunks
    col = lambda w, c: pl.BlockSpec((None, tb, w), lambda bi, i, c=c: (bi, i, c))
    full = lambda shape: pl.BlockSpec(shape, lambda bi, i: (0,) * len(shape))
    state = pl.BlockSpec((None, GLA_HEADS, GLA_DK, GLA_DV), lambda bi, i: (bi, 0, 0, 0))
    kern = functools.partial(_gla_kernel, chunk=chunk, n_chunks=n_chunks, t_valid=t_valid)
    return pl.pallas_call(
        kern,
        grid=(b, t // tb),
        in_specs=[col(GLA_QK, 0), col(GLA_QK, 1), col(GLA_VW, 1), col(GLA_VW, 2), col(Z_PAD, 0),
                  full((Z_PAD, GLA_QK)), full((1, GLA_QK)), full((1, GLA_DV)), state],
        out_specs=[pl.BlockSpec((None, tb, GLA_VW), lambda bi, i: (bi, i, 0)), state],
        out_shape=[jax.ShapeDtypeStruct((b, t, GLA_VW), BF16),
                   jax.ShapeDtypeStruct((b, GLA_HEADS, GLA_DK, GLA_DV), F32)],
        scratch_shapes=[pltpu.VMEM((GLA_HEADS, GLA_DV, GLA_DK), F32)],
        compiler_params=_params(("arbitrary", "arbitrary")),
        name="gla",
    )(src, src, src, src, zsrc, wg, bg, gn, s0)


QB = 128


def _dil_branch_kernel(*refs, first, final, dilation):
    if first:
        q_ref, kp_ref, kc_ref, vp_ref, vc_ref = refs[:5]
        outs = refs[5:]
    else:
        q_ref, kp_ref, kc_ref, vp_ref, vc_ref, num_ref, m_ref, d_ref = refs[:8]
        outs = refs[8:]
    i = pl.program_id(1)
    rows = pl.ds(pl.program_id(2), QB, stride=dilation) if dilation > 1 else slice(None)
    k2 = jnp.concatenate([kp_ref[...], kc_ref[...]], axis=0)
    v2 = jnp.concatenate([vp_ref[...], vc_ref[...]], axis=0)
    q = q_ref[...]
    qi = QB + lax.broadcasted_iota(jnp.int32, (QB, 1), 0)
    kj = lax.broadcasted_iota(jnp.int32, (1, 2 * QB), 1)
    off = qi - kj
    valid = (off >= 0) & (off <= QB) & ((kj >= QB) | (i > 0))
    lane = lax.broadcasted_iota(jnp.int32, (1, LANE), 1)
    lo = lane < HEAD_DIM
    m_tile = jnp.zeros((QB, LANE), F32)
    d_tile = jnp.zeros((QB, LANE), F32)
    if not first:
        m_in = m_ref[rows, :]
        d_in = d_ref[rows, :]
    pairs = []
    for p in range(DIL_HEADS // 2):
        cs = slice(p * LANE, (p + 1) * LANE)
        qp, kp, vp = q[:, cs], k2[:, cs], v2[:, cs]
        pv, al, dn = [], [], []
        for hh in range(2):
            h = 2 * p + hh
            sel = lo if hh == 0 else jnp.logical_not(lo)
            qm = jnp.where(sel, qp, jnp.zeros_like(qp))
            s = lax.dot_general(qm, kp, NT_DIMS, preferred_element_type=F32)
            s = jnp.where(valid, s, -jnp.inf)
            m_new = jnp.max(s, axis=-1, keepdims=True)
            if not first:
                m_old = jnp.sum(jnp.where(lane == h, m_in, 0.0), axis=-1, keepdims=True)
                d_old = jnp.sum(jnp.where(lane == h, d_in, 0.0), axis=-1, keepdims=True)
                m_new = jnp.maximum(m_new, m_old)
            pr = jnp.exp(s - m_new)
            d_new = jnp.sum(pr, axis=-1, keepdims=True)
            if not first:
                alpha = jnp.exp(m_old - m_new)
                d_new = d_new + d_old * alpha
                al.append(alpha)
            pv.append(jnp.dot(pr.astype(BF16), vp, preferred_element_type=F32))
            dn.append(d_new)
            m_tile = m_tile + jnp.where(lane == h, m_new, 0.0)
            d_tile = d_tile + jnp.where(lane == h, d_new, 0.0)
        num = jnp.where(lo, pv[0], pv[1])
        if not first:
            num = num + num_ref[p, rows, :] * jnp.where(lo, al[0], al[1])
        if final:
            num = num / jnp.where(lo, dn[0], dn[1])
        else:
            outs[0][p, rows, :] = num
        pairs.append(num)
    if final:
        outs[0][...] = jnp.concatenate(pairs, axis=1).astype(BF16)
    else:
        outs[1][rows, :] = m_tile
        outs[2][rows, :] = d_tile


def _dil_branch(q, k, v, state, *, first, final):
    b, r, ts, _ = q.shape
    t = r * ts
    nq = ts // QB
    cur = pl.BlockSpec((None, None, QB, DIL_W), lambda bi, i, c: (bi, c, i, 0))
    prev = pl.BlockSpec((None, None, QB, DIL_W), lambda bi, i, c: (bi, c, jnp.maximum(i - 1, 0), 0))
    nat = lambda w: pl.BlockSpec((None, r * QB, w), lambda bi, i, c: (bi, i, 0))
    nat_num = pl.BlockSpec((None, DIL_W // LANE, r * QB, LANE), lambda bi, i, c: (bi, 0, i, 0))
    args = [q, k, k, v, v]
    in_specs = [cur, prev, cur, prev, cur]
    if not first:
        args += list(state)
        in_specs += [nat_num, nat(LANE), nat(LANE)]
    if final:
        assert r == 1
        out_specs = [nat(DIL_W)]
        out_shape = [jax.ShapeDtypeStruct((b, t, DIL_W), BF16)]
    else:
        out_specs = [nat_num, nat(LANE), nat(LANE)]
        out_shape = [jax.ShapeDtypeStruct((b, DIL_W // LANE, t, LANE), F32),
                     jax.ShapeDtypeStruct((b, t, LANE), F32),
                     jax.ShapeDtypeStruct((b, t, LANE), F32)]
    return pl.pallas_call(
        functools.partial(_dil_branch_kernel, first=first, final=final, dilation=r),
        grid=(b, nq, r),
        in_specs=in_specs,
        out_specs=out_specs,
        out_shape=out_shape,
        compiler_params=_params(("arbitrary", "arbitrary", "arbitrary")),
        name=f"dil_branch_r{r}",
    )(*args)


SAMPLE_HB = 6
SAMPLE_ROWS = 16


def _dil_sample_kernel(q_ref, kn_ref, vn_ref, knt_ref, vnt_ref, kc_ref, vc_ref, o_ref, ko_ref, vo_ref,
                       *, t_new, hist):
    t_row = lax.broadcasted_iota(jnp.int32, (SAMPLE_ROWS, 1), 0)
    j = lax.broadcasted_iota(jnp.int32, (1, hist), 1)
    lane = lax.broadcasted_iota(jnp.int32, (1, LANE), 1)
    keep = lane < LANE - t_new

    def weight(off):
        w = jnp.zeros(off.shape, F32)
        for win, dil in DIL_PAIRS:
            w = w + jnp.where((off >= 0) & (off <= win) & (off % dil == 0), 1.0, 0.0)
        return w

    w_c = weight(hist + t_row - j)
    w_n = [weight(t_row - t) for t in range(t_new)]
    outs = []
    for h in range(SAMPLE_HB):
        hs = slice(h * HEAD_DIM, (h + 1) * HEAD_DIM)
        qh = q_ref[:, hs]
        s_c = jnp.dot(qh.astype(BF16), kc_ref[h].astype(BF16), preferred_element_type=F32)
        s_c = jnp.where(w_c > 0, s_c, -jnp.inf)
        m = jnp.max(s_c, axis=-1, keepdims=True)
        s_n = []
        for t in range(t_new):
            s_t = jnp.sum(qh * kn_ref[t:t + 1, hs], axis=-1, keepdims=True)
            s_t = jnp.where(w_n[t] > 0, s_t, -jnp.inf)
            s_n.append(s_t)
            m = jnp.maximum(m, s_t)
        p_c = w_c * jnp.exp(s_c - m)
        den = jnp.sum(p_c, axis=-1, keepdims=True)
        num = lax.dot_general(p_c.astype(BF16), vc_ref[h].astype(BF16), NT_DIMS, preferred_element_type=F32)
        for t in range(t_new):
            p_t = w_n[t] * jnp.exp(s_n[t] - m)
            den = den + p_t
            num = num + p_t * vn_ref[t:t + 1, hs]
        outs.append(num / den)
        for src, new_ref, dst in ((kc_ref, knt_ref, ko_ref), (vc_ref, vnt_ref, vo_ref)):
            nxt = pltpu.roll(src[h, :, 0:LANE], LANE - t_new, axis=1)
            for c in range(hist // LANE):
                cur_t = nxt
                if c + 1 < hist // LANE:
                    nxt = pltpu.roll(src[h, :, (c + 1) * LANE:(c + 2) * LANE], LANE - t_new, axis=1)
                else:
                    nxt = new_ref[h]
                dst[h, :, c * LANE:(c + 1) * LANE] = jnp.where(keep, cur_t, nxt)
    o_ref[...] = jnp.concatenate(outs, axis=1)


def _dil_sample(q, kn, vn, knt, vnt, cache_kt, cache_vt, t_new):
    b, nh, _, hist = cache_kt.shape
    w = SAMPLE_HB * HEAD_DIM
    small = pl.BlockSpec((None, SAMPLE_ROWS, w), lambda bi, c: (bi, 0, c))
    newt = pl.BlockSpec((None, SAMPLE_HB, HEAD_DIM, LANE), lambda bi, c: (bi, c, 0, 0))
    big = pl.BlockSpec((None, SAMPLE_HB, HEAD_DIM, hist), lambda bi, c: (bi, c, 0, 0))
    return pl.pallas_call(
        functools.partial(_dil_sample_kernel, t_new=t_new, hist=hist),
        grid=(b, nh // SAMPLE_HB),
        in_specs=[small, small, small, newt, newt, big, big],
        out_specs=[small, big, big],
        out_shape=[jax.ShapeDtypeStruct((b, SAMPLE_ROWS, nh * HEAD_DIM), F32),
                   jax.ShapeDtypeStruct(cache_kt.shape, F32),
                   jax.ShapeDtypeStruct(cache_vt.shape, F32)],
        compiler_params=_params(("arbitrary", "arbitrary")),
        name="dil_sample",
    )(q, kn, vn, knt, vnt, cache_kt, cache_vt)


def _out_proj_kernel(oa_ref, ob_ref, wa_ref, wb_ref, x_ref, g_ref, h_ref, xn_ref):
    y = (jnp.dot(oa_ref[...], wa_ref[...], preferred_element_type=F32)
         + jnp.dot(ob_ref[...], wb_ref[...], preferred_element_type=F32))
    h = x_ref[...] + y
    h_ref[...] = h
    ms = jnp.mean(h * h, axis=-1, keepdims=True)
    xn_ref[...] = (h * lax.rsqrt(ms + EPS) * g_ref[...]).astype(BF16)


def _out_proj(oa, ob, wa, wb, x, g, tm):
    m, d = x.shape
    row = lambda w: pl.BlockSpec((tm, w), lambda i: (i, 0))
    full = lambda a: pl.BlockSpec(a.shape, lambda i: (0, 0))
    g2 = g.reshape(1, d)
    return pl.pallas_call(
        _out_proj_kernel,
        grid=(m // tm,),
        in_specs=[row(GLA_VW), row(DIL_W), full(wa), full(wb), row(d), full(g2)],
        out_specs=[row(d), row(d)],
        out_shape=[jax.ShapeDtypeStruct((m, d), F32), jax.ShapeDtypeStruct((m, d), BF16)],
        compiler_params=_params(("arbitrary",)),
        name="out_proj",
    )(oa, ob, wa, wb, x, g2)


FF_ROWS = 256
FF_SUB = FF_TN // LANE
FF_TILES = D_FF // LANE


def _conv_taps(cw, cb, up, prev1, prev2):
    return cb + cw[0:1] * prev2 + cw[1:2] * prev1 + cw[2:3] * up


def _swiglu(u, g):
    return (g / (1.0 + jnp.exp(-g)) * u).astype(BF16)


def _stage_up_weights(w_blocks, wbf_ref, j):
    for half in range(2):
        for q in range(FF_SUB):
            blk = w_blocks[half * FF_SUB + q][...]
            if (FF_NT - 1) * FF_SUB + q >= FF_TILES:
                blk = jnp.where(j * FF_SUB + q < FF_TILES, blk, 0.0)
            wbf_ref[half, :, q * LANE:(q + 1) * LANE] = blk.astype(BF16)


def _up_weight_specs(tile_of_step):
    specs = []
    for half in range(2):
        for q in range(FF_SUB):
            def index(*ids, half=half, q=q):
                blk = jnp.minimum(tile_of_step(*ids) * FF_SUB + q, FF_TILES - 1)
                return (0, half * FF_TILES + blk)
            specs.append(pl.BlockSpec((D_MODEL, LANE), index))
    return specs


def _ffn_up_seq_kernel(*refs, tiles_per_seq):
    x_ref = refs[0]
    w_blocks = refs[1:1 + 2 * FF_SUB]
    cwu_ref, cwg_ref, cbu_ref, cbg_ref, act_ref, tu_ref, tg_ref, wbf_ref, carry_ref = refs[1 + 2 * FF_SUB:]
    j = pl.program_id(0)
    i = pl.program_id(1)
    tm = x_ref.shape[0]

    @pl.when(i == 0)
    def _():
        _stage_up_weights(w_blocks, wbf_ref, j)

    @pl.when(i % tiles_per_seq == 0)
    def _():
        carry_ref[...] = jnp.zeros_like(carry_ref)

    carry = [carry_ref[0], carry_ref[1]]
    for r0 in range(0, tm, FF_ROWS):
        x = x_ref[r0:r0 + FF_ROWS, :]
        conv = []
        for idx, (cw_ref, cb_ref) in enumerate(((cwu_ref, cbu_ref), (cwg_ref, cbg_ref))):
            up = jnp.dot(x, wbf_ref[idx], preferred_element_type=F32)
            ext = jnp.concatenate([carry[idx], up], axis=0)
            prev1 = pltpu.roll(ext, 1, axis=0)[SUBLANE:]
            prev2 = pltpu.roll(ext, 2, axis=0)[SUBLANE:]
            carry[idx] = up[FF_ROWS - SUBLANE:]
            conv.append(_conv_taps(cw_ref[...], cb_ref[...], up, prev1, prev2))
        act_ref[r0:r0 + FF_ROWS, :] = _swiglu(*conv)
    for idx, tail_ref in enumerate((tu_ref, tg_ref)):
        carry_ref[idx] = carry[idx]
        tail_ref[...] = carry[idx]


def _ffn_up_seq(xn, w_up, conv_w, conv_b, tm, t_seq):
    m, d = xn.shape
    tiles_per_seq = t_seq // tm
    xs = pl.BlockSpec((tm, d), lambda j, i: (i, 0))
    cu = lambda r: pl.BlockSpec((r, FF_TN), lambda j, i: (0, j))
    cg = lambda r: pl.BlockSpec((r, FF_TN), lambda j, i: (0, FF_NT + j))
    act = pl.BlockSpec((tm, FF_TN), lambda j, i: (i, j))
    tail = pl.BlockSpec((None, SUBLANE, FF_TN), lambda j, i: (i // tiles_per_seq, 0, j))
    w_specs = _up_weight_specs(lambda j, i: j)
    return pl.pallas_call(
        functools.partial(_ffn_up_seq_kernel, tiles_per_seq=tiles_per_seq),
        grid=(FF_NT, m // tm),
        in_specs=[xs] + w_specs + [cu(CONV_W), cg(CONV_W), cu(1), cg(1)],
        out_specs=[act, tail, tail],
        out_shape=[jax.ShapeDtypeStruct((m, D_FF_P), BF16)]
        + [jax.ShapeDtypeStruct((m // t_seq, SUBLANE, D_FF_P), F32)] * 2,
        scratch_shapes=[pltpu.VMEM((2, d, FF_TN), BF16), pltpu.VMEM((2, SUBLANE, FF_TN), F32)],
        compiler_params=_params(("arbitrary", "arbitrary")),
        name="ffn_up",
    )(xn, *([w_up] * len(w_specs)), conv_w, conv_w, conv_b, conv_b)


def _ffn_up_hist_kernel(*refs, t_seq):
    x_ref = refs[0]
    w_blocks = refs[1:1 + 2 * FF_SUB]
    (cwu_ref, cwg_ref, cbu_ref, cbg_ref, h1u_ref, h1g_ref, h2u_ref, h2g_ref,
     act_ref, upu_ref, upg_ref, wbf_ref) = refs[1 + 2 * FF_SUB:]
    _stage_up_weights(w_blocks, wbf_ref, pl.program_id(0))
    x = x_ref[...]
    t_in = lax.broadcasted_iota(jnp.int32, (x.shape[0], 1), 0) % t_seq
    conv = []
    for idx, (cw_ref, cb_ref, h1_ref, h2_ref, up_ref) in enumerate((
            (cwu_ref, cbu_ref, h1u_ref, h2u_ref, upu_ref), (cwg_ref, cbg_ref, h1g_ref, h2g_ref, upg_ref))):
        up = jnp.dot(x, wbf_ref[idx], preferred_element_type=F32)
        prev1 = jnp.where(t_in >= 1, pltpu.roll(up, 1, axis=0), 0.0) + h1_ref[...]
        prev2 = jnp.where(t_in >= 2, pltpu.roll(up, 2, axis=0), 0.0) + h2_ref[...]
        up_ref[...] = up
        conv.append(_conv_taps(cw_ref[...], cb_ref[...], up, prev1, prev2))
    act_ref[...] = _swiglu(*conv)


def _ffn_up_hist(xn, w_up, conv_w, conv_b, h1, h2, t_seq):
    m, d = xn.shape
    xs = pl.BlockSpec((m, d), lambda j: (0, 0))
    u = lambda r: pl.BlockSpec((r, FF_TN), lambda j: (0, j))
    g = lambda r: pl.BlockSpec((r, FF_TN), lambda j: (0, FF_NT + j))
    w_specs = _up_weight_specs(lambda j: j)
    return pl.pallas_call(
        functools.partial(_ffn_up_hist_kernel, t_seq=t_seq),
        grid=(FF_NT,),
        in_specs=[xs] + w_specs + [u(CONV_W), g(CONV_W), u(1), g(1), u(m), g(m), u(m), g(m)],
        out_specs=[u(m), u(m), u(m)],
        out_shape=[jax.ShapeDtypeStruct((m, D_FF_P), BF16)] + [jax.ShapeDtypeStruct((m, D_FF_P), F32)] * 2,
        scratch_shapes=[pltpu.VMEM((2, d, FF_TN), BF16)],
        compiler_params=_params(("arbitrary",)),
        name="ffn_up_hist",
    )(xn, *([w_up] * len(w_specs)), conv_w, conv_w, conv_b, conv_b, h1, h1, h2, h2)


def _ffn_down_kernel(a_ref, w_ref, h_ref, g_ref, y_ref, acc_ref):
    kk = pl.program_id(1)

    @pl.when(kk == 0)
    def _():
        acc_ref[...] = jnp.zeros_like(acc_ref)

    acc_ref[...] += jnp.dot(a_ref[...], w_ref[...], preferred_element_type=F32)

    @pl.when(kk == pl.num_programs(1) - 1)
    def _():
        h = h_ref[...] + acc_ref[...]
        ms = jnp.mean(h * h, axis=-1, keepdims=True)
        y_ref[...] = h * lax.rsqrt(ms + EPS) * g_ref[...]


def _ffn_down(act, w, h, g, tm, tk):
    m, d = h.shape
    kdim = act.shape[1]
    return pl.pallas_call(
        _ffn_down_kernel,
        grid=(m // tm, kdim // tk),
        in_specs=[pl.BlockSpec((tm, tk), lambda i, kk: (i, kk)),
                  pl.BlockSpec((tk, d), lambda i, kk: (kk, 0)),
                  pl.BlockSpec((tm, d), lambda i, kk: (i, 0)),
                  pl.BlockSpec((1, d), lambda i, kk: (0, 0))],
        out_specs=pl.BlockSpec((tm, d), lambda i, kk: (i, 0)),
        out_shape=jax.ShapeDtypeStruct((m, d), F32),
        scratch_shapes=[pltpu.VMEM((tm, d), F32)],
        compiler_params=_params(("arbitrary", "arbitrary")),
        name="ffn_down",
    )(act, w, h, g.reshape(1, d))


def _pad_cols(a, width):
    return jnp.pad(a, ((0, 0), (0, width - a.shape[1])))


def _split_pad_ff(a):
    return jnp.concatenate([_pad_cols(a[:, :D_FF], D_FF_P), _pad_cols(a[:, D_FF:], D_FF_P)], axis=1)


def _unpad_ff(u, g):
    return jnp.concatenate([u[..., :D_FF], g[..., :D_FF]], axis=-1)


def kernel(x_prompt, x_sample, state_gla, cache_dil_k, cache_dil_v, state_ffn_conv, norm_mix, w_in, w_gate_up,
           b_gate, gla_norm, w_out, norm_ffn, w_ffn_up, ffn_conv_w, ffn_conv_b, w_ffn_down, norm_final):
    bp, tp, d = x_prompt.shape
    bs, ts, _ = x_sample.shape
    l = 0

    z0 = 2 * GLA_QK + 2 * GLA_VW
    w_in_t = w_in[l].T
    w_dil_t = w_in_t[z0 + GLA_RANK:]
    w_z_t = jnp.pad(w_in_t[z0:z0 + GLA_RANK], ((0, Z_PAD - GLA_RANK), (0, 0)))
    wg_p = jnp.pad(w_gate_up[l], ((0, Z_PAD - GLA_RANK), (0, 0)))
    bg = b_gate[l].reshape(1, GLA_QK)
    gn = gla_norm[l].reshape(1, GLA_DV)
    wa = w_out[l][:GLA_VW].astype(BF16)
    wb = w_out[l][GLA_VW:].astype(BF16)
    cw_p = _split_pad_ff(ffn_conv_w[l])
    cb_p = _split_pad_ff(ffn_conv_b[l].reshape(1, -1))
    w_dn_p = jnp.pad(w_ffn_down[l], ((0, D_FF_P - D_FF), (0, 0))).astype(BF16)

    mp = bp * tp
    xp = x_prompt.reshape(mp, d)
    xn_mix_p = _rms(xp, norm_mix[l], 512)
    mix_p = _proj(xn_mix_p, w_in_t, 1024, 768, n=z0)
    dil_p = _proj(xn_mix_p, w_dil_t, 1024, 768)
    z_p = _proj(xn_mix_p, w_z_t, 1024, Z_PAD)
    cos_p, sin_p = _rope_tables(jnp.arange(tp))
    cos_p = jnp.tile(cos_p, (bp, 1))
    sin_p = jnp.tile(sin_p, (bp, 1))
    rope_out = _rope(dil_p, cos_p, sin_p, 512, n_seq=bp)
    q_nat, k_rot, k_nat, v_nat = rope_out[:4]
    by_class = {1: tuple(a.reshape(bp, 1, tp, DIL_W) for a in (q_nat, k_nat, v_nat))}
    for di, r in enumerate(CLASS_DILATIONS):
        by_class[r] = tuple(rope_out[4 + 3 * di:7 + 3 * di])

    oa_p, gla_p = _gla(mix_p.reshape(bp, tp, z0), z_p.reshape(bp, tp, Z_PAD), wg_p, bg, gn,
                       jnp.zeros((bp, GLA_HEADS, GLA_DK, GLA_DV), F32),
                       chunk=GLA_CHUNK, n_chunks=4, t_valid=tp)

    state = _dil_branch(*by_class[16], None, first=True, final=False)
    state = _dil_branch(*by_class[4], state, first=False, final=False)
    (ob_p,) = _dil_branch(*by_class[1], state, first=False, final=True)

    h_p, xn_p = _out_proj(oa_p.reshape(mp, GLA_VW), ob_p.reshape(mp, DIL_W), wa, wb, xp, norm_ffn[l], 512)
    act_p, tail_u, tail_g = _ffn_up_seq(xn_p, w_ffn_up[l], cw_p, cb_p, 1024, tp)
    y_p = _ffn_down(act_p, w_dn_p, h_p, norm_final, 512, 1408)

    buf_p = min(DIL_WINDOW, tp)
    y_prompt = y_p.reshape(bp, tp, d)
    new_gla_p = gla_p[None]
    new_k_p = k_rot.reshape(bp, tp, DIL_HEADS, HEAD_DIM)[None, :, tp - buf_p:]
    v_f32 = dil_p[:, 2 * DIL_W:]
    new_v_p = v_f32.reshape(bp, tp, DIL_HEADS, HEAD_DIM)[None, :, tp - buf_p:]
    new_conv_p = _unpad_ff(tail_u[:, SUBLANE - (CONV_W - 1):], tail_g[:, SUBLANE - (CONV_W - 1):])[None]

    ms_ = bs * ts
    xs = x_sample.reshape(ms_, d)
    xn_mix_s = _rms(xs, norm_mix[l], ms_)
    mix_s = _proj(xn_mix_s, w_in_t, ms_, 768, n=z0)
    dil_s = _proj(xn_mix_s, w_dil_t, ms_, 768)
    z_s = _proj(xn_mix_s, w_z_t, ms_, Z_PAD)
    cos_s, sin_s = _rope_tables(PAST_LEN + jnp.arange(ts))
    cos_s = jnp.tile(cos_s, (bs, 1))
    sin_s = jnp.tile(sin_s, (bs, 1))
    q_rs, k_rs = _rope(dil_s, cos_s, sin_s, ms_)

    pad_chunk = lambda a: jnp.pad(a.reshape(bs, ts, -1), ((0, 0), (0, GLA_CHUNK - ts), (0, 0)))
    oa_s, gla_s = _gla(pad_chunk(mix_s), pad_chunk(z_s), wg_p, bg, gn, state_gla[l],
                       chunk=GLA_CHUNK, n_chunks=1, t_valid=ts)
    oa_s = oa_s[:, :ts].reshape(ms_, GLA_VW)

    v_s = dil_s[:, 2 * DIL_W:]
    pad_rows = lambda a: jnp.pad(a.reshape(bs, ts, DIL_W), ((0, 0), (0, SAMPLE_ROWS - ts), (0, 0)))
    new_cols = lambda a: jnp.pad(a.reshape(bs, ts, DIL_HEADS, HEAD_DIM).transpose(0, 2, 3, 1),
                                 ((0, 0), (0, 0), (0, 0), (LANE - ts, 0)))
    time_minor = lambda a: a.transpose(0, 2, 3, 1)
    ob_s, k_s_t, v_s_t = _dil_sample(pad_rows(q_rs), pad_rows(k_rs), pad_rows(v_s), new_cols(k_rs), new_cols(v_s),
                                     time_minor(cache_dil_k[l]), time_minor(cache_dil_v[l]), ts)
    ob_s = ob_s[:, :ts].astype(BF16)

    h_s, xn_s = _out_proj(oa_s, ob_s.reshape(ms_, DIL_W), wa, wb, xs, norm_ffn[l], ms_)
    conv_hist = _split_pad_ff(state_ffn_conv[l].reshape(bs * (CONV_W - 1), 2 * D_FF))
    conv_hist = conv_hist.reshape(bs, CONV_W - 1, 2 * D_FF_P)
    zero_row = jnp.zeros((bs, 1, 2 * D_FF_P), F32)
    h1 = jnp.concatenate([conv_hist[:, 1:2]] + [zero_row] * (ts - 1), axis=1).reshape(ms_, 2 * D_FF_P)
    h2 = jnp.concatenate([conv_hist[:, 0:1], conv_hist[:, 1:2]] + [zero_row] * (ts - 2),
                         axis=1).reshape(ms_, 2 * D_FF_P)
    act_s, up_u, up_g = _ffn_up_hist(xn_s, w_ffn_up[l], cw_p, cb_p, h1, h2, ts)
    y_s = _ffn_down(act_s, w_dn_p, h_s, norm_final, ms_, 1408)

    y_sample = y_s.reshape(bs, ts, d)
    new_gla_s = gla_s[None]
    new_k_s = k_s_t.transpose(0, 3, 1, 2)[None]
    new_v_s = v_s_t.transpose(0, 3, 1, 2)[None]
    up_full = _unpad_ff(up_u, up_g).reshape(bs, ts, 2 * D_FF)
    new_conv_s = up_full[:, ts - (CONV_W - 1):][None]

    return (y_prompt, y_sample, new_gla_p, new_gla_s, new_k_p, new_k_s, new_v_p, new_v_s,
            new_conv_p, new_conv_s)
```

```python
import functools

import jax
import jax.numpy as jnp
from jax import lax
from jax.experimental import pallas as pl
from jax.experimental.pallas import tpu as pltpu

F32 = jnp.float32
BF16 = jnp.bfloat16

D_MODEL = 2048
HEAD_DIM = 64
GLA_HEADS = 10
GLA_DK = 64
GLA_DV = 128
GLA_RANK = 16
GLA_TAU = 16.0
GLA_CHUNK = 64
DIL_HEADS = 12
DIL_PAIRS = ((128, 1), (512, 4), (2048, 16))
DIL_WINDOW = 2048
ROPE_THETA = 10000.0
D_FF = 5504
PAST_LEN = 8192
CONV_W = 3
EPS = 1e-6

GLA_QK = GLA_HEADS * GLA_DK
GLA_VW = GLA_HEADS * GLA_DV
DIL_W = DIL_HEADS * HEAD_DIM

LANE = 128
SUBLANE = 8
VMEM_LIMIT = 56 * 1024 * 1024

Z_PAD = LANE
D_FF_P = 5632
FF_TN = 512
FF_NT = D_FF_P // FF_TN

NT_DIMS = (((1,), (1,)), ((), ()))


def _params(sem, vmem=VMEM_LIMIT, flags=None):
    return pltpu.CompilerParams(dimension_semantics=sem, vmem_limit_bytes=vmem, flags=flags)


def _rms_kernel(x_ref, g_ref, o_ref):
    x = x_ref[...]
    ms = jnp.mean(x * x, axis=-1, keepdims=True)
    o_ref[...] = (x * lax.rsqrt(ms + EPS) * g_ref[...]).astype(BF16)


def _rms(x, g, tm):
    m, d = x.shape
    return pl.pallas_call(
        _rms_kernel,
        grid=(m // tm,),
        in_specs=[pl.BlockSpec((tm, d), lambda i: (i, 0)), pl.BlockSpec((1, d), lambda i: (0, 0))],
        out_specs=pl.BlockSpec((tm, d), lambda i: (i, 0)),
        out_shape=jax.ShapeDtypeStruct((m, d), BF16),
        compiler_params=_params(("arbitrary",)),
        name="rms",
    )(x, g.reshape(1, d))


def _proj_kernel(x_ref, w_ref, o_ref, wbf_ref):
    @pl.when(pl.program_id(1) == 0)
    def _():
        wbf_ref[...] = w_ref[...].astype(BF16)

    o_ref[...] = lax.dot_general(x_ref[...], wbf_ref[...], NT_DIMS, preferred_element_type=F32)


def _proj(xn, wt, tm, tn, n=None):
    m, k = xn.shape
    n = wt.shape[0] if n is None else n
    return pl.pallas_call(
        _proj_kernel,
        grid=(n // tn, m // tm),
        in_specs=[pl.BlockSpec((tm, k), lambda j, i: (i, 0)), pl.BlockSpec((tn, k), lambda j, i: (j, 0))],
        out_specs=pl.BlockSpec((tm, tn), lambda j, i: (i, j)),
        out_shape=jax.ShapeDtypeStruct((m, n), F32),
        scratch_shapes=[pltpu.VMEM((tn, k), BF16)],
        compiler_params=_params(("arbitrary", "arbitrary")),
        name="proj",
    )(xn, wt)


CLASS_DILATIONS = tuple(r for _, r in DIL_PAIRS if r > 1)


def _rope_kernel(q_ref, k_ref, v_ref, cos_ref, sin_ref, *rest, by_class, tiles, first_kept):
    reps = DIL_W // LANE
    cos = jnp.concatenate([cos_ref[...]] * reps, axis=1)
    sin = jnp.concatenate([sin_ref[...]] * reps, axis=1)
    lane = lax.broadcasted_iota(jnp.int32, (1, DIL_W), 1)
    first_half = (lane % HEAD_DIM) < (HEAD_DIM // 2)

    def rot(x):
        partner = jnp.where(first_half,
                            pltpu.roll(x, DIL_W - HEAD_DIM // 2, axis=1),
                            pltpu.roll(x, HEAD_DIM // 2, axis=1))
        return x * cos + partner * sin

    q = rot(q_ref[...]) * (HEAD_DIM ** -0.5)
    k = rot(k_ref[...])
    if not by_class:
        qo_ref, ko_ref = rest
        qo_ref[...] = q
        ko_ref[...] = k
        return
    n_class = 3 * len(CLASS_DILATIONS)
    qb_ref, kb_ref, vb_ref = rest[:3]
    class_refs = rest[3:3 + n_class]
    kt_ref, vt_ref, sq_ref, sk_ref, sv_ref = rest[3 + n_class:]
    v = v_ref[...]
    qb_ref[...] = q.astype(BF16)
    kb_ref[...] = k.astype(BF16)
    vb_ref[...] = v.astype(BF16)

    @pl.when(pl.program_id(0) % tiles >= first_kept)
    def _():
        kt_ref[...] = k.T
        vt_ref[...] = v.T

    for src, val in ((sq_ref, q), (sk_ref, k), (sv_ref, v)):
        for j in range(reps):
            src[j] = val[:, j * LANE:(j + 1) * LANE]
    tm = q.shape[0]
    for di, r in enumerate(CLASS_DILATIONS):
        for src, dst in zip((sq_ref, sk_ref, sv_ref), class_refs[3 * di:3 * di + 3]):
            for c in range(r):
                rows = pl.ds(c, tm // r, stride=r)
                dst[c] = jnp.concatenate([src[j, rows, :] for j in range(reps)], axis=1).astype(BF16)


def _rope(proj, cos_t, sin_t, tm, n_seq=None):
    m = proj.shape[0]
    by_class = n_seq is not None
    blk = lambda c: pl.BlockSpec((tm, DIL_W), lambda i, c=c: (i, c))
    tab = pl.BlockSpec((tm, LANE), lambda i: (i, 0))
    out = pl.BlockSpec((tm, DIL_W), lambda i: (i, 0))
    tiles, first_kept, scratch = 1, 0, []
    if by_class:
        t_seq = m // n_seq
        tiles = t_seq // tm
        window = min(DIL_WINDOW, t_seq)
        first_kept = (t_seq - window) // tm
        out_specs = [out, out, out]
        out_shape = [jax.ShapeDtypeStruct((m, DIL_W), BF16)] * 3
        for r in CLASS_DILATIONS:
            spec = pl.BlockSpec((None, r, tm // r, DIL_W), lambda i: (i // tiles, 0, i % tiles, 0))
            out_specs += [spec] * 3
            out_shape += [jax.ShapeDtypeStruct((n_seq, r, t_seq // r, DIL_W), BF16)] * 3
        kept = pl.BlockSpec((None, DIL_W, tm), lambda i: (i // tiles, 0, jnp.maximum(i % tiles - first_kept, 0)))
        out_specs += [kept] * 2
        out_shape += [jax.ShapeDtypeStruct((n_seq, DIL_W, window), F32)] * 2
        scratch = [pltpu.VMEM((DIL_W // LANE, tm, LANE), F32)] * 3
    else:
        out_specs = [out, out]
        out_shape = [jax.ShapeDtypeStruct((m, DIL_W), F32)] * 2
    return pl.pallas_call(
        functools.partial(_rope_kernel, by_class=by_class, tiles=tiles, first_kept=first_kept),
        grid=(m // tm,),
        in_specs=[blk(0), blk(1), blk(2), tab, tab],
        out_specs=out_specs,
        out_shape=out_shape,
        scratch_shapes=scratch,
        compiler_params=_params(("arbitrary",)),
        name="rope",
    )(proj, proj, proj, cos_t, sin_t)


def _rope_tables(pos):
    half = HEAD_DIM // 2
    inv_freq = ROPE_THETA ** (-2.0 * jnp.arange(half, dtype=F32) / HEAD_DIM)
    ang = pos.astype(F32)[:, None] * inv_freq[None, :]
    cos = jnp.cos(ang)
    sin = jnp.sin(ang)
    cos_t = jnp.concatenate([cos, cos, cos, cos], axis=1)
    sin_t = jnp.concatenate([-sin, sin, -sin, sin], axis=1)
    return cos_t, sin_t


def _gla_kernel(q_ref, k_ref, v_ref, r_ref, z_ref, wg_ref, bg_ref, gn_ref, s0_ref,
                o_ref, sfin_ref, st_ref, *, chunk, n_chunks, t_valid):
    i = pl.program_id(1)

    @pl.when(i == 0)
    def _():
        for h in range(GLA_HEADS):
            st_ref[h] = s0_ref[h].T

    row = lax.broadcasted_iota(jnp.int32, (chunk, chunk), 0)
    col = lax.broadcasted_iota(jnp.int32, (chunk, chunk), 1)
    causal = row >= col
    tril = causal.astype(BF16)
    t_in = lax.broadcasted_iota(jnp.int32, (chunk, 1), 0)
    mm = functools.partial(jnp.dot, preferred_element_type=F32)

    def split(a):
        hi = a.astype(BF16)
        return hi, (a - hi.astype(F32)).astype(BF16)

    for c in range(n_chunks):
        sl = pl.ds(c * chunk, chunk)
        z_hi, z_lo = split(z_ref[sl, :])
        x = mm(z_hi, wg_ref[0]) + mm(z_hi, wg_ref[1]) + mm(z_lo, wg_ref[0]) + bg_ref[...]
        log_a = -(jnp.maximum(-x, 0.0) + jnp.log1p(jnp.exp(-jnp.abs(x)))) / GLA_TAU
        t_abs = (i * n_chunks + c) * chunk + t_in
        log_a = jnp.where(t_abs < t_valid, log_a, 0.0)
        a_hi, a_lo = split(log_a)
        cum = mm(tril, a_hi) + mm(tril, a_lo)
        last = cum[chunk - 1:chunk, :]
        q = q_ref[sl, :] * (GLA_DK ** -0.5)
        k = k_ref[sl, :]
        q_dec = q * jnp.exp(cum)
        k_inv = k * jnp.exp(-cum)
        k_end = k * jnp.exp(last - cum)
        a_end = jnp.exp(last)
        heads = range(GLA_HEADS)
        ks = [slice(h * GLA_DK, (h + 1) * GLA_DK) for h in heads]
        vs = [slice(h * GLA_DV, (h + 1) * GLA_DV) for h in heads]
        nt = functools.partial(lax.dot_general, dimension_numbers=NT_DIMS, preferred_element_type=F32)
        qh = [q_dec[:, ks[h]].astype(BF16) for h in heads]
        v_f = [v_ref[sl, vs[h]] for h in heads]
        vh = [v_f[h].astype(BF16) for h in heads]
        scores = [nt(qh[h], k_inv[:, ks[h]].astype(BF16)) for h in heads]
        scores = [jnp.where(causal, scores[h], 0.0).astype(BF16) for h in heads]
        st = [st_ref[h] for h in heads]
        o = [mm(scores[h], vh[h]) + nt(qh[h], st[h].astype(BF16)) for h in heads]
        for h in heads:
            st_ref[h] = st[h] * a_end[:, ks[h]] + mm(v_f[h].T.astype(BF16), k_end[:, ks[h]].astype(BF16))
        ms = [jnp.mean(o[h] * o[h], axis=-1, keepdims=True) for h in heads]
        for h in heads:
            on = o[h] * lax.rsqrt(ms[h] + EPS) * gn_ref[...]
            r = r_ref[sl, vs[h]]
            o_ref[sl, vs[h]] = (on * (r / (1.0 + jnp.exp(-r)))).astype(BF16)

    @pl.when(i == pl.num_programs(1) - 1)
    def _():
        for h in range(GLA_HEADS):
            sfin_ref[h] = st_ref[h].T


def _gla(src, zsrc, wg, bg, gn, s0, *, chunk, n_chunks, t_valid):
    b, t, _ = src.shape
    tb = chunk * n_chunks
    col = lambda w, c: pl.BlockSpec((None, tb, w), lambda bi, i, c=c: (bi, i, c))
    full = lambda shape: pl.BlockSpec(shape, lambda bi, i: (0,) * len(shape))
    state = pl.BlockSpec((None, GLA_HEADS, GLA_DK, GLA_DV), lambda bi, i: (bi, 0, 0, 0))
    kern = functools.partial(_gla_kernel, chunk=chunk, n_chunks=n_chunks, t_valid=t_valid)
    return pl.pallas_call(
        kern,
        grid=(b, t // tb),
        in_specs=[col(GLA_QK, 0), col(GLA_QK, 1), col(GLA_VW, 1), col(GLA_VW, 2), col(Z_PAD, 0),
                  full((2, Z_PAD, GLA_QK)), full((1, GLA_QK)), full((1, GLA_DV)), state],
        out_specs=[pl.BlockSpec((None, tb, GLA_VW), lambda bi, i: (bi, i, 0)), state],
        out_shape=[jax.ShapeDtypeStruct((b, t, GLA_VW), BF16),
                   jax.ShapeDtypeStruct((b, GLA_HEADS, GLA_DK, GLA_DV), F32)],
        scratch_shapes=[pltpu.VMEM((GLA_HEADS, GLA_DV, GLA_DK), F32)],
        compiler_params=_params(("arbitrary", "arbitrary")),
        name="gla",
    )(src, src, src, src, zsrc, wg, bg, gn, s0)


QB = 128


def _dil_branch_kernel(*refs, first, final, dilation):
    if first:
        q_ref, kp_ref, kc_ref, vp_ref, vc_ref = refs[:5]
        outs = refs[5:]
    else:
        q_ref, kp_ref, kc_ref, vp_ref, vc_ref, num_ref, m_ref, d_ref = refs[:8]
        outs = refs[8:]
    i = pl.program_id(1)
    rows = pl.ds(pl.program_id(2), QB, stride=dilation) if dilation > 1 else slice(None)
    k2 = jnp.concatenate([kp_ref[...], kc_ref[...]], axis=0)
    v2 = jnp.concatenate([vp_ref[...], vc_ref[...]], axis=0)
    q = q_ref[...]
    qi = QB + lax.broadcasted_iota(jnp.int32, (QB, 1), 0)
    kj = lax.broadcasted_iota(jnp.int32, (1, 2 * QB), 1)
    off = qi - kj
    valid = (off >= 0) & (off <= QB) & ((kj >= QB) | (i > 0))
    lane = lax.broadcasted_iota(jnp.int32, (1, LANE), 1)
    lo = lane < HEAD_DIM
    heads = range(DIL_HEADS)
    cs = [slice(p * LANE, (p + 1) * LANE) for p in range(DIL_HEADS // 2)]
    sel = [lo if h % 2 == 0 else jnp.logical_not(lo) for h in heads]
    qm = [jnp.where(sel[h], q[:, cs[h // 2]], jnp.zeros((QB, LANE), BF16)) for h in heads]
    s = [lax.dot_general(qm[h], k2[:, cs[h // 2]], NT_DIMS, preferred_element_type=F32) for h in heads]
    s = [jnp.where(valid, s[h], -jnp.inf) for h in heads]
    m_new = [jnp.max(s[h], axis=-1, keepdims=True) for h in heads]
    if not first:
        m_in = m_ref[rows, :]
        d_in = d_ref[rows, :]
        m_old = [jnp.sum(jnp.where(lane == h, m_in, 0.0), axis=-1, keepdims=True) for h in heads]
        d_old = [jnp.sum(jnp.where(lane == h, d_in, 0.0), axis=-1, keepdims=True) for h in heads]
        m_new = [jnp.maximum(m_new[h], m_old[h]) for h in heads]
    pr = [jnp.exp(s[h] - m_new[h]) for h in heads]
    d_new = [jnp.sum(pr[h], axis=-1, keepdims=True) for h in heads]
    if not first:
        alpha = [jnp.exp(m_old[h] - m_new[h]) for h in heads]
        d_new = [d_new[h] + d_old[h] * alpha[h] for h in heads]
    pv = [jnp.dot(pr[h].astype(BF16), v2[:, cs[h // 2]], preferred_element_type=F32) for h in heads]
    pairs = []
    for p in range(DIL_HEADS // 2):
        num = jnp.where(lo, pv[2 * p], pv[2 * p + 1])
        if not first:
            num = num + num_ref[p, rows, :] * jnp.where(lo, alpha[2 * p], alpha[2 * p + 1])
        if final:
            num = num / jnp.where(lo, d_new[2 * p], d_new[2 * p + 1])
        else:
            outs[0][p, rows, :] = num
        pairs.append(num)
    if final:
        outs[0][...] = jnp.concatenate(pairs, axis=1).astype(BF16)
    else:
        outs[1][rows, :] = functools.reduce(jnp.add, [jnp.where(lane == h, m_new[h], 0.0) for h in heads])
        outs[2][rows, :] = functools.reduce(jnp.add, [jnp.where(lane == h, d_new[h], 0.0) for h in heads])


def _dil_branch(q, k, v, state, *, first, final):
    b, r, ts, _ = q.shape
    t = r * ts
    nq = ts // QB
    cur = pl.BlockSpec((None, None, QB, DIL_W), lambda bi, i, c: (bi, c, i, 0))
    prev = pl.BlockSpec((None, None, QB, DIL_W), lambda bi, i, c: (bi, c, jnp.maximum(i - 1, 0), 0))
    nat = lambda w: pl.BlockSpec((None, r * QB, w), lambda bi, i, c: (bi, i, 0))
    nat_num = pl.BlockSpec((None, DIL_W // LANE, r * QB, LANE), lambda bi, i, c: (bi, 0, i, 0))
    args = [q, k, k, v, v]
    in_specs = [cur, prev, cur, prev, cur]
    if not first:
        args += list(state)
        in_specs += [nat_num, nat(LANE), nat(LANE)]
    if final:
        assert r == 1
        out_specs = [nat(DIL_W)]
        out_shape = [jax.ShapeDtypeStruct((b, t, DIL_W), BF16)]
    else:
        out_specs = [nat_num, nat(LANE), nat(LANE)]
        out_shape = [jax.ShapeDtypeStruct((b, DIL_W // LANE, t, LANE), F32),
                     jax.ShapeDtypeStruct((b, t, LANE), F32),
                     jax.ShapeDtypeStruct((b, t, LANE), F32)]
    return pl.pallas_call(
        functools.partial(_dil_branch_kernel, first=first, final=final, dilation=r),
        grid=(b, nq, r),
        in_specs=in_specs,
        out_specs=out_specs,
        out_shape=out_shape,
        compiler_params=_params(("arbitrary", "arbitrary", "arbitrary")),
        name=f"dil_branch_r{r}",
    )(*args)


SAMPLE_HB = 6
SAMPLE_ROWS = 16


def _dil_sample_kernel(q_ref, kn_ref, vn_ref, knt_ref, vnt_ref, kc_ref, vc_ref, o_ref, ko_ref, vo_ref,
                       *, t_new, hist):
    t_row = lax.broadcasted_iota(jnp.int32, (SAMPLE_ROWS, 1), 0)
    j = lax.broadcasted_iota(jnp.int32, (1, hist), 1)
    lane = lax.broadcasted_iota(jnp.int32, (1, LANE), 1)
    keep = lane < LANE - t_new

    def weight(off):
        w = jnp.zeros(off.shape, F32)
        for win, dil in DIL_PAIRS:
            w = w + jnp.where((off >= 0) & (off <= win) & (off % dil == 0), 1.0, 0.0)
        return w

    w_c = weight(hist + t_row - j)
    w_n = [weight(t_row - t) for t in range(t_new)]
    heads = range(SAMPLE_HB)
    hs = [slice(h * HEAD_DIM, (h + 1) * HEAD_DIM) for h in heads]
    qs = [q_ref[:, hs[h]] for h in heads]
    s_c = [jnp.dot(qs[h].astype(BF16), kc_ref[h].astype(BF16), preferred_element_type=F32) for h in heads]
    s_c = [jnp.where(w_c > 0, s, -jnp.inf) for s in s_c]
    s_n = [[jnp.where(w_n[t] > 0, jnp.sum(qs[h] * kn_ref[t:t + 1, hs[h]], axis=-1, keepdims=True), -jnp.inf)
            for t in range(t_new)] for h in heads]
    m = [functools.reduce(jnp.maximum, s_n[h], jnp.max(s_c[h], axis=-1, keepdims=True)) for h in heads]
    p_c = [w_c * jnp.exp(s_c[h] - m[h]) for h in heads]
    p_n = [[w_n[t] * jnp.exp(s_n[h][t] - m[h]) for t in range(t_new)] for h in heads]
    den = [functools.reduce(jnp.add, p_n[h], jnp.sum(p_c[h], axis=-1, keepdims=True)) for h in heads]
    num = [lax.dot_general(p_c[h].astype(BF16), vc_ref[h].astype(BF16), NT_DIMS, preferred_element_type=F32)
           for h in heads]
    num = [functools.reduce(jnp.add, [p_n[h][t] * vn_ref[t:t + 1, hs[h]] for t in range(t_new)], num[h])
           for h in heads]
    o_ref[...] = jnp.concatenate([num[h] / den[h] for h in heads], axis=1)

    for h in heads:
        for src, new_ref, dst in ((kc_ref, knt_ref, ko_ref), (vc_ref, vnt_ref, vo_ref)):
            nxt = pltpu.roll(src[h, :, 0:LANE], LANE - t_new, axis=1)
            for c in range(hist // LANE):
                cur_t = nxt
                if c + 1 < hist // LANE:
                    nxt = pltpu.roll(src[h, :, (c + 1) * LANE:(c + 2) * LANE], LANE - t_new, axis=1)
                else:
                    nxt = new_ref[h]
                dst[h, :, c * LANE:(c + 1) * LANE] = jnp.where(keep, cur_t, nxt)


def _dil_sample(q, kn, vn, knt, vnt, cache_kt, cache_vt, t_new):
    b, nh, _, hist = cache_kt.shape
    w = SAMPLE_HB * HEAD_DIM
    small = pl.BlockSpec((None, SAMPLE_ROWS, w), lambda bi, c: (bi, 0, c))
    newt = pl.BlockSpec((None, SAMPLE_HB, HEAD_DIM, LANE), lambda bi, c: (bi, c, 0, 0))
    big = pl.BlockSpec((None, SAMPLE_HB, HEAD_DIM, hist), lambda bi, c: (bi, c, 0, 0))
    return pl.pallas_call(
        functools.partial(_dil_sample_kernel, t_new=t_new, hist=hist),
        grid=(b, nh // SAMPLE_HB),
        in_specs=[small, small, small, newt, newt, big, big],
        out_specs=[small, big, big],
        out_shape=[jax.ShapeDtypeStruct((b, SAMPLE_ROWS, nh * HEAD_DIM), F32),
                   jax.ShapeDtypeStruct(cache_kt.shape, F32),
                   jax.ShapeDtypeStruct(cache_vt.shape, F32)],
        compiler_params=_params(("arbitrary", "arbitrary")),
        name="dil_sample",
    )(q, kn, vn, knt, vnt, cache_kt, cache_vt)


def _out_proj_kernel(oa_ref, ob_ref, wa_ref, wb_ref, x_ref, g_ref, h_ref, xn_ref):
    y = (jnp.dot(oa_ref[...], wa_ref[...], preferred_element_type=F32)
         + jnp.dot(ob_ref[...], wb_ref[...], preferred_element_type=F32))
    h = x_ref[...] + y
    h_ref[...] = h
    ms = jnp.mean(h * h, axis=-1, keepdims=True)
    xn_ref[...] = (h * lax.rsqrt(ms + EPS) * g_ref[...]).astype(BF16)


def _out_proj(oa, ob, wa, wb, x, g, tm):
    m, d = x.shape
    row = lambda w: pl.BlockSpec((tm, w), lambda i: (i, 0))
    full = lambda a: pl.BlockSpec(a.shape, lambda i: (0, 0))
    g2 = g.reshape(1, d)
    return pl.pallas_call(
        _out_proj_kernel,
        grid=(m // tm,),
        in_specs=[row(GLA_VW), row(DIL_W), full(wa), full(wb), row(d), full(g2)],
        out_specs=[row(d), row(d)],
        out_shape=[jax.ShapeDtypeStruct((m, d), F32), jax.ShapeDtypeStruct((m, d), BF16)],
        compiler_params=_params(("arbitrary",)),
        name="out_proj",
    )(oa, ob, wa, wb, x, g2)


FF_ROWS = 256
FF_COLS = FF_TN
FF_SUB = FF_TN // LANE
FF_TILES = D_FF // LANE


def _conv_taps(cw, cb, up, prev1, prev2):
    return cb + cw[0:1] * prev2 + cw[1:2] * prev1 + cw[2:3] * up


def _swiglu(u, g):
    return (g / (1.0 + jnp.exp(-g)) * u).astype(BF16)


def _stage_up_weights(w_blocks, wbf_ref, j):
    for half in range(2):
        for q in range(FF_SUB):
            blk = w_blocks[half * FF_SUB + q][...]
            if (FF_NT - 1) * FF_SUB + q >= FF_TILES:
                blk = jnp.where(j * FF_SUB + q < FF_TILES, blk, 0.0)
            wbf_ref[half, :, q * LANE:(q + 1) * LANE] = blk.astype(BF16)


def _up_weight_specs(tile_of_step):
    specs = []
    for half in range(2):
        for q in range(FF_SUB):
            def index(*ids, half=half, q=q):
                blk = jnp.minimum(tile_of_step(*ids) * FF_SUB + q, FF_TILES - 1)
                return (0, half * FF_TILES + blk)
            specs.append(pl.BlockSpec((D_MODEL, LANE), index))
    return specs


def _ffn_up_seq_kernel(*refs, tiles_per_seq):
    x_ref = refs[0]
    w_blocks = refs[1:1 + 2 * FF_SUB]
    cwu_ref, cwg_ref, cbu_ref, cbg_ref, act_ref, tu_ref, tg_ref, wbf_ref, carry_ref = refs[1 + 2 * FF_SUB:]
    j = pl.program_id(0)
    i = pl.program_id(1)
    tm = x_ref.shape[0]

    @pl.when(i == 0)
    def _():
        _stage_up_weights(w_blocks, wbf_ref, j)

    @pl.when(i % tiles_per_seq == 0)
    def _():
        carry_ref[...] = jnp.zeros_like(carry_ref)

    for c0 in range(0, FF_TN, FF_COLS):
        cols = slice(c0, c0 + FF_COLS)
        carry = [carry_ref[0, :, cols], carry_ref[1, :, cols]]
        for r0 in range(0, tm, FF_ROWS):
            x = x_ref[r0:r0 + FF_ROWS, :]
            conv = []
            for idx, (cw_ref, cb_ref) in enumerate(((cwu_ref, cbu_ref), (cwg_ref, cbg_ref))):
                up = jnp.dot(x, wbf_ref[idx, :, cols], preferred_element_type=F32)
                ext = jnp.concatenate([carry[idx], up], axis=0)
                prev1 = pltpu.roll(ext, 1, axis=0)[SUBLANE:]
                prev2 = pltpu.roll(ext, 2, axis=0)[SUBLANE:]
                carry[idx] = up[FF_ROWS - SUBLANE:]
                conv.append(_conv_taps(cw_ref[:, cols], cb_ref[:, cols], up, prev1, prev2))
            act_ref[r0:r0 + FF_ROWS, cols] = _swiglu(*conv)
        for idx, tail_ref in enumerate((tu_ref, tg_ref)):
            carry_ref[idx, :, cols] = carry[idx]
            tail_ref[:, cols] = carry[idx]


def _ffn_up_seq(xn, w_up, conv_w, conv_b, tm, t_seq):
    m, d = xn.shape
    tiles_per_seq = t_seq // tm
    xs = pl.BlockSpec((tm, d), lambda j, i: (i, 0))
    cu = lambda r: pl.BlockSpec((r, FF_TN), lambda j, i: (0, j))
    cg = lambda r: pl.BlockSpec((r, FF_TN), lambda j, i: (0, FF_NT + j))
    act = pl.BlockSpec((tm, FF_TN), lambda j, i: (i, j))
    tail = pl.BlockSpec((None, SUBLANE, FF_TN), lambda j, i: (i // tiles_per_seq, 0, j))
    w_specs = _up_weight_specs(lambda j, i: j)
    return pl.pallas_call(
        functools.partial(_ffn_up_seq_kernel, tiles_per_seq=tiles_per_seq),
        grid=(FF_NT, m // tm),
        in_specs=[xs] + w_specs + [cu(CONV_W), cg(CONV_W), cu(1), cg(1)],
        out_specs=[act, tail, tail],
        out_shape=[jax.ShapeDtypeStruct((m, D_FF_P), BF16)]
        + [jax.ShapeDtypeStruct((m // t_seq, SUBLANE, D_FF_P), F32)] * 2,
        scratch_shapes=[pltpu.VMEM((2, d, FF_TN), BF16), pltpu.VMEM((2, SUBLANE, FF_TN), F32)],
        compiler_params=_params(("arbitrary", "arbitrary")),
        name="ffn_up",
    )(xn, *([w_up] * len(w_specs)), conv_w, conv_w, conv_b, conv_b)


def _ffn_up_hist_kernel(*refs, t_seq):
    x_ref = refs[0]
    w_blocks = refs[1:1 + 2 * FF_SUB]
    (cwu_ref, cwg_ref, cbu_ref, cbg_ref, h1u_ref, h1g_ref, h2u_ref, h2g_ref,
     act_ref, upu_ref, upg_ref, wbf_ref) = refs[1 + 2 * FF_SUB:]
    _stage_up_weights(w_blocks, wbf_ref, pl.program_id(0))
    x = x_ref[...]
    t_in = lax.broadcasted_iota(jnp.int32, (x.shape[0], 1), 0) % t_seq
    conv = []
    for idx, (cw_ref, cb_ref, h1_ref, h2_ref, up_ref) in enumerate((
            (cwu_ref, cbu_ref, h1u_ref, h2u_ref, upu_ref), (cwg_ref, cbg_ref, h1g_ref, h2g_ref, upg_ref))):
        up = jnp.dot(x, wbf_ref[idx], preferred_element_type=F32)
        prev1 = jnp.where(t_in >= 1, pltpu.roll(up, 1, axis=0), 0.0) + h1_ref[...]
        prev2 = jnp.where(t_in >= 2, pltpu.roll(up, 2, axis=0), 0.0) + h2_ref[...]
        up_ref[...] = up
        conv.append(_conv_taps(cw_ref[...], cb_ref[...], up, prev1, prev2))
    act_ref[...] = _swiglu(*conv)


def _ffn_up_hist(xn, w_up, conv_w, conv_b, h1, h2, t_seq):
    m, d = xn.shape
    xs = pl.BlockSpec((m, d), lambda j: (0, 0))
    u = lambda r: pl.BlockSpec((r, FF_TN), lambda j: (0, j))
    g = lambda r: pl.BlockSpec((r, FF_TN), lambda j: (0, FF_NT + j))
    w_specs = _up_weight_specs(lambda j: j)
    return pl.pallas_call(
        functools.partial(_ffn_up_hist_kernel, t_seq=t_seq),
        grid=(FF_NT,),
        in_specs=[xs] + w_specs + [u(CONV_W), g(CONV_W), u(1), g(1), u(m), g(m), u(m), g(m)],
        out_specs=[u(m), u(m), u(m)],
        out_shape=[jax.ShapeDtypeStruct((m, D_FF_P), BF16)] + [jax.ShapeDtypeStruct((m, D_FF_P), F32)] * 2,
        scratch_shapes=[pltpu.VMEM((2, d, FF_TN), BF16)],
        compiler_params=_params(("arbitrary",)),
        name="ffn_up_hist",
    )(xn, *([w_up] * len(w_specs)), conv_w, conv_w, conv_b, conv_b, h1, h1, h2, h2)


def _ffn_down_kernel(a_ref, w_ref, h_ref, g_ref, y_ref, acc_ref):
    kk = pl.program_id(1)

    @pl.when(kk == 0)
    def _():
        acc_ref[...] = jnp.zeros_like(acc_ref)

    acc_ref[...] += jnp.dot(a_ref[...], w_ref[...], preferred_element_type=F32)

    @pl.when(kk == pl.num_programs(1) - 1)
    def _():
        h = h_ref[...] + acc_ref[...]
        ms = jnp.mean(h * h, axis=-1, keepdims=True)
        y_ref[...] = h * lax.rsqrt(ms + EPS) * g_ref[...]


def _ffn_down(act, w, h, g, tm, tk):
    m, d = h.shape
    kdim = act.shape[1]
    return pl.pallas_call(
        _ffn_down_kernel,
        grid=(m // tm, kdim // tk),
        in_specs=[pl.BlockSpec((tm, tk), lambda i, kk: (i, kk)),
                  pl.BlockSpec((tk, d), lambda i, kk: (kk, 0)),
                  pl.BlockSpec((tm, d), lambda i, kk: (i, 0)),
                  pl.BlockSpec((1, d), lambda i, kk: (0, 0))],
        out_specs=pl.BlockSpec((tm, d), lambda i, kk: (i, 0)),
        out_shape=jax.ShapeDtypeStruct((m, d), F32),
        scratch_shapes=[pltpu.VMEM((tm, d), F32)],
        compiler_params=_params(("arbitrary", "arbitrary")),
        name="ffn_down",
    )(act, w, h, g.reshape(1, d))


def _pad_cols(a, width):
    return jnp.pad(a, ((0, 0), (0, width - a.shape[1])))


def _split_pad_ff(a):
    return jnp.concatenate([_pad_cols(a[:, :D_FF], D_FF_P), _pad_cols(a[:, D_FF:], D_FF_P)], axis=1)


def _unpad_ff(u, g):
    return jnp.concatenate([u[..., :D_FF], g[..., :D_FF]], axis=-1)


def kernel(x_prompt, x_sample, state_gla, cache_dil_k, cache_dil_v, state_ffn_conv, norm_mix, w_in, w_gate_up,
           b_gate, gla_norm, w_out, norm_ffn, w_ffn_up, ffn_conv_w, ffn_conv_b, w_ffn_down, norm_final):
    bp, tp, d = x_prompt.shape
    bs, ts, _ = x_sample.shape
    l = 0

    z0 = 2 * GLA_QK + 2 * GLA_VW
    w_in_t = w_in[l].T
    w_dil_t = w_in_t[z0 + GLA_RANK:]
    w_z_t = jnp.pad(w_in_t[z0:z0 + GLA_RANK], ((0, Z_PAD - GLA_RANK), (0, 0)))
    wg_f = jnp.pad(w_gate_up[l], ((0, Z_PAD - GLA_RANK), (0, 0)))
    wg_hi = wg_f.astype(BF16)
    wg_p = jnp.stack([wg_hi, (wg_f - wg_hi.astype(F32)).astype(BF16)])
    bg = b_gate[l].reshape(1, GLA_QK)
    gn = gla_norm[l].reshape(1, GLA_DV)
    wa = w_out[l][:GLA_VW].astype(BF16)
    wb = w_out[l][GLA_VW:].astype(BF16)
    cw_p = _split_pad_ff(ffn_conv_w[l])
    cb_p = _split_pad_ff(ffn_conv_b[l].reshape(1, -1))
    w_dn_p = jnp.pad(w_ffn_down[l], ((0, D_FF_P - D_FF), (0, 0))).astype(BF16)

    mp = bp * tp
    xp = x_prompt.reshape(mp, d)
    xn_mix_p = _rms(xp, norm_mix[l], 512)
    mix_p = _proj(xn_mix_p, w_in_t, 1024, 768, n=z0)
    dil_p = _proj(xn_mix_p, w_dil_t, 1024, 768)
    z_p = _proj(xn_mix_p, w_z_t, 1024, Z_PAD)
    cos_p, sin_p = _rope_tables(jnp.arange(tp))
    cos_p = jnp.tile(cos_p, (bp, 1))
    sin_p = jnp.tile(sin_p, (bp, 1))
    rope_out = _rope(dil_p, cos_p, sin_p, 512, n_seq=bp)
    by_class = {1: tuple(a.reshape(bp, 1, tp, DIL_W) for a in rope_out[:3])}
    for di, r in enumerate(CLASS_DILATIONS):
        by_class[r] = tuple(rope_out[3 + 3 * di:6 + 3 * di])
    k_kept_t, v_kept_t = rope_out[-2:]

    oa_p, gla_p = _gla(mix_p.reshape(bp, tp, z0), z_p.reshape(bp, tp, Z_PAD), wg_p, bg, gn,
                       jnp.zeros((bp, GLA_HEADS, GLA_DK, GLA_DV), F32),
                       chunk=GLA_CHUNK, n_chunks=4, t_valid=tp)

    state = _dil_branch(*by_class[16], None, first=True, final=False)
    state = _dil_branch(*by_class[4], state, first=False, final=False)
    (ob_p,) = _dil_branch(*by_class[1], state, first=False, final=True)

    h_p, xn_p = _out_proj(oa_p.reshape(mp, GLA_VW), ob_p.reshape(mp, DIL_W), wa, wb, xp, norm_ffn[l], 512)
    act_p, tail_u, tail_g = _ffn_up_seq(xn_p, w_ffn_up[l], cw_p, cb_p, 1024, tp)
    y_p = _ffn_down(act_p, w_dn_p, h_p, norm_final, 512, 1408)

    buf_p = min(DIL_WINDOW, tp)
    y_prompt = y_p.reshape(bp, tp, d)
    new_gla_p = gla_p[None]
    new_k_p = k_kept_t.reshape(bp, DIL_HEADS, HEAD_DIM, buf_p).transpose(0, 3, 1, 2)[None]
    new_v_p = v_kept_t.reshape(bp, DIL_HEADS, HEAD_DIM, buf_p).transpose(0, 3, 1, 2)[None]
    new_conv_p = _unpad_ff(tail_u[:, SUBLANE - (CONV_W - 1):], tail_g[:, SUBLANE - (CONV_W - 1):])[None]

    ms_ = bs * ts
    xs = x_sample.reshape(ms_, d)
    xn_mix_s = _rms(xs, norm_mix[l], ms_)
    mix_s = _proj(xn_mix_s, w_in_t, ms_, 768, n=z0)
    dil_s = _proj(xn_mix_s, w_dil_t, ms_, 768)
    z_s = _proj(xn_mix_s, w_z_t, ms_, Z_PAD)
    cos_s, sin_s = _rope_tables(PAST_LEN + jnp.arange(ts))
    cos_s = jnp.tile(cos_s, (bs, 1))
    sin_s = jnp.tile(sin_s, (bs, 1))
    q_rs, k_rs = _rope(dil_s, cos_s, sin_s, ms_)

    pad_chunk = lambda a: jnp.pad(a.reshape(bs, ts, -1), ((0, 0), (0, GLA_CHUNK - ts), (0, 0)))
    oa_s, gla_s = _gla(pad_chunk(mix_s), pad_chunk(z_s), wg_p, bg, gn, state_gla[l],
                       chunk=GLA_CHUNK, n_chunks=1, t_valid=ts)
    oa_s = oa_s[:, :ts].reshape(ms_, GLA_VW)

    v_s = dil_s[:, 2 * DIL_W:]
    pad_rows = lambda a: jnp.pad(a.reshape(bs, ts, DIL_W), ((0, 0), (0, SAMPLE_ROWS - ts), (0, 0)))
    new_cols = lambda a: jnp.pad(a.reshape(bs, ts, DIL_HEADS, HEAD_DIM).transpose(0, 2, 3, 1),
                                 ((0, 0), (0, 0), (0, 0), (LANE - ts, 0)))
    time_minor = lambda a: a.transpose(0, 2, 3, 1)
    ob_s, k_s_t, v_s_t = _dil_sample(pad_rows(q_rs), pad_rows(k_rs), pad_rows(v_s), new_cols(k_rs), new_cols(v_s),
                                     time_minor(cache_dil_k[l]), time_minor(cache_dil_v[l]), ts)
    ob_s = ob_s[:, :ts].astype(BF16)

    h_s, xn_s = _out_proj(oa_s, ob_s.reshape(ms_, DIL_W), wa, wb, xs, norm_ffn[l], ms_)
    conv_hist = _split_pad_ff(state_ffn_conv[l].reshape(bs * (CONV_W - 1), 2 * D_FF))
    conv_hist = conv_hist.reshape(bs, CONV_W - 1, 2 * D_FF_P)
    zero_row = jnp.zeros((bs, 1, 2 * D_FF_P), F32)
    h1 = jnp.concatenate([conv_hist[:, 1:2]] + [zero_row] * (ts - 1), axis=1).reshape(ms_, 2 * D_FF_P)
    h2 = jnp.concatenate([conv_hist[:, 0:1], conv_hist[:, 1:2]] + [zero_row] * (ts - 2),
                         axis=1).reshape(ms_, 2 * D_FF_P)
    act_s, up_u, up_g = _ffn_up_hist(xn_s, w_ffn_up[l], cw_p, cb_p, h1, h2, ts)
    y_s = _ffn_down(act_s, w_dn_p, h_s, norm_final, ms_, 1408)

    y_sample = y_s.reshape(bs, ts, d)
    new_gla_s = gla_s[None]
    new_k_s = k_s_t.transpose(0, 3, 1, 2)[None]
    new_v_s = v_s_t.transpose(0, 3, 1, 2)[None]
    up_full = _unpad_ff(up_u, up_g).reshape(bs, ts, 2 * D_FF)
    new_conv_s = up_full[:, ts - (CONV_W - 1):][None]

    return (y_prompt, y_sample, new_gla_p, new_gla_s, new_k_p, new_k_s, new_v_p, new_v_s,
            new_conv_p, new_conv_s)
```

```python
import functools

import jax
import jax.numpy as jnp
from jax import lax
from jax.experimental import pallas as pl
from jax.experimental.pallas import tpu as pltpu

F32 = jnp.float32
BF16 = jnp.bfloat16

D_MODEL = 2048
HEAD_DIM = 64
GLA_HEADS = 10
GLA_DK = 64
GLA_DV = 128
GLA_RANK = 16
GLA_TAU = 16.0
GLA_CHUNK = 64
DIL_HEADS = 12
DIL_PAIRS = ((128, 1), (512, 4), (2048, 16))
DIL_WINDOW = 2048
ROPE_THETA = 10000.0
D_FF = 5504
PAST_LEN = 8192
CONV_W = 3
EPS = 1e-6

GLA_QK = GLA_HEADS * GLA_DK
GLA_VW = GLA_HEADS * GLA_DV
DIL_W = DIL_HEADS * HEAD_DIM

LANE = 128
SUBLANE = 8
VMEM_LIMIT = 56 * 1024 * 1024

Z_PAD = LANE
D_FF_P = 5632
FF_TN = 512
FF_NT = D_FF_P // FF_TN

NT_DIMS = (((1,), (1,)), ((), ()))


def _params(sem, vmem=VMEM_LIMIT, flags=None):
    return pltpu.CompilerParams(dimension_semantics=sem, vmem_limit_bytes=vmem, flags=flags)


def _rms_kernel(x_ref, g_ref, o_ref):
    x = x_ref[...]
    ms = jnp.mean(x * x, axis=-1, keepdims=True)
    o_ref[...] = (x * lax.rsqrt(ms + EPS) * g_ref[...]).astype(BF16)


def _rms(x, g, tm):
    m, d = x.shape
    return pl.pallas_call(
        _rms_kernel,
        grid=(m // tm,),
        in_specs=[pl.BlockSpec((tm, d), lambda i: (i, 0)), pl.BlockSpec((1, d), lambda i: (0, 0))],
        out_specs=pl.BlockSpec((tm, d), lambda i: (i, 0)),
        out_shape=jax.ShapeDtypeStruct((m, d), BF16),
        compiler_params=_params(("arbitrary",)),
        name="rms",
    )(x, g.reshape(1, d))


def _proj_kernel(x_ref, w_ref, o_ref, wbf_ref):
    @pl.when(pl.program_id(1) == 0)
    def _():
        wbf_ref[...] = w_ref[...].astype(BF16)

    o_ref[...] = lax.dot_general(x_ref[...], wbf_ref[...], NT_DIMS, preferred_element_type=F32)


def _proj(xn, wt, tm, tn, n=None):
    m, k = xn.shape
    n = wt.shape[0] if n is None else n
    return pl.pallas_call(
        _proj_kernel,
        grid=(n // tn, m // tm),
        in_specs=[pl.BlockSpec((tm, k), lambda j, i: (i, 0)), pl.BlockSpec((tn, k), lambda j, i: (j, 0))],
        out_specs=pl.BlockSpec((tm, tn), lambda j, i: (i, j)),
        out_shape=jax.ShapeDtypeStruct((m, n), F32),
        scratch_shapes=[pltpu.VMEM((tn, k), BF16)],
        compiler_params=_params(("arbitrary", "arbitrary")),
        name="proj",
    )(xn, wt)


CLASS_DILATIONS = tuple(r for _, r in DIL_PAIRS if r > 1)


def _rope_kernel(q_ref, k_ref, v_ref, cos_ref, sin_ref, *rest, by_class, tiles, first_kept):
    reps = DIL_W // LANE
    cos = jnp.concatenate([cos_ref[...]] * reps, axis=1)
    sin = jnp.concatenate([sin_ref[...]] * reps, axis=1)
    lane = lax.broadcasted_iota(jnp.int32, (1, DIL_W), 1)
    first_half = (lane % HEAD_DIM) < (HEAD_DIM // 2)

    def rot(x):
        partner = jnp.where(first_half,
                            pltpu.roll(x, DIL_W - HEAD_DIM // 2, axis=1),
                            pltpu.roll(x, HEAD_DIM // 2, axis=1))
        return x * cos + partner * sin

    q = rot(q_ref[...]) * (HEAD_DIM ** -0.5)
    k = rot(k_ref[...])
    if not by_class:
        qo_ref, ko_ref = rest
        qo_ref[...] = q
        ko_ref[...] = k
        return
    n_class = 3 * len(CLASS_DILATIONS)
    qb_ref, kb_ref, vb_ref = rest[:3]
    class_refs = rest[3:3 + n_class]
    kt_ref, vt_ref, sq_ref, sk_ref, sv_ref = rest[3 + n_class:]
    v = v_ref[...]
    qb_ref[...] = q.astype(BF16)
    kb_ref[...] = k.astype(BF16)
    vb_ref[...] = v.astype(BF16)

    @pl.when(pl.program_id(0) % tiles >= first_kept)
    def _():
        kt_ref[...] = k.T
        vt_ref[...] = v.T

    for src, val in ((sq_ref, q), (sk_ref, k), (sv_ref, v)):
        for j in range(reps):
            src[j] = val[:, j * LANE:(j + 1) * LANE]
    tm = q.shape[0]
    for di, r in enumerate(CLASS_DILATIONS):
        for src, dst in zip((sq_ref, sk_ref, sv_ref), class_refs[3 * di:3 * di + 3]):
            for c in range(r):
                rows = pl.ds(c, tm // r, stride=r)
                dst[c] = jnp.concatenate([src[j, rows, :] for j in range(reps)], axis=1).astype(BF16)


def _rope(proj, cos_t, sin_t, tm, n_seq=None):
    m = proj.shape[0]
    by_class = n_seq is not None
    blk = lambda c: pl.BlockSpec((tm, DIL_W), lambda i, c=c: (i, c))
    tab = pl.BlockSpec((tm, LANE), lambda i: (i, 0))
    out = pl.BlockSpec((tm, DIL_W), lambda i: (i, 0))
    tiles, first_kept, scratch = 1, 0, []
    if by_class:
        t_seq = m // n_seq
        tiles = t_seq // tm
        window = min(DIL_WINDOW, t_seq)
        first_kept = (t_seq - window) // tm
        out_specs = [out, out, out]
        out_shape = [jax.ShapeDtypeStruct((m, DIL_W), BF16)] * 3
        for r in CLASS_DILATIONS:
            spec = pl.BlockSpec((None, r, tm // r, DIL_W), lambda i: (i // tiles, 0, i % tiles, 0))
            out_specs += [spec] * 3
            out_shape += [jax.ShapeDtypeStruct((n_seq, r, t_seq // r, DIL_W), BF16)] * 3
        kept = pl.BlockSpec((None, DIL_W, tm), lambda i: (i // tiles, 0, jnp.maximum(i % tiles - first_kept, 0)))
        out_specs += [kept] * 2
        out_shape += [jax.ShapeDtypeStruct((n_seq, DIL_W, window), F32)] * 2
        scratch = [pltpu.VMEM((DIL_W // LANE, tm, LANE), F32)] * 3
    else:
        out_specs = [out, out]
        out_shape = [jax.ShapeDtypeStruct((m, DIL_W), F32)] * 2
    return pl.pallas_call(
        functools.partial(_rope_kernel, by_class=by_class, tiles=tiles, first_kept=first_kept),
        grid=(m // tm,),
        in_specs=[blk(0), blk(1), blk(2), tab, tab],
        out_specs=out_specs,
        out_shape=out_shape,
        scratch_shapes=scratch,
        compiler_params=_params(("arbitrary",)),
        name="rope",
    )(proj, proj, proj, cos_t, sin_t)


def _rope_tables(pos):
    half = HEAD_DIM // 2
    inv_freq = ROPE_THETA ** (-2.0 * jnp.arange(half, dtype=F32) / HEAD_DIM)
    ang = pos.astype(F32)[:, None] * inv_freq[None, :]
    cos = jnp.cos(ang)
    sin = jnp.sin(ang)
    cos_t = jnp.concatenate([cos, cos, cos, cos], axis=1)
    sin_t = jnp.concatenate([-sin, sin, -sin, sin], axis=1)
    return cos_t, sin_t


def _gla_kernel(q_ref, k_ref, v_ref, r_ref, z_ref, wg_ref, bg_ref, gn_ref, s0_ref,
                o_ref, sfin_ref, st_ref, *, chunk, n_chunks, t_valid):
    i = pl.program_id(1)

    @pl.when(i == 0)
    def _():
        for h in range(GLA_HEADS):
            st_ref[h] = s0_ref[h].T

    row = lax.broadcasted_iota(jnp.int32, (chunk, chunk), 0)
    col = lax.broadcasted_iota(jnp.int32, (chunk, chunk), 1)
    causal = row >= col
    tril = causal.astype(BF16)
    t_in = lax.broadcasted_iota(jnp.int32, (chunk, 1), 0)
    mm = functools.partial(jnp.dot, preferred_element_type=F32)

    def split(a):
        hi = a.astype(BF16)
        return hi, (a - hi.astype(F32)).astype(BF16)

    for c in range(n_chunks):
        sl = pl.ds(c * chunk, chunk)
        z_hi, z_lo = split(z_ref[sl, :])
        x = mm(z_hi, wg_ref[0]) + mm(z_hi, wg_ref[1]) + mm(z_lo, wg_ref[0]) + bg_ref[...]
        log_a = -(jnp.maximum(-x, 0.0) + jnp.log1p(jnp.exp(-jnp.abs(x)))) / GLA_TAU
        t_abs = (i * n_chunks + c) * chunk + t_in
        log_a = jnp.where(t_abs < t_valid, log_a, 0.0)
        a_hi, a_lo = split(log_a)
        cum = mm(tril, a_hi) + mm(tril, a_lo)
        last = cum[chunk - 1:chunk, :]
        q = q_ref[sl, :] * (GLA_DK ** -0.5)
        k = k_ref[sl, :]
        q_dec = q * jnp.exp(cum)
        k_inv = k * jnp.exp(-cum)
        k_end = k * jnp.exp(last - cum)
        a_end = jnp.exp(last)
        heads = range(GLA_HEADS)
        ks = [slice(h * GLA_DK, (h + 1) * GLA_DK) for h in heads]
        vs = [slice(h * GLA_DV, (h + 1) * GLA_DV) for h in heads]
        nt = functools.partial(lax.dot_general, dimension_numbers=NT_DIMS, preferred_element_type=F32)
        qh = [q_dec[:, ks[h]].astype(BF16) for h in heads]
        v_f = [v_ref[sl, vs[h]] for h in heads]
        vh = [v_f[h].astype(BF16) for h in heads]
        scores = [nt(qh[h], k_inv[:, ks[h]].astype(BF16)) for h in heads]
        scores = [jnp.where(causal, scores[h], 0.0).astype(BF16) for h in heads]
        st = [st_ref[h] for h in heads]
        o = [mm(scores[h], vh[h]) + nt(qh[h], st[h].astype(BF16)) for h in heads]
        for h in heads:
            st_ref[h] = st[h] * a_end[:, ks[h]] + mm(v_f[h].T.astype(BF16), k_end[:, ks[h]].astype(BF16))
        ms = [jnp.mean(o[h] * o[h], axis=-1, keepdims=True) for h in heads]
        for h in heads:
            on = o[h] * lax.rsqrt(ms[h] + EPS) * gn_ref[...]
            r = r_ref[sl, vs[h]]
            o_ref[sl, vs[h]] = (on * (r / (1.0 + jnp.exp(-r)))).astype(BF16)

    @pl.when(i == pl.num_programs(1) - 1)
    def _():
        for h in range(GLA_HEADS):
            sfin_ref[h] = st_ref[h].T


def _gla(src, zsrc, wg, bg, gn, s0, *, chunk, n_chunks, t_valid):
    b, t, _ = src.shape
    tb = chunk * n_chunks
    col = lambda w, c: pl.BlockSpec((None, tb, w), lambda bi, i, c=c: (bi, i, c))
    full = lambda shape: pl.BlockSpec(shape, lambda bi, i: (0,) * len(shape))
    state = pl.BlockSpec((None, GLA_HEADS, GLA_DK, GLA_DV), lambda bi, i: (bi, 0, 0, 0))
    kern = functools.partial(_gla_kernel, chunk=chunk, n_chunks=n_chunks, t_valid=t_valid)
    return pl.pallas_call(
        kern,
        grid=(b, t // tb),
        in_specs=[col(GLA_QK, 0), col(GLA_QK, 1), col(GLA_VW, 1), col(GLA_VW, 2), col(Z_PAD, 0),
                  full((2, Z_PAD, GLA_QK)), full((1, GLA_QK)), full((1, GLA_DV)), state],
        out_specs=[pl.BlockSpec((None, tb, GLA_VW), lambda bi, i: (bi, i, 0)), state],
        out_shape=[jax.ShapeDtypeStruct((b, t, GLA_VW), BF16),
                   jax.ShapeDtypeStruct((b, GLA_HEADS, GLA_DK, GLA_DV), F32)],
        scratch_shapes=[pltpu.VMEM((GLA_HEADS, GLA_DV, GLA_DK), F32)],
        compiler_params=_params(("arbitrary", "arbitrary")),
        name="gla",
    )(src, src, src, src, zsrc, wg, bg, gn, s0)


QB = 128


def _dil_branch_kernel(*refs, first, final, dilation):
    if first:
        q_ref, kp_ref, kc_ref, vp_ref, vc_ref = refs[:5]
        outs = refs[5:]
    else:
        q_ref, kp_ref, kc_ref, vp_ref, vc_ref, num_ref, m_ref, d_ref = refs[:8]
        outs = refs[8:]
    i = pl.program_id(1)
    rows = pl.ds(pl.program_id(2), QB, stride=dilation) if dilation > 1 else slice(None)
    k2 = jnp.concatenate([kp_ref[...], kc_ref[...]], axis=0)
    v2 = jnp.concatenate([vp_ref[...], vc_ref[...]], axis=0)
    q = q_ref[...]
    qi = QB + lax.broadcasted_iota(jnp.int32, (QB, 1), 0)
    kj = lax.broadcasted_iota(jnp.int32, (1, 2 * QB), 1)
    off = qi - kj
    valid = (off >= 0) & (off <= QB) & ((kj >= QB) | (i > 0))
    lane = lax.broadcasted_iota(jnp.int32, (1, LANE), 1)
    lo = lane < HEAD_DIM
    heads = range(DIL_HEADS)
    cs = [slice(p * LANE, (p + 1) * LANE) for p in range(DIL_HEADS // 2)]
    sel = [lo if h % 2 == 0 else jnp.logical_not(lo) for h in heads]
    qm = [jnp.where(sel[h], q[:, cs[h // 2]], jnp.zeros((QB, LANE), BF16)) for h in heads]
    s = [lax.dot_general(qm[h], k2[:, cs[h // 2]], NT_DIMS, preferred_element_type=F32) for h in heads]
    s = [jnp.where(valid, s[h], -jnp.inf) for h in heads]
    m_new = [jnp.max(s[h], axis=-1, keepdims=True) for h in heads]
    if not first:
        m_in = m_ref[rows, :]
        d_in = d_ref[rows, :]
        m_old = [jnp.sum(jnp.where(lane == h, m_in, 0.0), axis=-1, keepdims=True) for h in heads]
        d_old = [jnp.sum(jnp.where(lane == h, d_in, 0.0), axis=-1, keepdims=True) for h in heads]
        m_new = [jnp.maximum(m_new[h], m_old[h]) for h in heads]
    pr = [jnp.exp(s[h] - m_new[h]) for h in heads]
    d_new = [jnp.sum(pr[h], axis=-1, keepdims=True) for h in heads]
    if not first:
        alpha = [jnp.exp(m_old[h] - m_new[h]) for h in heads]
        d_new = [d_new[h] + d_old[h] * alpha[h] for h in heads]
    pv = [jnp.dot(pr[h].astype(BF16), v2[:, cs[h // 2]], preferred_element_type=F32) for h in heads]
    pairs = []
    for p in range(DIL_HEADS // 2):
        num = jnp.where(lo, pv[2 * p], pv[2 * p + 1])
        if not first:
            num = num + num_ref[p, rows, :] * jnp.where(lo, alpha[2 * p], alpha[2 * p + 1])
        if final:
            num = num / jnp.where(lo, d_new[2 * p], d_new[2 * p + 1])
        else:
            outs[0][p, rows, :] = num
        pairs.append(num)
    if final:
        outs[0][...] = jnp.concatenate(pairs, axis=1).astype(BF16)
    else:
        outs[1][rows, :] = functools.reduce(jnp.add, [jnp.where(lane == h, m_new[h], 0.0) for h in heads])
        outs[2][rows, :] = functools.reduce(jnp.add, [jnp.where(lane == h, d_new[h], 0.0) for h in heads])


def _dil_branch(q, k, v, state, *, first, final):
    b, r, ts, _ = q.shape
    t = r * ts
    nq = ts // QB
    cur = pl.BlockSpec((None, None, QB, DIL_W), lambda bi, i, c: (bi, c, i, 0))
    prev = pl.BlockSpec((None, None, QB, DIL_W), lambda bi, i, c: (bi, c, jnp.maximum(i - 1, 0), 0))
    nat = lambda w: pl.BlockSpec((None, r * QB, w), lambda bi, i, c: (bi, i, 0))
    nat_num = pl.BlockSpec((None, DIL_W // LANE, r * QB, LANE), lambda bi, i, c: (bi, 0, i, 0))
    args = [q, k, k, v, v]
    in_specs = [cur, prev, cur, prev, cur]
    if not first:
        args += list(state)
        in_specs += [nat_num, nat(LANE), nat(LANE)]
    if final:
        assert r == 1
        out_specs = [nat(DIL_W)]
        out_shape = [jax.ShapeDtypeStruct((b, t, DIL_W), BF16)]
    else:
        out_specs = [nat_num, nat(LANE), nat(LANE)]
        out_shape = [jax.ShapeDtypeStruct((b, DIL_W // LANE, t, LANE), F32),
                     jax.ShapeDtypeStruct((b, t, LANE), F32),
                     jax.ShapeDtypeStruct((b, t, LANE), F32)]
    return pl.pallas_call(
        functools.partial(_dil_branch_kernel, first=first, final=final, dilation=r),
        grid=(b, nq, r),
        in_specs=in_specs,
        out_specs=out_specs,
        out_shape=out_shape,
        compiler_params=_params(("arbitrary", "arbitrary", "arbitrary")),
        name=f"dil_branch_r{r}",
    )(*args)


SAMPLE_HB = 6
SAMPLE_ROWS = 16


def _dil_sample_kernel(q_ref, kn_ref, vn_ref, knt_ref, vnt_ref, kc_ref, vc_ref, o_ref, ko_ref, vo_ref,
                       *, t_new, hist):
    t_row = lax.broadcasted_iota(jnp.int32, (SAMPLE_ROWS, 1), 0)
    j = lax.broadcasted_iota(jnp.int32, (1, hist), 1)
    lane = lax.broadcasted_iota(jnp.int32, (1, LANE), 1)
    keep = lane < LANE - t_new

    def weight(off):
        w = jnp.zeros(off.shape, F32)
        for win, dil in DIL_PAIRS:
            w = w + jnp.where((off >= 0) & (off <= win) & (off % dil == 0), 1.0, 0.0)
        return w

    w_c = weight(hist + t_row - j)
    w_n = [weight(t_row - t) for t in range(t_new)]
    heads = range(SAMPLE_HB)
    hs = [slice(h * HEAD_DIM, (h + 1) * HEAD_DIM) for h in heads]
    qs = [q_ref[:, hs[h]] for h in heads]
    s_c = [jnp.dot(qs[h].astype(BF16), kc_ref[h].astype(BF16), preferred_element_type=F32) for h in heads]
    s_c = [jnp.where(w_c > 0, s, -jnp.inf) for s in s_c]
    s_n = [[jnp.where(w_n[t] > 0, jnp.sum(qs[h] * kn_ref[t:t + 1, hs[h]], axis=-1, keepdims=True), -jnp.inf)
            for t in range(t_new)] for h in heads]
    m = [functools.reduce(jnp.maximum, s_n[h], jnp.max(s_c[h], axis=-1, keepdims=True)) for h in heads]
    p_c = [w_c * jnp.exp(s_c[h] - m[h]) for h in heads]
    p_n = [[w_n[t] * jnp.exp(s_n[h][t] - m[h]) for t in range(t_new)] for h in heads]
    den = [functools.reduce(jnp.add, p_n[h], jnp.sum(p_c[h], axis=-1, keepdims=True)) for h in heads]
    num = [lax.dot_general(p_c[h].astype(BF16), vc_ref[h].astype(BF16), NT_DIMS, preferred_element_type=F32)
           for h in heads]
    num = [functools.reduce(jnp.add, [p_n[h][t] * vn_ref[t:t + 1, hs[h]] for t in range(t_new)], num[h])
           for h in heads]
    o_ref[...] = jnp.concatenate([num[h] / den[h] for h in heads], axis=1)

    for h in heads:
        for src, new_ref, dst in ((kc_ref, knt_ref, ko_ref), (vc_ref, vnt_ref, vo_ref)):
            nxt = pltpu.roll(src[h, :, 0:LANE], LANE - t_new, axis=1)
            for c in range(hist // LANE):
                cur_t = nxt
                if c + 1 < hist // LANE:
                    nxt = pltpu.roll(src[h, :, (c + 1) * LANE:(c + 2) * LANE], LANE - t_new, axis=1)
                else:
                    nxt = new_ref[h]
                dst[h, :, c * LANE:(c + 1) * LANE] = jnp.where(keep, cur_t, nxt)


def _dil_sample(q, kn, vn, knt, vnt, cache_kt, cache_vt, t_new):
    b, nh, _, hist = cache_kt.shape
    w = SAMPLE_HB * HEAD_DIM
    small = pl.BlockSpec((None, SAMPLE_ROWS, w), lambda bi, c: (bi, 0, c))
    newt = pl.BlockSpec((None, SAMPLE_HB, HEAD_DIM, LANE), lambda bi, c: (bi, c, 0, 0))
    big = pl.BlockSpec((None, SAMPLE_HB, HEAD_DIM, hist), lambda bi, c: (bi, c, 0, 0))
    return pl.pallas_call(
        functools.partial(_dil_sample_kernel, t_new=t_new, hist=hist),
        grid=(b, nh // SAMPLE_HB),
        in_specs=[small, small, small, newt, newt, big, big],
        out_specs=[small, big, big],
        out_shape=[jax.ShapeDtypeStruct((b, SAMPLE_ROWS, nh * HEAD_DIM), F32),
                   jax.ShapeDtypeStruct(cache_kt.shape, F32),
                   jax.ShapeDtypeStruct(cache_vt.shape, F32)],
        compiler_params=_params(("arbitrary", "arbitrary")),
        name="dil_sample",
    )(q, kn, vn, knt, vnt, cache_kt, cache_vt)


OUT_ROWS = 256


def _out_proj_kernel(oa_ref, ob_ref, w_ref, x_ref, g_ref, h_ref, xn_ref, wbf_ref):
    @pl.when(pl.program_id(0) == 0)
    def _():
        wbf_ref[...] = w_ref[...].astype(BF16)

    for r0 in range(0, x_ref.shape[0], OUT_ROWS):
        rows = slice(r0, min(r0 + OUT_ROWS, x_ref.shape[0]))
        y = (jnp.dot(oa_ref[rows, :], wbf_ref[:GLA_VW, :], preferred_element_type=F32)
             + jnp.dot(ob_ref[rows, :], wbf_ref[GLA_VW:, :], preferred_element_type=F32))
        h = x_ref[rows, :] + y
        h_ref[rows, :] = h
        ms = jnp.mean(h * h, axis=-1, keepdims=True)
        xn_ref[rows, :] = (h * lax.rsqrt(ms + EPS) * g_ref[...]).astype(BF16)


def _out_proj(oa, ob, w, x, g, tm):
    m, d = x.shape
    row = lambda width: pl.BlockSpec((tm, width), lambda i: (i, 0))
    g2 = g.reshape(1, d)
    return pl.pallas_call(
        _out_proj_kernel,
        grid=(m // tm,),
        in_specs=[row(GLA_VW), row(DIL_W),
                  pl.BlockSpec(w.shape, lambda i: (0, 0), pipeline_mode=pl.Buffered(1)),
                  row(d), pl.BlockSpec(g2.shape, lambda i: (0, 0))],
        out_specs=[row(d), row(d)],
        out_shape=[jax.ShapeDtypeStruct((m, d), F32), jax.ShapeDtypeStruct((m, d), BF16)],
        scratch_shapes=[pltpu.VMEM(w.shape, BF16)],
        compiler_params=_params(("arbitrary",)),
        name="out_proj",
    )(oa, ob, w, x, g2)


FF_ROWS = 256
FF_COLS = FF_TN
FF_SUB = FF_TN // LANE
FF_TILES = D_FF // LANE


def _conv_taps(cw, cb, up, prev1, prev2):
    return cb + cw[0:1] * prev2 + cw[1:2] * prev1 + cw[2:3] * up


def _swiglu(u, g):
    return (g / (1.0 + jnp.exp(-g)) * u).astype(BF16)


def _stage_up_weights(w_blocks, wbf_ref, j):
    for half in range(2):
        for q in range(FF_SUB):
            blk = w_blocks[half * FF_SUB + q][...]
            if (FF_NT - 1) * FF_SUB + q >= FF_TILES:
                blk = jnp.where(j * FF_SUB + q < FF_TILES, blk, 0.0)
            wbf_ref[half, :, q * LANE:(q + 1) * LANE] = blk.astype(BF16)


def _up_weight_specs(tile_of_step):
    specs = []
    for half in range(2):
        for q in range(FF_SUB):
            def index(*ids, half=half, q=q):
                blk = jnp.minimum(tile_of_step(*ids) * FF_SUB + q, FF_TILES - 1)
                return (0, half * FF_TILES + blk)
            specs.append(pl.BlockSpec((D_MODEL, LANE), index))
    return specs


def _ffn_up_seq_kernel(*refs, tiles_per_seq):
    x_ref = refs[0]
    w_blocks = refs[1:1 + 2 * FF_SUB]
    cwu_ref, cwg_ref, cbu_ref, cbg_ref, act_ref, tu_ref, tg_ref, wbf_ref, carry_ref = refs[1 + 2 * FF_SUB:]
    j = pl.program_id(0)
    i = pl.program_id(1)
    tm = x_ref.shape[0]

    @pl.when(i == 0)
    def _():
        _stage_up_weights(w_blocks, wbf_ref, j)

    @pl.when(i % tiles_per_seq == 0)
    def _():
        carry_ref[...] = jnp.zeros_like(carry_ref)

    for c0 in range(0, FF_TN, FF_COLS):
        cols = slice(c0, c0 + FF_COLS)
        carry = [carry_ref[0, :, cols], carry_ref[1, :, cols]]
        for r0 in range(0, tm, FF_ROWS):
            x = x_ref[r0:r0 + FF_ROWS, :]
            conv = []
            for idx, (cw_ref, cb_ref) in enumerate(((cwu_ref, cbu_ref), (cwg_ref, cbg_ref))):
                up = jnp.dot(x, wbf_ref[idx, :, cols], preferred_element_type=F32)
                ext = jnp.concatenate([carry[idx], up], axis=0)
                prev1 = pltpu.roll(ext, 1, axis=0)[SUBLANE:]
                prev2 = pltpu.roll(ext, 2, axis=0)[SUBLANE:]
                carry[idx] = up[FF_ROWS - SUBLANE:]
                conv.append(_conv_taps(cw_ref[:, cols], cb_ref[:, cols], up, prev1, prev2))
            act_ref[r0:r0 + FF_ROWS, cols] = _swiglu(*conv)
        for idx, tail_ref in enumerate((tu_ref, tg_ref)):
            carry_ref[idx, :, cols] = carry[idx]
            tail_ref[:, cols] = carry[idx]


def _ffn_up_seq(xn, w_up, conv_w, conv_b, tm, t_seq):
    m, d = xn.shape
    tiles_per_seq = t_seq // tm
    xs = pl.BlockSpec((tm, d), lambda j, i: (i, 0))
    cu = lambda r: pl.BlockSpec((r, FF_TN), lambda j, i: (0, j))
    cg = lambda r: pl.BlockSpec((r, FF_TN), lambda j, i: (0, FF_NT + j))
    act = pl.BlockSpec((tm, FF_TN), lambda j, i: (i, j))
    tail = pl.BlockSpec((None, SUBLANE, FF_TN), lambda j, i: (i // tiles_per_seq, 0, j))
    w_specs = _up_weight_specs(lambda j, i: j)
    return pl.pallas_call(
        functools.partial(_ffn_up_seq_kernel, tiles_per_seq=tiles_per_seq),
        grid=(FF_NT, m // tm),
        in_specs=[xs] + w_specs + [cu(CONV_W), cg(CONV_W), cu(1), cg(1)],
        out_specs=[act, tail, tail],
        out_shape=[jax.ShapeDtypeStruct((m, D_FF_P), BF16)]
        + [jax.ShapeDtypeStruct((m // t_seq, SUBLANE, D_FF_P), F32)] * 2,
        scratch_shapes=[pltpu.VMEM((2, d, FF_TN), BF16), pltpu.VMEM((2, SUBLANE, FF_TN), F32)],
        compiler_params=_params(("arbitrary", "arbitrary")),
        name="ffn_up",
    )(xn, *([w_up] * len(w_specs)), conv_w, conv_w, conv_b, conv_b)


def _ffn_up_hist_kernel(*refs, t_seq):
    x_ref = refs[0]
    w_blocks = refs[1:1 + 2 * FF_SUB]
    (cwu_ref, cwg_ref, cbu_ref, cbg_ref, h1u_ref, h1g_ref, h2u_ref, h2g_ref,
     act_ref, upu_ref, upg_ref, wbf_ref) = refs[1 + 2 * FF_SUB:]
    _stage_up_weights(w_blocks, wbf_ref, pl.program_id(0))
    x = x_ref[...]
    t_in = lax.broadcasted_iota(jnp.int32, (x.shape[0], 1), 0) % t_seq
    conv = []
    for idx, (cw_ref, cb_ref, h1_ref, h2_ref, up_ref) in enumerate((
            (cwu_ref, cbu_ref, h1u_ref, h2u_ref, upu_ref), (cwg_ref, cbg_ref, h1g_ref, h2g_ref, upg_ref))):
        up = jnp.dot(x, wbf_ref[idx], preferred_element_type=F32)
        prev1 = jnp.where(t_in >= 1, pltpu.roll(up, 1, axis=0), 0.0) + h1_ref[...]
        prev2 = jnp.where(t_in >= 2, pltpu.roll(up, 2, axis=0), 0.0) + h2_ref[...]
        up_ref[...] = up
        conv.append(_conv_taps(cw_ref[...], cb_ref[...], up, prev1, prev2))
    act_ref[...] = _swiglu(*conv)


def _ffn_up_hist(xn, w_up, conv_w, conv_b, h1, h2, t_seq):
    m, d = xn.shape
    xs = pl.BlockSpec((m, d), lambda j: (0, 0))
    u = lambda r: pl.BlockSpec((r, FF_TN), lambda j: (0, j))
    g = lambda r: pl.BlockSpec((r, FF_TN), lambda j: (0, FF_NT + j))
    w_specs = _up_weight_specs(lambda j: j)
    return pl.pallas_call(
        functools.partial(_ffn_up_hist_kernel, t_seq=t_seq),
        grid=(FF_NT,),
        in_specs=[xs] + w_specs + [u(CONV_W), g(CONV_W), u(1), g(1), u(m), g(m), u(m), g(m)],
        out_specs=[u(m), u(m), u(m)],
        out_shape=[jax.ShapeDtypeStruct((m, D_FF_P), BF16)] + [jax.ShapeDtypeStruct((m, D_FF_P), F32)] * 2,
        scratch_shapes=[pltpu.VMEM((2, d, FF_TN), BF16)],
        compiler_params=_params(("arbitrary",)),
        name="ffn_up_hist",
    )(xn, *([w_up] * len(w_specs)), conv_w, conv_w, conv_b, conv_b, h1, h1, h2, h2)


DOWN_ROWS = 256


def _ffn_down_kernel(a_ref, w_ref, h_ref, g_ref, y_ref):
    kdim = w_ref.shape[0]
    for r0 in range(0, a_ref.shape[0], DOWN_ROWS):
        rows = slice(r0, min(r0 + DOWN_ROWS, a_ref.shape[0]))
        h = h_ref[rows, :] + jnp.dot(a_ref[rows, :kdim], w_ref[...], preferred_element_type=F32)
        ms = jnp.mean(h * h, axis=-1, keepdims=True)
        y_ref[rows, :] = h * lax.rsqrt(ms + EPS) * g_ref[...]


def _ffn_down(act, w, h, g, tm):
    m, d = h.shape
    return pl.pallas_call(
        _ffn_down_kernel,
        grid=(m // tm,),
        in_specs=[pl.BlockSpec((tm, act.shape[1]), lambda i: (i, 0)),
                  pl.BlockSpec(w.shape, lambda i: (0, 0), pipeline_mode=pl.Buffered(1)),
                  pl.BlockSpec((tm, d), lambda i: (i, 0)),
                  pl.BlockSpec((1, d), lambda i: (0, 0))],
        out_specs=pl.BlockSpec((tm, d), lambda i: (i, 0)),
        out_shape=jax.ShapeDtypeStruct((m, d), F32),
        compiler_params=_params(("arbitrary",)),
        name="ffn_down",
    )(act, w, h, g.reshape(1, d))


def _pad_cols(a, width):
    return jnp.pad(a, ((0, 0), (0, width - a.shape[1])))


def _split_pad_ff(a):
    return jnp.concatenate([_pad_cols(a[:, :D_FF], D_FF_P), _pad_cols(a[:, D_FF:], D_FF_P)], axis=1)


def _unpad_ff(u, g):
    return jnp.concatenate([u[..., :D_FF], g[..., :D_FF]], axis=-1)


def kernel(x_prompt, x_sample, state_gla, cache_dil_k, cache_dil_v, state_ffn_conv, norm_mix, w_in, w_gate_up,
           b_gate, gla_norm, w_out, norm_ffn, w_ffn_up, ffn_conv_w, ffn_conv_b, w_ffn_down, norm_final):
    bp, tp, d = x_prompt.shape
    bs, ts, _ = x_sample.shape
    l = 0

    z0 = 2 * GLA_QK + 2 * GLA_VW
    w_in_t = w_in[l].T
    w_dil_t = w_in_t[z0 + GLA_RANK:]
    w_z_t = jnp.pad(w_in_t[z0:z0 + GLA_RANK], ((0, Z_PAD - GLA_RANK), (0, 0)))
    wg_f = jnp.pad(w_gate_up[l], ((0, Z_PAD - GLA_RANK), (0, 0)))
    wg_hi = wg_f.astype(BF16)
    wg_p = jnp.stack([wg_hi, (wg_f - wg_hi.astype(F32)).astype(BF16)])
    bg = b_gate[l].reshape(1, GLA_QK)
    gn = gla_norm[l].reshape(1, GLA_DV)
    cw_p = _split_pad_ff(ffn_conv_w[l])
    cb_p = _split_pad_ff(ffn_conv_b[l].reshape(1, -1))
    w_dn = w_ffn_down[l].astype(BF16)

    mp = bp * tp
    xp = x_prompt.reshape(mp, d)
    xn_mix_p = _rms(xp, norm_mix[l], 512)
    mix_p = _proj(xn_mix_p, w_in_t, 2048, 768, n=z0)
    dil_p = _proj(xn_mix_p, w_dil_t, 2048, 768)
    z_p = _proj(xn_mix_p, w_z_t, 2048, Z_PAD)
    cos_p, sin_p = _rope_tables(jnp.arange(tp))
    cos_p = jnp.tile(cos_p, (bp, 1))
    sin_p = jnp.tile(sin_p, (bp, 1))
    rope_out = _rope(dil_p, cos_p, sin_p, 512, n_seq=bp)
    by_class = {1: tuple(a.reshape(bp, 1, tp, DIL_W) for a in rope_out[:3])}
    for di, r in enumerate(CLASS_DILATIONS):
        by_class[r] = tuple(rope_out[3 + 3 * di:6 + 3 * di])
    k_kept_t, v_kept_t = rope_out[-2:]

    oa_p, gla_p = _gla(mix_p.reshape(bp, tp, z0), z_p.reshape(bp, tp, Z_PAD), wg_p, bg, gn,
                       jnp.zeros((bp, GLA_HEADS, GLA_DK, GLA_DV), F32),
                       chunk=GLA_CHUNK, n_chunks=4, t_valid=tp)

    state = _dil_branch(*by_class[16], None, first=True, final=False)
    state = _dil_branch(*by_class[4], state, first=False, final=False)
    (ob_p,) = _dil_branch(*by_class[1], state, first=False, final=True)

    h_p, xn_p = _out_proj(oa_p.reshape(mp, GLA_VW), ob_p.reshape(mp, DIL_W), w_out[l], xp, norm_ffn[l], 512)
    act_p, tail_u, tail_g = _ffn_up_seq(xn_p, w_ffn_up[l], cw_p, cb_p, 1024, tp)
    y_p = _ffn_down(act_p, w_dn, h_p, norm_final, 512)

    buf_p = min(DIL_WINDOW, tp)
    y_prompt = y_p.reshape(bp, tp, d)
    new_gla_p = gla_p[None]
    new_k_p = k_kept_t.reshape(bp, DIL_HEADS, HEAD_DIM, buf_p).transpose(0, 3, 1, 2)[None]
    new_v_p = v_kept_t.reshape(bp, DIL_HEADS, HEAD_DIM, buf_p).transpose(0, 3, 1, 2)[None]
    new_conv_p = _unpad_ff(tail_u[:, SUBLANE - (CONV_W - 1):], tail_g[:, SUBLANE - (CONV_W - 1):])[None]

    ms_ = bs * ts
    xs = x_sample.reshape(ms_, d)
    xn_mix_s = _rms(xs, norm_mix[l], ms_)
    mix_s = _proj(xn_mix_s, w_in_t, ms_, 768, n=z0)
    dil_s = _proj(xn_mix_s, w_dil_t, ms_, 768)
    z_s = _proj(xn_mix_s, w_z_t, ms_, Z_PAD)
    cos_s, sin_s = _rope_tables(PAST_LEN + jnp.arange(ts))
    cos_s = jnp.tile(cos_s, (bs, 1))
    sin_s = jnp.tile(sin_s, (bs, 1))
    q_rs, k_rs = _rope(dil_s, cos_s, sin_s, ms_)

    pad_chunk = lambda a: jnp.pad(a.reshape(bs, ts, -1), ((0, 0), (0, GLA_CHUNK - ts), (0, 0)))
    oa_s, gla_s = _gla(pad_chunk(mix_s), pad_chunk(z_s), wg_p, bg, gn, state_gla[l],
                       chunk=GLA_CHUNK, n_chunks=1, t_valid=ts)
    oa_s = oa_s[:, :ts].reshape(ms_, GLA_VW)

    v_s = dil_s[:, 2 * DIL_W:]
    pad_rows = lambda a: jnp.pad(a.reshape(bs, ts, DIL_W), ((0, 0), (0, SAMPLE_ROWS - ts), (0, 0)))
    new_cols = lambda a: jnp.pad(a.reshape(bs, ts, DIL_HEADS, HEAD_DIM).transpose(0, 2, 3, 1),
                                 ((0, 0), (0, 0), (0, 0), (LANE - ts, 0)))
    time_minor = lambda a: a.transpose(0, 2, 3, 1)
    ob_s, k_s_t, v_s_t = _dil_sample(pad_rows(q_rs), pad_rows(k_rs), pad_rows(v_s), new_cols(k_rs), new_cols(v_s),
                                     time_minor(cache_dil_k[l]), time_minor(cache_dil_v[l]), ts)
    ob_s = ob_s[:, :ts].astype(BF16)

    h_s, xn_s = _out_proj(oa_s, ob_s.reshape(ms_, DIL_W), w_out[l], xs, norm_ffn[l], ms_)
    conv_hist = _split_pad_ff(state_ffn_conv[l].reshape(bs * (CONV_W - 1), 2 * D_FF))
    conv_hist = conv_hist.reshape(bs, CONV_W - 1, 2 * D_FF_P)
    zero_row = jnp.zeros((bs, 1, 2 * D_FF_P), F32)
    h1 = jnp.concatenate([conv_hist[:, 1:2]] + [zero_row] * (ts - 1), axis=1).reshape(ms_, 2 * D_FF_P)
    h2 = jnp.concatenate([conv_hist[:, 0:1], conv_hist[:, 1:2]] + [zero_row] * (ts - 2),
                         axis=1).reshape(ms_, 2 * D_FF_P)
    act_s, up_u, up_g = _ffn_up_hist(xn_s, w_ffn_up[l], cw_p, cb_p, h1, h2, ts)
    y_s = _ffn_down(act_s, w_dn, h_s, norm_final, ms_)

    y_sample = y_s.reshape(bs, ts, d)
    new_gla_s = gla_s[None]
    new_k_s = k_s_t.transpose(0, 3, 1, 2)[None]
    new_v_s = v_s_t.transpose(0, 3, 1, 2)[None]
    up_full = _unpad_ff(up_u, up_g).reshape(bs, ts, 2 * D_FF)
    new_conv_s = up_full[:, ts - (CONV_W - 1):][None]

    return (y_prompt, y_sample, new_gla_p, new_gla_s, new_k_p, new_k_s, new_v_p, new_v_s,
            new_conv_p, new_conv_s)
```

```python
import functools

import jax
import jax.numpy as jnp
from jax import lax
from jax.experimental import pallas as pl
from jax.experimental.pallas import tpu as pltpu

F32 = jnp.float32
BF16 = jnp.bfloat16

D_MODEL = 2048
HEAD_DIM = 64
GLA_HEADS = 10
GLA_DK = 64
GLA_DV = 128
GLA_RANK = 16
GLA_TAU = 16.0
GLA_CHUNK = 64
DIL_HEADS = 12
DIL_PAIRS = ((128, 1), (512, 4), (2048, 16))
DIL_WINDOW = 2048
ROPE_THETA = 10000.0
D_FF = 5504
PAST_LEN = 8192
CONV_W = 3
EPS = 1e-6

GLA_QK = GLA_HEADS * GLA_DK
GLA_VW = GLA_HEADS * GLA_DV
DIL_W = DIL_HEADS * HEAD_DIM

LANE = 128
SUBLANE = 8
VMEM_LIMIT = 56 * 1024 * 1024

Z_PAD = LANE
D_FF_P = 5632
FF_TN = 512
FF_NT = D_FF_P // FF_TN

NT_DIMS = (((1,), (1,)), ((), ()))


def _params(sem, vmem=VMEM_LIMIT, flags=None):
    return pltpu.CompilerParams(dimension_semantics=sem, vmem_limit_bytes=vmem, flags=flags)


def _rms_kernel(x_ref, g_ref, o_ref):
    x = x_ref[...]
    ms = jnp.mean(x * x, axis=-1, keepdims=True)
    o_ref[...] = (x * lax.rsqrt(ms + EPS) * g_ref[...]).astype(BF16)


def _rms(x, g, tm):
    m, d = x.shape
    return pl.pallas_call(
        _rms_kernel,
        grid=(m // tm,),
        in_specs=[pl.BlockSpec((tm, d), lambda i: (i, 0)), pl.BlockSpec((1, d), lambda i: (0, 0))],
        out_specs=pl.BlockSpec((tm, d), lambda i: (i, 0)),
        out_shape=jax.ShapeDtypeStruct((m, d), BF16),
        compiler_params=_params(("arbitrary",)),
        name="rms",
    )(x, g.reshape(1, d))


def _proj_kernel(x_ref, w_ref, o_ref, wbf_ref):
    @pl.when(pl.program_id(1) == 0)
    def _():
        wbf_ref[...] = w_ref[...].astype(BF16)

    o_ref[...] = lax.dot_general(x_ref[...], wbf_ref[...], NT_DIMS, preferred_element_type=F32)


def _proj(xn, wt, tm, tn, n=None):
    m, k = xn.shape
    n = wt.shape[0] if n is None else n
    return pl.pallas_call(
        _proj_kernel,
        grid=(n // tn, m // tm),
        in_specs=[pl.BlockSpec((tm, k), lambda j, i: (i, 0)), pl.BlockSpec((tn, k), lambda j, i: (j, 0))],
        out_specs=pl.BlockSpec((tm, tn), lambda j, i: (i, j)),
        out_shape=jax.ShapeDtypeStruct((m, n), F32),
        scratch_shapes=[pltpu.VMEM((tn, k), BF16)],
        compiler_params=_params(("arbitrary", "arbitrary")),
        name="proj",
    )(xn, wt)


CLASS_DILATIONS = tuple(r for _, r in DIL_PAIRS if r > 1)


def _rope_kernel(q_ref, k_ref, v_ref, cos_ref, sin_ref, *rest, by_class, tiles, first_kept):
    reps = DIL_W // LANE
    cos = jnp.concatenate([cos_ref[...]] * reps, axis=1)
    sin = jnp.concatenate([sin_ref[...]] * reps, axis=1)
    lane = lax.broadcasted_iota(jnp.int32, (1, DIL_W), 1)
    first_half = (lane % HEAD_DIM) < (HEAD_DIM // 2)

    def rot(x):
        partner = jnp.where(first_half,
                            pltpu.roll(x, DIL_W - HEAD_DIM // 2, axis=1),
                            pltpu.roll(x, HEAD_DIM // 2, axis=1))
        return x * cos + partner * sin

    q = rot(q_ref[...]) * (HEAD_DIM ** -0.5)
    k = rot(k_ref[...])
    if not by_class:
        qo_ref, ko_ref = rest
        qo_ref[...] = q
        ko_ref[...] = k
        return
    n_class = 3 * len(CLASS_DILATIONS)
    qb_ref, kb_ref, vb_ref = rest[:3]
    class_refs = rest[3:3 + n_class]
    kt_ref, vt_ref, sq_ref, sk_ref, sv_ref = rest[3 + n_class:]
    v = v_ref[...]
    qb_ref[...] = q.astype(BF16)
    kb_ref[...] = k.astype(BF16)
    vb_ref[...] = v.astype(BF16)

    @pl.when(pl.program_id(0) % tiles >= first_kept)
    def _():
        kt_ref[...] = k.T
        vt_ref[...] = v.T

    for src, val in ((sq_ref, q), (sk_ref, k), (sv_ref, v)):
        for j in range(reps):
            src[j] = val[:, j * LANE:(j + 1) * LANE]
    tm = q.shape[0]
    for di, r in enumerate(CLASS_DILATIONS):
        for src, dst in zip((sq_ref, sk_ref, sv_ref), class_refs[3 * di:3 * di + 3]):
            for c in range(r):
                rows = pl.ds(c, tm // r, stride=r)
                dst[c] = jnp.concatenate([src[j, rows, :] for j in range(reps)], axis=1).astype(BF16)


def _rope(proj, cos_t, sin_t, tm, n_seq=None):
    m = proj.shape[0]
    by_class = n_seq is not None
    blk = lambda c: pl.BlockSpec((tm, DIL_W), lambda i, c=c: (i, c))
    tab = pl.BlockSpec((tm, LANE), lambda i: (i, 0))
    out = pl.BlockSpec((tm, DIL_W), lambda i: (i, 0))
    tiles, first_kept, scratch = 1, 0, []
    if by_class:
        t_seq = m // n_seq
        tiles = t_seq // tm
        window = min(DIL_WINDOW, t_seq)
        first_kept = (t_seq - window) // tm
        out_specs = [out, out, out]
        out_shape = [jax.ShapeDtypeStruct((m, DIL_W), BF16)] * 3
        for r in CLASS_DILATIONS:
            spec = pl.BlockSpec((None, r, tm // r, DIL_W), lambda i: (i // tiles, 0, i % tiles, 0))
            out_specs += [spec] * 3
            out_shape += [jax.ShapeDtypeStruct((n_seq, r, t_seq // r, DIL_W), BF16)] * 3
        kept = pl.BlockSpec((None, DIL_W, tm), lambda i: (i // tiles, 0, jnp.maximum(i % tiles - first_kept, 0)))
        out_specs += [kept] * 2
        out_shape += [jax.ShapeDtypeStruct((n_seq, DIL_W, window), F32)] * 2
        scratch = [pltpu.VMEM((DIL_W // LANE, tm, LANE), F32)] * 3
    else:
        out_specs = [out, out]
        out_shape = [jax.ShapeDtypeStruct((m, DIL_W), F32)] * 2
    return pl.pallas_call(
        functools.partial(_rope_kernel, by_class=by_class, tiles=tiles, first_kept=first_kept),
        grid=(m // tm,),
        in_specs=[blk(0), blk(1), blk(2), tab, tab],
        out_specs=out_specs,
        out_shape=out_shape,
        scratch_shapes=scratch,
        compiler_params=_params(("arbitrary",)),
        name="rope",
    )(proj, proj, proj, cos_t, sin_t)


def _rope_tables(pos):
    half = HEAD_DIM // 2
    inv_freq = ROPE_THETA ** (-2.0 * jnp.arange(half, dtype=F32) / HEAD_DIM)
    ang = pos.astype(F32)[:, None] * inv_freq[None, :]
    cos = jnp.cos(ang)
    sin = jnp.sin(ang)
    cos_t = jnp.concatenate([cos, cos, cos, cos], axis=1)
    sin_t = jnp.concatenate([-sin, sin, -sin, sin], axis=1)
    return cos_t, sin_t


def _gla_kernel(q_ref, k_ref, v_ref, r_ref, z_ref, wg_ref, bg_ref, gn_ref, s0_ref,
                o_ref, sfin_ref, st_ref, *, chunk, n_chunks, t_valid):
    i = pl.program_id(1)

    @pl.when(i == 0)
    def _():
        for h in range(GLA_HEADS):
            st_ref[h] = s0_ref[h].T

    row = lax.broadcasted_iota(jnp.int32, (chunk, chunk), 0)
    col = lax.broadcasted_iota(jnp.int32, (chunk, chunk), 1)
    causal = row >= col
    tril = causal.astype(BF16)
    t_in = lax.broadcasted_iota(jnp.int32, (chunk, 1), 0)
    mm = functools.partial(jnp.dot, preferred_element_type=F32)

    def split(a):
        hi = a.astype(BF16)
        return hi, (a - hi.astype(F32)).astype(BF16)

    for c in range(n_chunks):
        sl = pl.ds(c * chunk, chunk)
        z_hi, z_lo = split(z_ref[sl, :])
        x = mm(z_hi, wg_ref[0]) + mm(z_hi, wg_ref[1]) + mm(z_lo, wg_ref[0]) + bg_ref[...]
        log_a = -(jnp.maximum(-x, 0.0) + jnp.log1p(jnp.exp(-jnp.abs(x)))) / GLA_TAU
        t_abs = (i * n_chunks + c) * chunk + t_in
        log_a = jnp.where(t_abs < t_valid, log_a, 0.0)
        a_hi, a_lo = split(log_a)
        cum = mm(tril, a_hi) + mm(tril, a_lo)
        last = cum[chunk - 1:chunk, :]
        q = q_ref[sl, :] * (GLA_DK ** -0.5)
        k = k_ref[sl, :]
        q_dec = q * jnp.exp(cum)
        k_inv = k * jnp.exp(-cum)
        k_end = k * jnp.exp(last - cum)
        a_end = jnp.exp(last)
        heads = range(GLA_HEADS)
        ks = [slice(h * GLA_DK, (h + 1) * GLA_DK) for h in heads]
        vs = [slice(h * GLA_DV, (h + 1) * GLA_DV) for h in heads]
        nt = functools.partial(lax.dot_general, dimension_numbers=NT_DIMS, preferred_element_type=F32)
        qh = [q_dec[:, ks[h]].astype(BF16) for h in heads]
        v_f = [v_ref[sl, vs[h]] for h in heads]
        vh = [v_f[h].astype(BF16) for h in heads]
        scores = [nt(qh[h], k_inv[:, ks[h]].astype(BF16)) for h in heads]
        scores = [jnp.where(causal, scores[h], 0.0).astype(BF16) for h in heads]
        st = [st_ref[h] for h in heads]
        o = [mm(scores[h], vh[h]) + nt(qh[h], st[h].astype(BF16)) for h in heads]
        for h in heads:
            st_ref[h] = st[h] * a_end[:, ks[h]] + mm(v_f[h].T.astype(BF16), k_end[:, ks[h]].astype(BF16))
        ms = [jnp.mean(o[h] * o[h], axis=-1, keepdims=True) for h in heads]
        for h in heads:
            on = o[h] * lax.rsqrt(ms[h] + EPS) * gn_ref[...]
            r = r_ref[sl, vs[h]]
            o_ref[sl, vs[h]] = (on * (r / (1.0 + jnp.exp(-r)))).astype(BF16)

    @pl.when(i == pl.num_programs(1) - 1)
    def _():
        for h in range(GLA_HEADS):
            sfin_ref[h] = st_ref[h].T


def _gla(src, zsrc, wg, bg, gn, s0, *, chunk, n_chunks, t_valid):
    b, t, _ = src.shape
    tb = chunk * n_chunks
    col = lambda w, c: pl.BlockSpec((None, tb, w), lambda bi, i, c=c: (bi, i, c))
    full = lambda shape: pl.BlockSpec(shape, lambda bi, i: (0,) * len(shape))
    state = pl.BlockSpec((None, GLA_HEADS, GLA_DK, GLA_DV), lambda bi, i: (bi, 0, 0, 0))
    kern = functools.partial(_gla_kernel, chunk=chunk, n_chunks=n_chunks, t_valid=t_valid)
    return pl.pallas_call(
        kern,
        grid=(b, t // tb),
        in_specs=[col(GLA_QK, 0), col(GLA_QK, 1), col(GLA_VW, 1), col(GLA_VW, 2), col(Z_PAD, 0),
                  full((2, Z_PAD, GLA_QK)), full((1, GLA_QK)), full((1, GLA_DV)), state],
        out_specs=[pl.BlockSpec((None, tb, GLA_VW), lambda bi, i: (bi, i, 0)), state],
        out_shape=[jax.ShapeDtypeStruct((b, t, GLA_VW), BF16),
                   jax.ShapeDtypeStruct((b, GLA_HEADS, GLA_DK, GLA_DV), F32)],
        scratch_shapes=[pltpu.VMEM((GLA_HEADS, GLA_DV, GLA_DK), F32)],
        compiler_params=_params(("arbitrary", "arbitrary")),
        name="gla",
    )(src, src, src, src, zsrc, wg, bg, gn, s0)


QB = 128
DIL_SUB = 2


def _dil_branch_kernel(*refs, first, final, dilation):
    if first:
        q_ref, kp_ref, kc_ref, vp_ref, vc_ref = refs[:5]
        outs = refs[5:]
    else:
        q_ref, kp_ref, kc_ref, vp_ref, vc_ref, num_ref, m_ref, d_ref = refs[:8]
        outs = refs[8:]
    i = pl.program_id(1)
    c = pl.program_id(2)
    qi = QB + lax.broadcasted_iota(jnp.int32, (QB, 1), 0)
    kj = lax.broadcasted_iota(jnp.int32, (1, 2 * QB), 1)
    off = qi - kj
    in_window = (off >= 0) & (off <= QB)
    lane = lax.broadcasted_iota(jnp.int32, (1, LANE), 1)
    lo = lane < HEAD_DIM
    heads = range(DIL_HEADS)
    cs = [slice(p * LANE, (p + 1) * LANE) for p in range(DIL_HEADS // 2)]
    sel = [lo if h % 2 == 0 else jnp.logical_not(lo) for h in heads]
    for u in range(DIL_SUB):
        blk = slice(u * QB, (u + 1) * QB)
        rows = pl.ds(dilation * u * QB + c, QB, stride=dilation) if dilation > 1 else blk
        if u == 0:
            k_before, v_before = kp_ref[...], vp_ref[...]
            valid = in_window & ((kj >= QB) | (i > 0))
        else:
            before = slice((u - 1) * QB, u * QB)
            k_before, v_before = kc_ref[before, :], vc_ref[before, :]
            valid = in_window
        k2 = jnp.concatenate([k_before, kc_ref[blk, :]], axis=0)
        v2 = jnp.concatenate([v_before, vc_ref[blk, :]], axis=0)
        q = q_ref[blk, :]
        qm = [jnp.where(sel[h], q[:, cs[h // 2]], jnp.zeros((QB, LANE), BF16)) for h in heads]
        s = [lax.dot_general(qm[h], k2[:, cs[h // 2]], NT_DIMS, preferred_element_type=F32) for h in heads]
        s = [jnp.where(valid, s[h], -jnp.inf) for h in heads]
        m_new = [jnp.max(s[h], axis=-1, keepdims=True) for h in heads]
        if not first:
            m_in = m_ref[rows, :]
            d_in = d_ref[rows, :]
            m_old = [jnp.sum(jnp.where(lane == h, m_in, 0.0), axis=-1, keepdims=True) for h in heads]
            d_old = [jnp.sum(jnp.where(lane == h, d_in, 0.0), axis=-1, keepdims=True) for h in heads]
            m_new = [jnp.maximum(m_new[h], m_old[h]) for h in heads]
        pr = [jnp.exp(s[h] - m_new[h]) for h in heads]
        d_new = [jnp.sum(pr[h], axis=-1, keepdims=True) for h in heads]
        if not first:
            alpha = [jnp.exp(m_old[h] - m_new[h]) for h in heads]
            d_new = [d_new[h] + d_old[h] * alpha[h] for h in heads]
        pv = [jnp.dot(pr[h].astype(BF16), v2[:, cs[h // 2]], preferred_element_type=F32) for h in heads]
        pairs = []
        for p in range(DIL_HEADS // 2):
            num = jnp.where(lo, pv[2 * p], pv[2 * p + 1])
            if not first:
                num = num + num_ref[p, rows, :] * jnp.where(lo, alpha[2 * p], alpha[2 * p + 1])
            if final:
                num = num / jnp.where(lo, d_new[2 * p], d_new[2 * p + 1])
            else:
                outs[0][p, rows, :] = num
            pairs.append(num)
        if final:
            outs[0][rows, :] = jnp.concatenate(pairs, axis=1).astype(BF16)
        else:
            outs[1][rows, :] = functools.reduce(jnp.add, [jnp.where(lane == h, m_new[h], 0.0) for h in heads])
            outs[2][rows, :] = functools.reduce(jnp.add, [jnp.where(lane == h, d_new[h], 0.0) for h in heads])


def _dil_branch(q, k, v, state, *, first, final):
    b, r, ts, _ = q.shape
    t = r * ts
    step = DIL_SUB * QB
    nq = ts // step
    cur = pl.BlockSpec((None, None, step, DIL_W), lambda bi, i, c: (bi, c, i, 0))
    prev = pl.BlockSpec((None, None, QB, DIL_W), lambda bi, i, c: (bi, c, jnp.maximum(DIL_SUB * i - 1, 0), 0))
    nat = lambda w: pl.BlockSpec((None, r * step, w), lambda bi, i, c: (bi, i, 0))
    nat_num = pl.BlockSpec((None, DIL_W // LANE, r * step, LANE), lambda bi, i, c: (bi, 0, i, 0))
    args = [q, k, k, v, v]
    in_specs = [cur, prev, cur, prev, cur]
    if not first:
        args += list(state)
        in_specs += [nat_num, nat(LANE), nat(LANE)]
    if final:
        assert r == 1
        out_specs = [nat(DIL_W)]
        out_shape = [jax.ShapeDtypeStruct((b, t, DIL_W), BF16)]
    else:
        out_specs = [nat_num, nat(LANE), nat(LANE)]
        out_shape = [jax.ShapeDtypeStruct((b, DIL_W // LANE, t, LANE), F32),
                     jax.ShapeDtypeStruct((b, t, LANE), F32),
                     jax.ShapeDtypeStruct((b, t, LANE), F32)]
    return pl.pallas_call(
        functools.partial(_dil_branch_kernel, first=first, final=final, dilation=r),
        grid=(b, nq, r),
        in_specs=in_specs,
        out_specs=out_specs,
        out_shape=out_shape,
        compiler_params=_params(("arbitrary", "arbitrary", "arbitrary")),
        name=f"dil_branch_r{r}",
    )(*args)


SAMPLE_HB = 6
SAMPLE_ROWS = 16


def _dil_sample_kernel(q_ref, kn_ref, vn_ref, knt_ref, vnt_ref, kc_ref, vc_ref, o_ref, ko_ref, vo_ref,
                       *, t_new, hist):
    t_row = lax.broadcasted_iota(jnp.int32, (SAMPLE_ROWS, 1), 0)
    j = lax.broadcasted_iota(jnp.int32, (1, hist), 1)
    lane = lax.broadcasted_iota(jnp.int32, (1, LANE), 1)
    keep = lane < LANE - t_new

    def weight(off):
        w = jnp.zeros(off.shape, F32)
        for win, dil in DIL_PAIRS:
            w = w + jnp.where((off >= 0) & (off <= win) & (off % dil == 0), 1.0, 0.0)
        return w

    w_c = weight(hist + t_row - j)
    w_n = [weight(t_row - t) for t in range(t_new)]
    heads = range(SAMPLE_HB)
    hs = [slice(h * HEAD_DIM, (h + 1) * HEAD_DIM) for h in heads]
    qs = [q_ref[:, hs[h]] for h in heads]
    s_c = [jnp.dot(qs[h].astype(BF16), kc_ref[h].astype(BF16), preferred_element_type=F32) for h in heads]
    s_c = [jnp.where(w_c > 0, s, -jnp.inf) for s in s_c]
    s_n = [[jnp.where(w_n[t] > 0, jnp.sum(qs[h] * kn_ref[t:t + 1, hs[h]], axis=-1, keepdims=True), -jnp.inf)
            for t in range(t_new)] for h in heads]
    m = [functools.reduce(jnp.maximum, s_n[h], jnp.max(s_c[h], axis=-1, keepdims=True)) for h in heads]
    p_c = [w_c * jnp.exp(s_c[h] - m[h]) for h in heads]
    p_n = [[w_n[t] * jnp.exp(s_n[h][t] - m[h]) for t in range(t_new)] for h in heads]
    den = [functools.reduce(jnp.add, p_n[h], jnp.sum(p_c[h], axis=-1, keepdims=True)) for h in heads]
    num = [lax.dot_general(p_c[h].astype(BF16), vc_ref[h].astype(BF16), NT_DIMS, preferred_element_type=F32)
           for h in heads]
    num = [functools.reduce(jnp.add, [p_n[h][t] * vn_ref[t:t + 1, hs[h]] for t in range(t_new)], num[h])
           for h in heads]
    o_ref[...] = jnp.concatenate([num[h] / den[h] for h in heads], axis=1)

    for h in heads:
        for src, new_ref, dst in ((kc_ref, knt_ref, ko_ref), (vc_ref, vnt_ref, vo_ref)):
            nxt = pltpu.roll(src[h, :, 0:LANE], LANE - t_new, axis=1)
            for c in range(hist // LANE):
                cur_t = nxt
                if c + 1 < hist // LANE:
                    nxt = pltpu.roll(src[h, :, (c + 1) * LANE:(c + 2) * LANE], LANE - t_new, axis=1)
                else:
                    nxt = new_ref[h]
                dst[h, :, c * LANE:(c + 1) * LANE] = jnp.where(keep, cur_t, nxt)


def _dil_sample(q, kn, vn, knt, vnt, cache_kt, cache_vt, t_new):
    b, nh, _, hist = cache_kt.shape
    w = SAMPLE_HB * HEAD_DIM
    small = pl.BlockSpec((None, SAMPLE_ROWS, w), lambda bi, c: (bi, 0, c))
    newt = pl.BlockSpec((None, SAMPLE_HB, HEAD_DIM, LANE), lambda bi, c: (bi, c, 0, 0))
    big = pl.BlockSpec((None, SAMPLE_HB, HEAD_DIM, hist), lambda bi, c: (bi, c, 0, 0))
    return pl.pallas_call(
        functools.partial(_dil_sample_kernel, t_new=t_new, hist=hist),
        grid=(b, nh // SAMPLE_HB),
        in_specs=[small, small, small, newt, newt, big, big],
        out_specs=[small, big, big],
        out_shape=[jax.ShapeDtypeStruct((b, SAMPLE_ROWS, nh * HEAD_DIM), F32),
                   jax.ShapeDtypeStruct(cache_kt.shape, F32),
                   jax.ShapeDtypeStruct(cache_vt.shape, F32)],
        compiler_params=_params(("arbitrary", "arbitrary")),
        name="dil_sample",
    )(q, kn, vn, knt, vnt, cache_kt, cache_vt)


OUT_ROWS = 256

STRIDE_ROWS = 256
STRIDE_GROUPS = STRIDE_ROWS // SUBLANE


def _to_strided(val, stage_ref):
    for j in range(val.shape[1] // LANE):
        for s_ in range(SUBLANE):
            stage_ref[j, pl.ds(s_, STRIDE_GROUPS, stride=SUBLANE), :] = (
                val[s_ * STRIDE_GROUPS:(s_ + 1) * STRIDE_GROUPS, j * LANE:(j + 1) * LANE])
    return jnp.concatenate([stage_ref[j] for j in range(val.shape[1] // LANE)], axis=1)


def _from_strided(val, stage_ref):
    for j in range(val.shape[1] // LANE):
        stage_ref[j] = val[:, j * LANE:(j + 1) * LANE]
    return jnp.concatenate(
        [jnp.concatenate([stage_ref[j, pl.ds(s_, STRIDE_GROUPS, stride=SUBLANE), :]
                          for j in range(val.shape[1] // LANE)], axis=1) for s_ in range(SUBLANE)], axis=0)


def _out_proj_kernel(oa_ref, ob_ref, w_ref, x_ref, g_ref, h_ref, xn_ref, wbf_ref, *stage, strided):
    @pl.when(pl.program_id(0) == 0)
    def _():
        wbf_ref[...] = w_ref[...].astype(BF16)

    for r0 in range(0, x_ref.shape[0], OUT_ROWS):
        rows = slice(r0, min(r0 + OUT_ROWS, x_ref.shape[0]))
        y = (jnp.dot(oa_ref[rows, :], wbf_ref[:GLA_VW, :], preferred_element_type=F32)
             + jnp.dot(ob_ref[rows, :], wbf_ref[GLA_VW:, :], preferred_element_type=F32))
        h = x_ref[rows, :] + y
        h_ref[rows, :] = h
        ms = jnp.mean(h * h, axis=-1, keepdims=True)
        xn = h * lax.rsqrt(ms + EPS) * g_ref[...]
        if strided:
            xn = _to_strided(xn, stage[0])
        xn_ref[rows, :] = xn.astype(BF16)


def _out_proj(oa, ob, w, x, g, tm, strided=False):
    m, d = x.shape
    assert not strided or OUT_ROWS == STRIDE_ROWS
    row = lambda width: pl.BlockSpec((tm, width), lambda i: (i, 0))
    g2 = g.reshape(1, d)
    scratch = [pltpu.VMEM(w.shape, BF16)]
    if strided:
        scratch.append(pltpu.VMEM((d // LANE, STRIDE_ROWS, LANE), F32))
    return pl.pallas_call(
        functools.partial(_out_proj_kernel, strided=strided),
        grid=(m // tm,),
        in_specs=[row(GLA_VW), row(DIL_W),
                  pl.BlockSpec(w.shape, lambda i: (0, 0), pipeline_mode=pl.Buffered(1)),
                  row(d), pl.BlockSpec(g2.shape, lambda i: (0, 0))],
        out_specs=[row(d), row(d)],
        out_shape=[jax.ShapeDtypeStruct((m, d), F32), jax.ShapeDtypeStruct((m, d), BF16)],
        scratch_shapes=scratch,
        compiler_params=_params(("arbitrary",)),
        name="out_proj",
    )(oa, ob, w, x, g2)


FF_ROWS = STRIDE_ROWS
FF_TAIL = 2 * SUBLANE
FF_SUB = FF_TN // LANE
FF_TILES = D_FF // LANE


def _conv_taps(cw, cb, up, prev1, prev2):
    return cb + cw[0:1] * prev2 + cw[1:2] * prev1 + cw[2:3] * up


def _swiglu(u, g):
    return (g / (1.0 + jnp.exp(-g)) * u).astype(BF16)


def _stage_up_weights(w_blocks, wbf_ref, j):
    for half in range(2):
        for q in range(FF_SUB):
            blk = w_blocks[half * FF_SUB + q][...]
            if (FF_NT - 1) * FF_SUB + q >= FF_TILES:
                blk = jnp.where(j * FF_SUB + q < FF_TILES, blk, 0.0)
            wbf_ref[half, :, q * LANE:(q + 1) * LANE] = blk.astype(BF16)


def _up_weight_specs(tile_of_step):
    specs = []
    for half in range(2):
        for q in range(FF_SUB):
            def index(*ids, half=half, q=q):
                blk = jnp.minimum(tile_of_step(*ids) * FF_SUB + q, FF_TILES - 1)
                return (0, half * FF_TILES + blk)
            specs.append(pl.BlockSpec((D_MODEL, LANE), index))
    return specs


def _ffn_up_seq_kernel(*refs, tiles_per_seq):
    x_ref = refs[0]
    w_blocks = refs[1:1 + 2 * FF_SUB]
    cwu_ref, cwg_ref, cbu_ref, cbg_ref, act_ref, tu_ref, tg_ref, wbf_ref, carry_ref = refs[1 + 2 * FF_SUB:]
    j = pl.program_id(0)
    i = pl.program_id(1)
    tm = x_ref.shape[0]

    @pl.when(i == 0)
    def _():
        _stage_up_weights(w_blocks, wbf_ref, j)

    @pl.when(i % tiles_per_seq == 0)
    def _():
        carry_ref[...] = jnp.zeros_like(carry_ref)

    carry = [carry_ref[0], carry_ref[1]]
    shift = lambda before, cur: pltpu.roll(jnp.concatenate([before, cur], axis=0), 1, axis=0)[SUBLANE:]
    for r0 in range(0, tm, FF_ROWS):
        x = x_ref[r0:r0 + FF_ROWS, :]
        conv = []
        for idx, (cw_ref, cb_ref) in enumerate(((cwu_ref, cbu_ref), (cwg_ref, cbg_ref))):
            up = jnp.dot(x, wbf_ref[idx], preferred_element_type=F32)
            tail = up[FF_ROWS - FF_TAIL:]
            p_a = shift(carry[idx][:SUBLANE], tail[:SUBLANE])
            p_b = shift(carry[idx][SUBLANE:], tail[SUBLANE:])
            prev1 = jnp.concatenate([p_b, up[:FF_ROWS - SUBLANE]], axis=0)
            prev2 = jnp.concatenate([p_a, p_b, up[:FF_ROWS - 2 * SUBLANE]], axis=0)
            carry[idx] = tail
            conv.append(_conv_taps(cw_ref[...], cb_ref[...], up, prev1, prev2))
        act_ref[r0:r0 + FF_ROWS, :] = _swiglu(*conv)
    for idx, tail_ref in enumerate((tu_ref, tg_ref)):
        carry_ref[idx] = carry[idx]
        tail_ref[...] = carry[idx]


def _ffn_up_seq(xn, w_up, conv_w, conv_b, tm, t_seq):
    m, d = xn.shape
    tiles_per_seq = t_seq // tm
    xs = pl.BlockSpec((tm, d), lambda j, i: (i, 0))
    cu = lambda r: pl.BlockSpec((r, FF_TN), lambda j, i: (0, j))
    cg = lambda r: pl.BlockSpec((r, FF_TN), lambda j, i: (0, FF_NT + j))
    act = pl.BlockSpec((tm, FF_TN), lambda j, i: (i, j))
    tail = pl.BlockSpec((None, FF_TAIL, FF_TN), lambda j, i: (i // tiles_per_seq, 0, j))
    w_specs = _up_weight_specs(lambda j, i: j)
    return pl.pallas_call(
        functools.partial(_ffn_up_seq_kernel, tiles_per_seq=tiles_per_seq),
        grid=(FF_NT, m // tm),
        in_specs=[xs] + w_specs + [cu(CONV_W), cg(CONV_W), cu(1), cg(1)],
        out_specs=[act, tail, tail],
        out_shape=[jax.ShapeDtypeStruct((m, D_FF_P), BF16)]
        + [jax.ShapeDtypeStruct((m // t_seq, FF_TAIL, D_FF_P), F32)] * 2,
        scratch_shapes=[pltpu.VMEM((2, d, FF_TN), BF16), pltpu.VMEM((2, FF_TAIL, FF_TN), F32)],
        compiler_params=_params(("arbitrary", "arbitrary")),
        name="ffn_up",
    )(xn, *([w_up] * len(w_specs)), conv_w, conv_w, conv_b, conv_b)


def _ffn_up_hist_kernel(*refs, t_seq):
    x_ref = refs[0]
    w_blocks = refs[1:1 + 2 * FF_SUB]
    (cwu_ref, cwg_ref, cbu_ref, cbg_ref, h1u_ref, h1g_ref, h2u_ref, h2g_ref,
     act_ref, upu_ref, upg_ref, wbf_ref) = refs[1 + 2 * FF_SUB:]
    _stage_up_weights(w_blocks, wbf_ref, pl.program_id(0))
    x = x_ref[...]
    t_in = lax.broadcasted_iota(jnp.int32, (x.shape[0], 1), 0) % t_seq
    conv = []
    for idx, (cw_ref, cb_ref, h1_ref, h2_ref, up_ref) in enumerate((
            (cwu_ref, cbu_ref, h1u_ref, h2u_ref, upu_ref), (cwg_ref, cbg_ref, h1g_ref, h2g_ref, upg_ref))):
        up = jnp.dot(x, wbf_ref[idx], preferred_element_type=F32)
        prev1 = jnp.where(t_in >= 1, pltpu.roll(up, 1, axis=0), 0.0) + h1_ref[...]
        prev2 = jnp.where(t_in >= 2, pltpu.roll(up, 2, axis=0), 0.0) + h2_ref[...]
        up_ref[...] = up
        conv.append(_conv_taps(cw_ref[...], cb_ref[...], up, prev1, prev2))
    act_ref[...] = _swiglu(*conv)


def _ffn_up_hist(xn, w_up, conv_w, conv_b, h1, h2, t_seq):
    m, d = xn.shape
    xs = pl.BlockSpec((m, d), lambda j: (0, 0))
    u = lambda r: pl.BlockSpec((r, FF_TN), lambda j: (0, j))
    g = lambda r: pl.BlockSpec((r, FF_TN), lambda j: (0, FF_NT + j))
    w_specs = _up_weight_specs(lambda j: j)
    return pl.pallas_call(
        functools.partial(_ffn_up_hist_kernel, t_seq=t_seq),
        grid=(FF_NT,),
        in_specs=[xs] + w_specs + [u(CONV_W), g(CONV_W), u(1), g(1), u(m), g(m), u(m), g(m)],
        out_specs=[u(m), u(m), u(m)],
        out_shape=[jax.ShapeDtypeStruct((m, D_FF_P), BF16)] + [jax.ShapeDtypeStruct((m, D_FF_P), F32)] * 2,
        scratch_shapes=[pltpu.VMEM((2, d, FF_TN), BF16)],
        compiler_params=_params(("arbitrary",)),
        name="ffn_up_hist",
    )(xn, *([w_up] * len(w_specs)), conv_w, conv_w, conv_b, conv_b, h1, h1, h2, h2)


DOWN_ROWS = 256


def _ffn_down_kernel(a_ref, w_ref, h_ref, g_ref, y_ref, *stage, strided):
    kdim = w_ref.shape[0]
    for r0 in range(0, a_ref.shape[0], DOWN_ROWS):
        rows = slice(r0, min(r0 + DOWN_ROWS, a_ref.shape[0]))
        f = jnp.dot(a_ref[rows, :kdim], w_ref[...], preferred_element_type=F32)
        if strided:
            f = _from_strided(f, stage[0])
        h = h_ref[rows, :] + f
        ms = jnp.mean(h * h, axis=-1, keepdims=True)
        y_ref[rows, :] = h * lax.rsqrt(ms + EPS) * g_ref[...]


def _ffn_down(act, w, h, g, tm, strided=False):
    m, d = h.shape
    assert not strided or DOWN_ROWS == STRIDE_ROWS
    scratch = [pltpu.VMEM((d // LANE, STRIDE_ROWS, LANE), F32)] if strided else []
    return pl.pallas_call(
        functools.partial(_ffn_down_kernel, strided=strided),
        grid=(m // tm,),
        in_specs=[pl.BlockSpec((tm, act.shape[1]), lambda i: (i, 0)),
                  pl.BlockSpec(w.shape, lambda i: (0, 0), pipeline_mode=pl.Buffered(1)),
                  pl.BlockSpec((tm, d), lambda i: (i, 0)),
                  pl.BlockSpec((1, d), lambda i: (0, 0))],
        out_specs=pl.BlockSpec((tm, d), lambda i: (i, 0)),
        out_shape=jax.ShapeDtypeStruct((m, d), F32),
        scratch_shapes=scratch,
        compiler_params=_params(("arbitrary",)),
        name="ffn_down",
    )(act, w, h, g.reshape(1, d))


def _pad_cols(a, width):
    return jnp.pad(a, ((0, 0), (0, width - a.shape[1])))


def _split_pad_ff(a):
    return jnp.concatenate([_pad_cols(a[:, :D_FF], D_FF_P), _pad_cols(a[:, D_FF:], D_FF_P)], axis=1)


def _unpad_ff(u, g):
    return jnp.concatenate([u[..., :D_FF], g[..., :D_FF]], axis=-1)


def kernel(x_prompt, x_sample, state_gla, cache_dil_k, cache_dil_v, state_ffn_conv, norm_mix, w_in, w_gate_up,
           b_gate, gla_norm, w_out, norm_ffn, w_ffn_up, ffn_conv_w, ffn_conv_b, w_ffn_down, norm_final):
    bp, tp, d = x_prompt.shape
    bs, ts, _ = x_sample.shape
    l = 0

    z0 = 2 * GLA_QK + 2 * GLA_VW
    w_in_t = w_in[l].T
    w_dil_t = w_in_t[z0 + GLA_RANK:]
    w_z_t = jnp.pad(w_in_t[z0:z0 + GLA_RANK], ((0, Z_PAD - GLA_RANK), (0, 0)))
    wg_f = jnp.pad(w_gate_up[l], ((0, Z_PAD - GLA_RANK), (0, 0)))
    wg_hi = wg_f.astype(BF16)
    wg_p = jnp.stack([wg_hi, (wg_f - wg_hi.astype(F32)).astype(BF16)])
    bg = b_gate[l].reshape(1, GLA_QK)
    gn = gla_norm[l].reshape(1, GLA_DV)
    cw_p = _split_pad_ff(ffn_conv_w[l])
    cb_p = _split_pad_ff(ffn_conv_b[l].reshape(1, -1))
    w_dn = w_ffn_down[l].astype(BF16)

    mp = bp * tp
    xp = x_prompt.reshape(mp, d)
    xn_mix_p = _rms(xp, norm_mix[l], 512)
    mix_p = _proj(xn_mix_p, w_in_t, 2048, 768, n=z0)
    dil_p = _proj(xn_mix_p, w_dil_t, 2048, 768)
    z_p = _proj(xn_mix_p, w_z_t, 2048, Z_PAD)
    cos_p, sin_p = _rope_tables(jnp.arange(tp))
    cos_p = jnp.tile(cos_p, (bp, 1))
    sin_p = jnp.tile(sin_p, (bp, 1))
    rope_out = _rope(dil_p, cos_p, sin_p, 512, n_seq=bp)
    by_class = {1: tuple(a.reshape(bp, 1, tp, DIL_W) for a in rope_out[:3])}
    for di, r in enumerate(CLASS_DILATIONS):
        by_class[r] = tuple(rope_out[3 + 3 * di:6 + 3 * di])
    k_kept_t, v_kept_t = rope_out[-2:]

    oa_p, gla_p = _gla(mix_p.reshape(bp, tp, z0), z_p.reshape(bp, tp, Z_PAD), wg_p, bg, gn,
                       jnp.zeros((bp, GLA_HEADS, GLA_DK, GLA_DV), F32),
                       chunk=GLA_CHUNK, n_chunks=4, t_valid=tp)

    state = _dil_branch(*by_class[16], None, first=True, final=False)
    state = _dil_branch(*by_class[4], state, first=False, final=False)
    (ob_p,) = _dil_branch(*by_class[1], state, first=False, final=True)

    h_p, xn_p = _out_proj(oa_p.reshape(mp, GLA_VW), ob_p.reshape(mp, DIL_W), w_out[l], xp, norm_ffn[l], 512, strided=True)
    act_p, tail_u, tail_g = _ffn_up_seq(xn_p, w_ffn_up[l], cw_p, cb_p, 1024, tp)
    y_p = _ffn_down(act_p, w_dn, h_p, norm_final, 512, strided=True)

    buf_p = min(DIL_WINDOW, tp)
    y_prompt = y_p.reshape(bp, tp, d)
    new_gla_p = gla_p[None]
    new_k_p = k_kept_t.reshape(bp, DIL_HEADS, HEAD_DIM, buf_p).transpose(0, 3, 1, 2)[None]
    new_v_p = v_kept_t.reshape(bp, DIL_HEADS, HEAD_DIM, buf_p).transpose(0, 3, 1, 2)[None]
    last_two = lambda a: a[:, SUBLANE - 1::SUBLANE]
    new_conv_p = _unpad_ff(last_two(tail_u), last_two(tail_g))[None]

    ms_ = bs * ts
    xs = x_sample.reshape(ms_, d)
    xn_mix_s = _rms(xs, norm_mix[l], ms_)
    mix_s = _proj(xn_mix_s, w_in_t, ms_, 768, n=z0)
    dil_s = _proj(xn_mix_s, w_dil_t, ms_, 768)
    z_s = _proj(xn_mix_s, w_z_t, ms_, Z_PAD)
    cos_s, sin_s = _rope_tables(PAST_LEN + jnp.arange(ts))
    cos_s = jnp.tile(cos_s, (bs, 1))
    sin_s = jnp.tile(sin_s, (bs, 1))
    q_rs, k_rs = _rope(dil_s, cos_s, sin_s, ms_)

    pad_chunk = lambda a: jnp.pad(a.reshape(bs, ts, -1), ((0, 0), (0, GLA_CHUNK - ts), (0, 0)))
    oa_s, gla_s = _gla(pad_chunk(mix_s), pad_chunk(z_s), wg_p, bg, gn, state_gla[l],
                       chunk=GLA_CHUNK, n_chunks=1, t_valid=ts)
    oa_s = oa_s[:, :ts].reshape(ms_, GLA_VW)

    v_s = dil_s[:, 2 * DIL_W:]
    pad_rows = lambda a: jnp.pad(a.reshape(bs, ts, DIL_W), ((0, 0), (0, SAMPLE_ROWS - ts), (0, 0)))
    new_cols = lambda a: jnp.pad(a.reshape(bs, ts, DIL_HEADS, HEAD_DIM).transpose(0, 2, 3, 1),
                                 ((0, 0), (0, 0), (0, 0), (LANE - ts, 0)))
    time_minor = lambda a: a.transpose(0, 2, 3, 1)
    ob_s, k_s_t, v_s_t = _dil_sample(pad_rows(q_rs), pad_rows(k_rs), pad_rows(v_s), new_cols(k_rs), new_cols(v_s),
                                     time_minor(cache_dil_k[l]), time_minor(cache_dil_v[l]), ts)
    ob_s = ob_s[:, :ts].astype(BF16)

    h_s, xn_s = _out_proj(oa_s, ob_s.reshape(ms_, DIL_W), w_out[l], xs, norm_ffn[l], ms_)
    conv_hist = _split_pad_ff(state_ffn_conv[l].reshape(bs * (CONV_W - 1), 2 * D_FF))
    conv_hist = conv_hist.reshape(bs, CONV_W - 1, 2 * D_FF_P)
    zero_row = jnp.zeros((bs, 1, 2 * D_FF_P), F32)
    h1 = jnp.concatenate([conv_hist[:, 1:2]] + [zero_row] * (ts - 1), axis=1).reshape(ms_, 2 * D_FF_P)
    h2 = jnp.concatenate([conv_hist[:, 0:1], conv_hist[:, 1:2]] + [zero_row] * (ts - 2),
                         axis=1).reshape(ms_, 2 * D_FF_P)
    act_s, up_u, up_g = _ffn_up_hist(xn_s, w_ffn_up[l], cw_p, cb_p, h1, h2, ts)
    y_s = _ffn_down(act_s, w_dn, h_s, norm_final, ms_)

    y_sample = y_s.reshape(bs, ts, d)
    new_gla_s = gla_s[None]
    new_k_s = k_s_t.transpose(0, 3, 1, 2)[None]
    new_v_s = v_s_t.transpose(0, 3, 1, 2)[None]
    up_full = _unpad_ff(up_u, up_g).reshape(bs, ts, 2 * D_FF)
    new_conv_s = up_full[:, ts - (CONV_W - 1):][None]

    return (y_prompt, y_sample, new_gla_p, new_gla_s, new_k_p, new_k_s, new_v_p, new_v_s,
            new_conv_p, new_conv_s)
```

```python
import functools

import jax
import jax.numpy as jnp
from jax import lax
from jax.experimental import pallas as pl
from jax.experimental.pallas import tpu as pltpu

F32 = jnp.float32
BF16 = jnp.bfloat16

D_MODEL = 2048
HEAD_DIM = 64
GLA_HEADS = 10
GLA_DK = 64
GLA_DV = 128
GLA_RANK = 16
GLA_TAU = 16.0
GLA_ROWS = 128
GLA_ROWS_SHORT = 16
DIL_HEADS = 12
DIL_PAIRS = ((128, 1), (512, 4), (2048, 16))
DIL_WINDOW = 2048
ROPE_THETA = 10000.0
D_FF = 5504
PAST_LEN = 8192
CONV_W = 3
EPS = 1e-6

GLA_QK = GLA_HEADS * GLA_DK
GLA_VW = GLA_HEADS * GLA_DV
DIL_W = DIL_HEADS * HEAD_DIM

LANE = 128
SUBLANE = 8
VMEM_LIMIT = 56 * 1024 * 1024

Z_PAD = LANE
D_FF_P = 5632
FF_TN = 512
FF_NT = D_FF_P // FF_TN

NT_DIMS = (((1,), (1,)), ((), ()))


def _params(sem, vmem=VMEM_LIMIT, flags=None):
    return pltpu.CompilerParams(dimension_semantics=sem, vmem_limit_bytes=vmem, flags=flags)


def _rms_kernel(x_ref, g_ref, o_ref):
    x = x_ref[...]
    ms = jnp.mean(x * x, axis=-1, keepdims=True)
    o_ref[...] = (x * lax.rsqrt(ms + EPS) * g_ref[...]).astype(BF16)


def _rms(x, g, tm):
    m, d = x.shape
    return pl.pallas_call(
        _rms_kernel,
        grid=(m // tm,),
        in_specs=[pl.BlockSpec((tm, d), lambda i: (i, 0)), pl.BlockSpec((1, d), lambda i: (0, 0))],
        out_specs=pl.BlockSpec((tm, d), lambda i: (i, 0)),
        out_shape=jax.ShapeDtypeStruct((m, d), BF16),
        compiler_params=_params(("arbitrary",)),
        name="rms",
    )(x, g.reshape(1, d))


def _proj_kernel(x_ref, w_ref, o_ref, wbf_ref):
    @pl.when(pl.program_id(1) == 0)
    def _():
        wbf_ref[...] = w_ref[...].astype(BF16)

    o_ref[...] = lax.dot_general(x_ref[...], wbf_ref[...], NT_DIMS, preferred_element_type=F32)


def _proj(xn, wt, tm, tn, n=None):
    m, k = xn.shape
    n = wt.shape[0] if n is None else n
    return pl.pallas_call(
        _proj_kernel,
        grid=(n // tn, m // tm),
        in_specs=[pl.BlockSpec((tm, k), lambda j, i: (i, 0)), pl.BlockSpec((tn, k), lambda j, i: (j, 0))],
        out_specs=pl.BlockSpec((tm, tn), lambda j, i: (i, j)),
        out_shape=jax.ShapeDtypeStruct((m, n), F32),
        scratch_shapes=[pltpu.VMEM((tn, k), BF16)],
        compiler_params=_params(("arbitrary", "arbitrary")),
        name="proj",
    )(xn, wt)


CLASS_DILATIONS = tuple(r for _, r in DIL_PAIRS if r > 1)


def _rope_kernel(q_ref, k_ref, v_ref, cos_ref, sin_ref, *rest, by_class, tiles, first_kept):
    reps = DIL_W // LANE
    cos = jnp.concatenate([cos_ref[...]] * reps, axis=1)
    sin = jnp.concatenate([sin_ref[...]] * reps, axis=1)
    lane = lax.broadcasted_iota(jnp.int32, (1, DIL_W), 1)
    first_half = (lane % HEAD_DIM) < (HEAD_DIM // 2)

    def rot(x):
        partner = jnp.where(first_half,
                            pltpu.roll(x, DIL_W - HEAD_DIM // 2, axis=1),
                            pltpu.roll(x, HEAD_DIM // 2, axis=1))
        return x * cos + partner * sin

    q = rot(q_ref[...]) * (HEAD_DIM ** -0.5)
    k = rot(k_ref[...])
    if not by_class:
        qo_ref, ko_ref = rest
        qo_ref[...] = q
        ko_ref[...] = k
        return
    n_class = 3 * len(CLASS_DILATIONS)
    qb_ref, kb_ref, vb_ref = rest[:3]
    class_refs = rest[3:3 + n_class]
    kt_ref, vt_ref, sq_ref, sk_ref, sv_ref = rest[3 + n_class:]
    v = v_ref[...]
    qb_ref[...] = q.astype(BF16)
    kb_ref[...] = k.astype(BF16)
    vb_ref[...] = v.astype(BF16)

    @pl.when(pl.program_id(0) % tiles >= first_kept)
    def _():
        kt_ref[...] = k.T
        vt_ref[...] = v.T

    for src, val in ((sq_ref, q), (sk_ref, k), (sv_ref, v)):
        for j in range(reps):
            src[j] = val[:, j * LANE:(j + 1) * LANE]
    tm = q.shape[0]
    for di, r in enumerate(CLASS_DILATIONS):
        for src, dst in zip((sq_ref, sk_ref, sv_ref), class_refs[3 * di:3 * di + 3]):
            for c in range(r):
                rows = pl.ds(c, tm // r, stride=r)
                dst[c] = jnp.concatenate([src[j, rows, :] for j in range(reps)], axis=1).astype(BF16)


def _rope(proj, cos_t, sin_t, tm, n_seq=None):
    m = proj.shape[0]
    by_class = n_seq is not None
    blk = lambda c: pl.BlockSpec((tm, DIL_W), lambda i, c=c: (i, c))
    tab = pl.BlockSpec((tm, LANE), lambda i: (i, 0))
    out = pl.BlockSpec((tm, DIL_W), lambda i: (i, 0))
    tiles, first_kept, scratch = 1, 0, []
    if by_class:
        t_seq = m // n_seq
        tiles = t_seq // tm
        window = min(DIL_WINDOW, t_seq)
        first_kept = (t_seq - window) // tm
        out_specs = [out, out, out]
        out_shape = [jax.ShapeDtypeStruct((m, DIL_W), BF16)] * 3
        for r in CLASS_DILATIONS:
            spec = pl.BlockSpec((None, r, tm // r, DIL_W), lambda i: (i // tiles, 0, i % tiles, 0))
            out_specs += [spec] * 3
            out_shape += [jax.ShapeDtypeStruct((n_seq, r, t_seq // r, DIL_W), BF16)] * 3
        kept = pl.BlockSpec((None, DIL_W, tm), lambda i: (i // tiles, 0, jnp.maximum(i % tiles - first_kept, 0)))
        out_specs += [kept] * 2
        out_shape += [jax.ShapeDtypeStruct((n_seq, DIL_W, window), F32)] * 2
        scratch = [pltpu.VMEM((DIL_W // LANE, tm, LANE), F32)] * 3
    else:
        out_specs = [out, out]
        out_shape = [jax.ShapeDtypeStruct((m, DIL_W), F32)] * 2
    return pl.pallas_call(
        functools.partial(_rope_kernel, by_class=by_class, tiles=tiles, first_kept=first_kept),
        grid=(m // tm,),
        in_specs=[blk(0), blk(1), blk(2), tab, tab],
        out_specs=out_specs,
        out_shape=out_shape,
        scratch_shapes=scratch,
        compiler_params=_params(("arbitrary",)),
        name="rope",
    )(proj, proj, proj, cos_t, sin_t)


def _rope_tables(pos):
    half = HEAD_DIM // 2
    inv_freq = ROPE_THETA ** (-2.0 * jnp.arange(half, dtype=F32) / HEAD_DIM)
    ang = pos.astype(F32)[:, None] * inv_freq[None, :]
    cos = jnp.cos(ang)
    sin = jnp.sin(ang)
    cos_t = jnp.concatenate([cos, cos, cos, cos], axis=1)
    sin_t = jnp.concatenate([-sin, sin, -sin, sin], axis=1)
    return cos_t, sin_t


def _gla_kernel(q_ref, k_ref, v_ref, r_ref, z_ref, wg_ref, bg_ref, gn_ref, s0_ref,
                o_ref, sfin_ref, st_ref, *, chunk, n_chunks, t_valid):
    i = pl.program_id(1)

    @pl.when(i == 0)
    def _():
        for h in range(GLA_HEADS):
            st_ref[h] = s0_ref[h].T

    row = lax.broadcasted_iota(jnp.int32, (chunk, chunk), 0)
    col = lax.broadcasted_iota(jnp.int32, (chunk, chunk), 1)
    tril = (row >= col).astype(BF16)
    t_in = lax.broadcasted_iota(jnp.int32, (chunk, 1), 0)
    levels = []
    s_ = chunk
    while s_ >= 2:
        half = s_ // 2
        pivot = (row // s_) * s_ + (half - 1)
        levels.append(dict(
            upto=(col <= pivot).astype(BF16),
            pair=(row // s_ == col // s_) & (row % s_ >= half) & (col % s_ < half),
            second=(t_in % s_) >= half))
        s_ = half
    diag = row == col
    mm = functools.partial(jnp.dot, preferred_element_type=F32)

    def split(a):
        hi = a.astype(BF16)
        return hi, (a - hi.astype(F32)).astype(BF16)

    for c in range(n_chunks):
        sl = pl.ds(c * chunk, chunk)
        z_hi, z_lo = split(z_ref[sl, :])
        x = mm(z_hi, wg_ref[0]) + mm(z_hi, wg_ref[1]) + mm(z_lo, wg_ref[0]) + bg_ref[...]
        log_a = -(jnp.maximum(-x, 0.0) + jnp.log1p(jnp.exp(-jnp.abs(x)))) / GLA_TAU
        t_abs = (i * n_chunks + c) * chunk + t_in
        log_a = jnp.where(t_abs < t_valid, log_a, 0.0)
        a_hi, a_lo = split(log_a)
        cum = mm(tril, a_hi) + mm(tril, a_lo)
        last = cum[chunk - 1:chunk, :]
        q = q_ref[sl, :] * (GLA_DK ** -0.5)
        k = k_ref[sl, :]
        q_dec = q * jnp.exp(cum)
        k_end = k * jnp.exp(last - cum)
        a_end = jnp.exp(last)
        heads = range(GLA_HEADS)
        ks = [slice(h * GLA_DK, (h + 1) * GLA_DK) for h in heads]
        vs = [slice(h * GLA_DV, (h + 1) * GLA_DV) for h in heads]
        nt = functools.partial(lax.dot_general, dimension_numbers=NT_DIMS, preferred_element_type=F32)
        qh = [q_dec[:, ks[h]].astype(BF16) for h in heads]
        v_f = [v_ref[sl, vs[h]] for h in heads]
        vh = [v_f[h].astype(BF16) for h in heads]
        pivot_cum = [mm(lv["upto"], a_hi) + mm(lv["upto"], a_lo) for lv in levels]
        q_lv = [(q * jnp.exp(jnp.where(lv["second"], cum - pc, 0.0))).astype(BF16)
                for lv, pc in zip(levels, pivot_cum)]
        k_lv = [(k * jnp.exp(jnp.where(lv["second"], 0.0, pc - cum))).astype(BF16)
                for lv, pc in zip(levels, pivot_cum)]
        q_bf, k_bf = q.astype(BF16), k.astype(BF16)
        parts = [[nt(ql[:, ks[h]], kl[:, ks[h]]) for h in heads] for ql, kl in zip(q_lv, k_lv)]
        scores = [jnp.where(diag, nt(q_bf[:, ks[h]], k_bf[:, ks[h]]), 0.0) for h in heads]
        for lv, part in zip(levels, parts):
            scores = [jnp.where(lv["pair"], part[h], scores[h]) for h in heads]
        scores = [scores[h].astype(BF16) for h in heads]
        st = [st_ref[h] for h in heads]
        o = [mm(scores[h], vh[h]) + nt(qh[h], st[h].astype(BF16)) for h in heads]
        for h in heads:
            st_ref[h] = st[h] * a_end[:, ks[h]] + mm(v_f[h].T.astype(BF16), k_end[:, ks[h]].astype(BF16))
        ms = [jnp.mean(o[h] * o[h], axis=-1, keepdims=True) for h in heads]
        for h in heads:
            on = o[h] * lax.rsqrt(ms[h] + EPS) * gn_ref[...]
            r = r_ref[sl, vs[h]]
            o_ref[sl, vs[h]] = (on * (r / (1.0 + jnp.exp(-r)))).astype(BF16)

    @pl.when(i == pl.num_programs(1) - 1)
    def _():
        for h in range(GLA_HEADS):
            sfin_ref[h] = st_ref[h].T


def _gla(src, zsrc, wg, bg, gn, s0, *, chunk, n_chunks, t_valid):
    b, t, _ = src.shape
    tb = chunk * n_chunks
    col = lambda w, c: pl.BlockSpec((None, tb, w), lambda bi, i, c=c: (bi, i, c))
    full = lambda shape: pl.BlockSpec(shape, lambda bi, i: (0,) * len(shape))
    state = pl.BlockSpec((None, GLA_HEADS, GLA_DK, GLA_DV), lambda bi, i: (bi, 0, 0, 0))
    kern = functools.partial(_gla_kernel, chunk=chunk, n_chunks=n_chunks, t_valid=t_valid)
    return pl.pallas_call(
        kern,
        grid=(b, t // tb),
        in_specs=[col(GLA_QK, 0), col(GLA_QK, 1), col(GLA_VW, 1), col(GLA_VW, 2), col(Z_PAD, 0),
                  full((2, Z_PAD, GLA_QK)), full((1, GLA_QK)), full((1, GLA_DV)), state],
        out_specs=[pl.BlockSpec((None, tb, GLA_VW), lambda bi, i: (bi, i, 0)), state],
        out_shape=[jax.ShapeDtypeStruct((b, t, GLA_VW), BF16),
                   jax.ShapeDtypeStruct((b, GLA_HEADS, GLA_DK, GLA_DV), F32)],
        scratch_shapes=[pltpu.VMEM((GLA_HEADS, GLA_DV, GLA_DK), F32)],
        compiler_params=_params(("arbitrary", "arbitrary")),
        name="gla",
    )(src, src, src, src, zsrc, wg, bg, gn, s0)


QB = 128
DIL_SUB = 2


def _dil_branch_kernel(*refs, first, final, dilation):
    if first:
        q_ref, kp_ref, kc_ref, vp_ref, vc_ref = refs[:5]
        outs = refs[5:]
    else:
        q_ref, kp_ref, kc_ref, vp_ref, vc_ref, num_ref, m_ref, d_ref = refs[:8]
        outs = refs[8:]
    i = pl.program_id(1)
    c = pl.program_id(2)
    qi = QB + lax.broadcasted_iota(jnp.int32, (QB, 1), 0)
    kj = lax.broadcasted_iota(jnp.int32, (1, 2 * QB), 1)
    off = qi - kj
    in_window = (off >= 0) & (off <= QB)
    lane = lax.broadcasted_iota(jnp.int32, (1, LANE), 1)
    lo = lane < HEAD_DIM
    heads = range(DIL_HEADS)
    cs = [slice(p * LANE, (p + 1) * LANE) for p in range(DIL_HEADS // 2)]
    sel = [lo if h % 2 == 0 else jnp.logical_not(lo) for h in heads]
    for u in range(DIL_SUB):
        blk = slice(u * QB, (u + 1) * QB)
        rows = pl.ds(dilation * u * QB + c, QB, stride=dilation) if dilation > 1 else blk
        if u == 0:
            k_before, v_before = kp_ref[...], vp_ref[...]
            valid = in_window & ((kj >= QB) | (i > 0))
        else:
            before = slice((u - 1) * QB, u * QB)
            k_before, v_before = kc_ref[before, :], vc_ref[before, :]
            valid = in_window
        k2 = jnp.concatenate([k_before, kc_ref[blk, :]], axis=0)
        v2 = jnp.concatenate([v_before, vc_ref[blk, :]], axis=0)
        q = q_ref[blk, :]
        qm = [jnp.where(sel[h], q[:, cs[h // 2]], jnp.zeros((QB, LANE), BF16)) for h in heads]
        s = [lax.dot_general(qm[h], k2[:, cs[h // 2]], NT_DIMS, preferred_element_type=F32) for h in heads]
        s = [jnp.where(valid, s[h], -jnp.inf) for h in heads]
        m_new = [jnp.max(s[h], axis=-1, keepdims=True) for h in heads]
        if not first:
            m_in = m_ref[rows, :]
            d_in = d_ref[rows, :]
            m_old = [jnp.sum(jnp.where(lane == h, m_in, 0.0), axis=-1, keepdims=True) for h in heads]
            d_old = [jnp.sum(jnp.where(lane == h, d_in, 0.0), axis=-1, keepdims=True) for h in heads]
            m_new = [jnp.maximum(m_new[h], m_old[h]) for h in heads]
        pr = [jnp.exp(s[h] - m_new[h]) for h in heads]
        d_new = [jnp.sum(pr[h], axis=-1, keepdims=True) for h in heads]
        if not first:
            alpha = [jnp.exp(m_old[h] - m_new[h]) for h in heads]
            d_new = [d_new[h] + d_old[h] * alpha[h] for h in heads]
        pv = [jnp.dot(pr[h].astype(BF16), v2[:, cs[h // 2]], preferred_element_type=F32) for h in heads]
        pairs = []
        for p in range(DIL_HEADS // 2):
            num = jnp.where(lo, pv[2 * p], pv[2 * p + 1])
            if not first:
                num = num + num_ref[p, rows, :] * jnp.where(lo, alpha[2 * p], alpha[2 * p + 1])
            if final:
                num = num / jnp.where(lo, d_new[2 * p], d_new[2 * p + 1])
            else:
                outs[0][p, rows, :] = num
            pairs.append(num)
        if final:
            outs[0][rows, :] = jnp.concatenate(pairs, axis=1).astype(BF16)
        else:
            outs[1][rows, :] = functools.reduce(jnp.add, [jnp.where(lane == h, m_new[h], 0.0) for h in heads])
            outs[2][rows, :] = functools.reduce(jnp.add, [jnp.where(lane == h, d_new[h], 0.0) for h in heads])


def _dil_branch(q, k, v, state, *, first, final):
    b, r, ts, _ = q.shape
    t = r * ts
    step = DIL_SUB * QB
    nq = ts // step
    cur = pl.BlockSpec((None, None, step, DIL_W), lambda bi, i, c: (bi, c, i, 0))
    prev = pl.BlockSpec((None, None, QB, DIL_W), lambda bi, i, c: (bi, c, jnp.maximum(DIL_SUB * i - 1, 0), 0))
    nat = lambda w: pl.BlockSpec((None, r * step, w), lambda bi, i, c: (bi, i, 0))
    nat_num = pl.BlockSpec((None, DIL_W // LANE, r * step, LANE), lambda bi, i, c: (bi, 0, i, 0))
    args = [q, k, k, v, v]
    in_specs = [cur, prev, cur, prev, cur]
    if not first:
        args += list(state)
        in_specs += [nat_num, nat(LANE), nat(LANE)]
    if final:
        assert r == 1
        out_specs = [nat(DIL_W)]
        out_shape = [jax.ShapeDtypeStruct((b, t, DIL_W), BF16)]
    else:
        out_specs = [nat_num, nat(LANE), nat(LANE)]
        out_shape = [jax.ShapeDtypeStruct((b, DIL_W // LANE, t, LANE), F32),
                     jax.ShapeDtypeStruct((b, t, LANE), F32),
                     jax.ShapeDtypeStruct((b, t, LANE), F32)]
    return pl.pallas_call(
        functools.partial(_dil_branch_kernel, first=first, final=final, dilation=r),
        grid=(b, nq, r),
        in_specs=in_specs,
        out_specs=out_specs,
        out_shape=out_shape,
        compiler_params=_params(("arbitrary", "arbitrary", "arbitrary")),
        name=f"dil_branch_r{r}",
    )(*args)


SAMPLE_HB = 6
SAMPLE_ROWS = 16


def _dil_sample_kernel(q_ref, kn_ref, vn_ref, knt_ref, vnt_ref, kc_ref, vc_ref, o_ref, ko_ref, vo_ref,
                       *, t_new, hist):
    t_row = lax.broadcasted_iota(jnp.int32, (SAMPLE_ROWS, 1), 0)
    j = lax.broadcasted_iota(jnp.int32, (1, hist), 1)
    lane = lax.broadcasted_iota(jnp.int32, (1, LANE), 1)
    keep = lane < LANE - t_new

    def weight(off):
        w = jnp.zeros(off.shape, F32)
        for win, dil in DIL_PAIRS:
            w = w + jnp.where((off >= 0) & (off <= win) & (off % dil == 0), 1.0, 0.0)
        return w

    w_c = weight(hist + t_row - j)
    w_n = [weight(t_row - t) for t in range(t_new)]
    heads = range(SAMPLE_HB)
    hs = [slice(h * HEAD_DIM, (h + 1) * HEAD_DIM) for h in heads]
    qs = [q_ref[:, hs[h]] for h in heads]
    s_c = [jnp.dot(qs[h].astype(BF16), kc_ref[h].astype(BF16), preferred_element_type=F32) for h in heads]
    s_c = [jnp.where(w_c > 0, s, -jnp.inf) for s in s_c]
    s_n = [[jnp.where(w_n[t] > 0, jnp.sum(qs[h] * kn_ref[t:t + 1, hs[h]], axis=-1, keepdims=True), -jnp.inf)
            for t in range(t_new)] for h in heads]
    m = [functools.reduce(jnp.maximum, s_n[h], jnp.max(s_c[h], axis=-1, keepdims=True)) for h in heads]
    p_c = [w_c * jnp.exp(s_c[h] - m[h]) for h in heads]
    p_n = [[w_n[t] * jnp.exp(s_n[h][t] - m[h]) for t in range(t_new)] for h in heads]
    den = [functools.reduce(jnp.add, p_n[h], jnp.sum(p_c[h], axis=-1, keepdims=True)) for h in heads]
    num = [lax.dot_general(p_c[h].astype(BF16), vc_ref[h].astype(BF16), NT_DIMS, preferred_element_type=F32)
           for h in heads]
    num = [functools.reduce(jnp.add, [p_n[h][t] * vn_ref[t:t + 1, hs[h]] for t in range(t_new)], num[h])
           for h in heads]
    o_ref[...] = jnp.concatenate([num[h] / den[h] for h in heads], axis=1)

    for h in heads:
        for src, new_ref, dst in ((kc_ref, knt_ref, ko_ref), (vc_ref, vnt_ref, vo_ref)):
            nxt = pltpu.roll(src[h, :, 0:LANE], LANE - t_new, axis=1)
            for c in range(hist // LANE):
                cur_t = nxt
                if c + 1 < hist // LANE:
                    nxt = pltpu.roll(src[h, :, (c + 1) * LANE:(c + 2) * LANE], LANE - t_new, axis=1)
                else:
                    nxt = new_ref[h]
                dst[h, :, c * LANE:(c + 1) * LANE] = jnp.where(keep, cur_t, nxt)


def _dil_sample(q, kn, vn, knt, vnt, cache_kt, cache_vt, t_new):
    b, nh, _, hist = cache_kt.shape
    w = SAMPLE_HB * HEAD_DIM
    small = pl.BlockSpec((None, SAMPLE_ROWS, w), lambda bi, c: (bi, 0, c))
    newt = pl.BlockSpec((None, SAMPLE_HB, HEAD_DIM, LANE), lambda bi, c: (bi, c, 0, 0))
    big = pl.BlockSpec((None, SAMPLE_HB, HEAD_DIM, hist), lambda bi, c: (bi, c, 0, 0))
    return pl.pallas_call(
        functools.partial(_dil_sample_kernel, t_new=t_new, hist=hist),
        grid=(b, nh // SAMPLE_HB),
        in_specs=[small, small, small, newt, newt, big, big],
        out_specs=[small, big, big],
        out_shape=[jax.ShapeDtypeStruct((b, SAMPLE_ROWS, nh * HEAD_DIM), F32),
                   jax.ShapeDtypeStruct(cache_kt.shape, F32),
                   jax.ShapeDtypeStruct(cache_vt.shape, F32)],
        compiler_params=_params(("arbitrary", "arbitrary")),
        name="dil_sample",
    )(q, kn, vn, knt, vnt, cache_kt, cache_vt)


OUT_ROWS = 256

STRIDE_ROWS = 256
STRIDE_GROUPS = STRIDE_ROWS // SUBLANE


def _to_strided(val, stage_ref):
    for j in range(val.shape[1] // LANE):
        for s_ in range(SUBLANE):
            stage_ref[j, pl.ds(s_, STRIDE_GROUPS, stride=SUBLANE), :] = (
                val[s_ * STRIDE_GROUPS:(s_ + 1) * STRIDE_GROUPS, j * LANE:(j + 1) * LANE])
    return jnp.concatenate([stage_ref[j] for j in range(val.shape[1] // LANE)], axis=1)


def _from_strided(val, stage_ref):
    for j in range(val.shape[1] // LANE):
        stage_ref[j] = val[:, j * LANE:(j + 1) * LANE]
    return jnp.concatenate(
        [jnp.concatenate([stage_ref[j, pl.ds(s_, STRIDE_GROUPS, stride=SUBLANE), :]
                          for j in range(val.shape[1] // LANE)], axis=1) for s_ in range(SUBLANE)], axis=0)


def _out_proj_kernel(oa_ref, ob_ref, w_ref, x_ref, g_ref, h_ref, xn_ref, wbf_ref, *stage, strided):
    @pl.when(pl.program_id(0) == 0)
    def _():
        wbf_ref[...] = w_ref[...].astype(BF16)

    for r0 in range(0, x_ref.shape[0], OUT_ROWS):
        rows = slice(r0, min(r0 + OUT_ROWS, x_ref.shape[0]))
        y = (jnp.dot(oa_ref[rows, :], wbf_ref[:GLA_VW, :], preferred_element_type=F32)
             + jnp.dot(ob_ref[rows, :], wbf_ref[GLA_VW:, :], preferred_element_type=F32))
        h = x_ref[rows, :] + y
        h_ref[rows, :] = h
        ms = jnp.mean(h * h, axis=-1, keepdims=True)
        xn = h * lax.rsqrt(ms + EPS) * g_ref[...]
        if strided:
            xn = _to_strided(xn, stage[0])
        xn_ref[rows, :] = xn.astype(BF16)


def _out_proj(oa, ob, w, x, g, tm, strided=False):
    m, d = x.shape
    assert not strided or OUT_ROWS == STRIDE_ROWS
    row = lambda width: pl.BlockSpec((tm, width), lambda i: (i, 0))
    g2 = g.reshape(1, d)
    scratch = [pltpu.VMEM(w.shape, BF16)]
    if strided:
        scratch.append(pltpu.VMEM((d // LANE, STRIDE_ROWS, LANE), F32))
    return pl.pallas_call(
        functools.partial(_out_proj_kernel, strided=strided),
        grid=(m // tm,),
        in_specs=[row(GLA_VW), row(DIL_W),
                  pl.BlockSpec(w.shape, lambda i: (0, 0), pipeline_mode=pl.Buffered(1)),
                  row(d), pl.BlockSpec(g2.shape, lambda i: (0, 0))],
        out_specs=[row(d), row(d)],
        out_shape=[jax.ShapeDtypeStruct((m, d), F32), jax.ShapeDtypeStruct((m, d), BF16)],
        scratch_shapes=scratch,
        compiler_params=_params(("arbitrary",)),
        name="out_proj",
    )(oa, ob, w, x, g2)


FF_ROWS = STRIDE_ROWS
FF_TAIL = 2 * SUBLANE
FF_SUB = FF_TN // LANE
FF_TILES = D_FF // LANE


def _conv_taps(cw, cb, up, prev1, prev2):
    return cb + cw[0:1] * prev2 + cw[1:2] * prev1 + cw[2:3] * up


def _swiglu(u, g):
    return (g / (1.0 + jnp.exp(-g)) * u).astype(BF16)


def _stage_up_weights(w_blocks, wbf_ref, j):
    for half in range(2):
        for q in range(FF_SUB):
            blk = w_blocks[half * FF_SUB + q][...]
            if (FF_NT - 1) * FF_SUB + q >= FF_TILES:
                blk = jnp.where(j * FF_SUB + q < FF_TILES, blk, 0.0)
            wbf_ref[half, :, q * LANE:(q + 1) * LANE] = blk.astype(BF16)


def _up_weight_specs(tile_of_step):
    specs = []
    for half in range(2):
        for q in range(FF_SUB):
            def index(*ids, half=half, q=q):
                blk = jnp.minimum(tile_of_step(*ids) * FF_SUB + q, FF_TILES - 1)
                return (0, half * FF_TILES + blk)
            specs.append(pl.BlockSpec((D_MODEL, LANE), index))
    return specs


def _ffn_up_seq_kernel(*refs, tiles_per_seq):
    x_ref = refs[0]
    w_blocks = refs[1:1 + 2 * FF_SUB]
    cwu_ref, cwg_ref, cbu_ref, cbg_ref, act_ref, tu_ref, tg_ref, wbf_ref, carry_ref = refs[1 + 2 * FF_SUB:]
    j = pl.program_id(0)
    i = pl.program_id(1)
    tm = x_ref.shape[0]

    @pl.when(i == 0)
    def _():
        _stage_up_weights(w_blocks, wbf_ref, j)

    @pl.when(i % tiles_per_seq == 0)
    def _():
        carry_ref[...] = jnp.zeros_like(carry_ref)

    carry = [carry_ref[0], carry_ref[1]]
    shift = lambda before, cur: pltpu.roll(jnp.concatenate([before, cur], axis=0), 1, axis=0)[SUBLANE:]
    for r0 in range(0, tm, FF_ROWS):
        x = x_ref[r0:r0 + FF_ROWS, :]
        conv = []
        for idx, (cw_ref, cb_ref) in enumerate(((cwu_ref, cbu_ref), (cwg_ref, cbg_ref))):
            up = jnp.dot(x, wbf_ref[idx], preferred_element_type=F32)
            tail = up[FF_ROWS - FF_TAIL:]
            p_a = shift(carry[idx][:SUBLANE], tail[:SUBLANE])
            p_b = shift(carry[idx][SUBLANE:], tail[SUBLANE:])
            prev1 = jnp.concatenate([p_b, up[:FF_ROWS - SUBLANE]], axis=0)
            prev2 = jnp.concatenate([p_a, p_b, up[:FF_ROWS - 2 * SUBLANE]], axis=0)
            carry[idx] = tail
            conv.append(_conv_taps(cw_ref[...], cb_ref[...], up, prev1, prev2))
        act_ref[r0:r0 + FF_ROWS, :] = _swiglu(*conv)
    for idx, tail_ref in enumerate((tu_ref, tg_ref)):
        carry_ref[idx] = carry[idx]
        tail_ref[...] = carry[idx]


def _ffn_up_seq(xn, w_up, conv_w, conv_b, tm, t_seq):
    m, d = xn.shape
    tiles_per_seq = t_seq // tm
    xs = pl.BlockSpec((tm, d), lambda j, i: (i, 0))
    cu = lambda r: pl.BlockSpec((r, FF_TN), lambda j, i: (0, j))
    cg = lambda r: pl.BlockSpec((r, FF_TN), lambda j, i: (0, FF_NT + j))
    act = pl.BlockSpec((tm, FF_TN), lambda j, i: (i, j))
    tail = pl.BlockSpec((None, FF_TAIL, FF_TN), lambda j, i: (i // tiles_per_seq, 0, j))
    w_specs = _up_weight_specs(lambda j, i: j)
    return pl.pallas_call(
        functools.partial(_ffn_up_seq_kernel, tiles_per_seq=tiles_per_seq),
        grid=(FF_NT, m // tm),
        in_specs=[xs] + w_specs + [cu(CONV_W), cg(CONV_W), cu(1), cg(1)],
        out_specs=[act, tail, tail],
        out_shape=[jax.ShapeDtypeStruct((m, D_FF_P), BF16)]
        + [jax.ShapeDtypeStruct((m // t_seq, FF_TAIL, D_FF_P), F32)] * 2,
        scratch_shapes=[pltpu.VMEM((2, d, FF_TN), BF16), pltpu.VMEM((2, FF_TAIL, FF_TN), F32)],
        compiler_params=_params(("arbitrary", "arbitrary")),
        name="ffn_up",
    )(xn, *([w_up] * len(w_specs)), conv_w, conv_w, conv_b, conv_b)


def _ffn_up_hist_kernel(*refs, t_seq):
    x_ref = refs[0]
    w_blocks = refs[1:1 + 2 * FF_SUB]
    (cwu_ref, cwg_ref, cbu_ref, cbg_ref, h1u_ref, h1g_ref, h2u_ref, h2g_ref,
     act_ref, upu_ref, upg_ref, wbf_ref) = refs[1 + 2 * FF_SUB:]
    _stage_up_weights(w_blocks, wbf_ref, pl.program_id(0))
    x = x_ref[...]
    t_in = lax.broadcasted_iota(jnp.int32, (x.shape[0], 1), 0) % t_seq
    conv = []
    for idx, (cw_ref, cb_ref, h1_ref, h2_ref, up_ref) in enumerate((
            (cwu_ref, cbu_ref, h1u_ref, h2u_ref, upu_ref), (cwg_ref, cbg_ref, h1g_ref, h2g_ref, upg_ref))):
        up = jnp.dot(x, wbf_ref[idx], preferred_element_type=F32)
        prev1 = jnp.where(t_in >= 1, pltpu.roll(up, 1, axis=0), 0.0) + h1_ref[...]
        prev2 = jnp.where(t_in >= 2, pltpu.roll(up, 2, axis=0), 0.0) + h2_ref[...]
        up_ref[...] = up
        conv.append(_conv_taps(cw_ref[...], cb_ref[...], up, prev1, prev2))
    act_ref[...] = _swiglu(*conv)


def _ffn_up_hist(xn, w_up, conv_w, conv_b, h1, h2, t_seq):
    m, d = xn.shape
    xs = pl.BlockSpec((m, d), lambda j: (0, 0))
    u = lambda r: pl.BlockSpec((r, FF_TN), lambda j: (0, j))
    g = lambda r: pl.BlockSpec((r, FF_TN), lambda j: (0, FF_NT + j))
    w_specs = _up_weight_specs(lambda j: j)
    return pl.pallas_call(
        functools.partial(_ffn_up_hist_kernel, t_seq=t_seq),
        grid=(FF_NT,),
        in_specs=[xs] + w_specs + [u(CONV_W), g(CONV_W), u(1), g(1), u(m), g(m), u(m), g(m)],
        out_specs=[u(m), u(m), u(m)],
        out_shape=[jax.ShapeDtypeStruct((m, D_FF_P), BF16)] + [jax.ShapeDtypeStruct((m, D_FF_P), F32)] * 2,
        scratch_shapes=[pltpu.VMEM((2, d, FF_TN), BF16)],
        compiler_params=_params(("arbitrary",)),
        name="ffn_up_hist",
    )(xn, *([w_up] * len(w_specs)), conv_w, conv_w, conv_b, conv_b, h1, h1, h2, h2)


DOWN_ROWS = 256


def _ffn_down_kernel(a_ref, w_ref, h_ref, g_ref, y_ref, *stage, strided):
    kdim = w_ref.shape[0]
    for r0 in range(0, a_ref.shape[0], DOWN_ROWS):
        rows = slice(r0, min(r0 + DOWN_ROWS, a_ref.shape[0]))
        f = jnp.dot(a_ref[rows, :kdim], w_ref[...], preferred_element_type=F32)
        if strided:
            f = _from_strided(f, stage[0])
        h = h_ref[rows, :] + f
        ms = jnp.mean(h * h, axis=-1, keepdims=True)
        y_ref[rows, :] = h * lax.rsqrt(ms + EPS) * g_ref[...]


def _ffn_down(act, w, h, g, tm, strided=False):
    m, d = h.shape
    assert not strided or DOWN_ROWS == STRIDE_ROWS
    scratch = [pltpu.VMEM((d // LANE, STRIDE_ROWS, LANE), F32)] if strided else []
    return pl.pallas_call(
        functools.partial(_ffn_down_kernel, strided=strided),
        grid=(m // tm,),
        in_specs=[pl.BlockSpec((tm, act.shape[1]), lambda i: (i, 0)),
                  pl.BlockSpec(w.shape, lambda i: (0, 0), pipeline_mode=pl.Buffered(1)),
                  pl.BlockSpec((tm, d), lambda i: (i, 0)),
                  pl.BlockSpec((1, d), lambda i: (0, 0))],
        out_specs=pl.BlockSpec((tm, d), lambda i: (i, 0)),
        out_shape=jax.ShapeDtypeStruct((m, d), F32),
        scratch_shapes=scratch,
        compiler_params=_params(("arbitrary",)),
        name="ffn_down",
    )(act, w, h, g.reshape(1, d))


def _pad_cols(a, width):
    return jnp.pad(a, ((0, 0), (0, width - a.shape[1])))


def _split_pad_ff(a):
    return jnp.concatenate([_pad_cols(a[:, :D_FF], D_FF_P), _pad_cols(a[:, D_FF:], D_FF_P)], axis=1)


def _unpad_ff(u, g):
    return jnp.concatenate([u[..., :D_FF], g[..., :D_FF]], axis=-1)


def kernel(x_prompt, x_sample, state_gla, cache_dil_k, cache_dil_v, state_ffn_conv, norm_mix, w_in, w_gate_up,
           b_gate, gla_norm, w_out, norm_ffn, w_ffn_up, ffn_conv_w, ffn_conv_b, w_ffn_down, norm_final):
    bp, tp, d = x_prompt.shape
    bs, ts, _ = x_sample.shape
    l = 0

    z0 = 2 * GLA_QK + 2 * GLA_VW
    w_in_t = w_in[l].T
    w_dil_t = w_in_t[z0 + GLA_RANK:]
    w_z_t = jnp.pad(w_in_t[z0:z0 + GLA_RANK], ((0, Z_PAD - GLA_RANK), (0, 0)))
    wg_f = jnp.pad(w_gate_up[l], ((0, Z_PAD - GLA_RANK), (0, 0)))
    wg_hi = wg_f.astype(BF16)
    wg_p = jnp.stack([wg_hi, (wg_f - wg_hi.astype(F32)).astype(BF16)])
    bg = b_gate[l].reshape(1, GLA_QK)
    gn = gla_norm[l].reshape(1, GLA_DV)
    cw_p = _split_pad_ff(ffn_conv_w[l])
    cb_p = _split_pad_ff(ffn_conv_b[l].reshape(1, -1))
    w_dn = w_ffn_down[l].astype(BF16)

    mp = bp * tp
    xp = x_prompt.reshape(mp, d)
    xn_mix_p = _rms(xp, norm_mix[l], 512)
    mix_p = _proj(xn_mix_p, w_in_t, 2048, 768, n=z0)
    dil_p = _proj(xn_mix_p, w_dil_t, 2048, 768)
    z_p = _proj(xn_mix_p, w_z_t, 2048, Z_PAD)
    cos_p, sin_p = _rope_tables(jnp.arange(tp))
    cos_p = jnp.tile(cos_p, (bp, 1))
    sin_p = jnp.tile(sin_p, (bp, 1))
    rope_out = _rope(dil_p, cos_p, sin_p, 512, n_seq=bp)
    by_class = {1: tuple(a.reshape(bp, 1, tp, DIL_W) for a in rope_out[:3])}
    for di, r in enumerate(CLASS_DILATIONS):
        by_class[r] = tuple(rope_out[3 + 3 * di:6 + 3 * di])
    k_kept_t, v_kept_t = rope_out[-2:]

    oa_p, gla_p = _gla(mix_p.reshape(bp, tp, z0), z_p.reshape(bp, tp, Z_PAD), wg_p, bg, gn,
                       jnp.zeros((bp, GLA_HEADS, GLA_DK, GLA_DV), F32),
                       chunk=GLA_ROWS, n_chunks=2, t_valid=tp)

    state = _dil_branch(*by_class[16], None, first=True, final=False)
    state = _dil_branch(*by_class[4], state, first=False, final=False)
    (ob_p,) = _dil_branch(*by_class[1], state, first=False, final=True)

    h_p, xn_p = _out_proj(oa_p.reshape(mp, GLA_VW), ob_p.reshape(mp, DIL_W), w_out[l], xp, norm_ffn[l], 512, strided=True)
    act_p, tail_u, tail_g = _ffn_up_seq(xn_p, w_ffn_up[l], cw_p, cb_p, 1024, tp)
    y_p = _ffn_down(act_p, w_dn, h_p, norm_final, 512, strided=True)

    buf_p = min(DIL_WINDOW, tp)
    y_prompt = y_p.reshape(bp, tp, d)
    new_gla_p = gla_p[None]
    new_k_p = k_kept_t.reshape(bp, DIL_HEADS, HEAD_DIM, buf_p).transpose(0, 3, 1, 2)[None]
    new_v_p = v_kept_t.reshape(bp, DIL_HEADS, HEAD_DIM, buf_p).transpose(0, 3, 1, 2)[None]
    last_two = lambda a: a[:, SUBLANE - 1::SUBLANE]
    new_conv_p = _unpad_ff(last_two(tail_u), last_two(tail_g))[None]

    ms_ = bs * ts
    xs = x_sample.reshape(ms_, d)
    xn_mix_s = _rms(xs, norm_mix[l], ms_)
    mix_s = _proj(xn_mix_s, w_in_t, ms_, 768, n=z0)
    dil_s = _proj(xn_mix_s, w_dil_t, ms_, 768)
    z_s = _proj(xn_mix_s, w_z_t, ms_, Z_PAD)
    cos_s, sin_s = _rope_tables(PAST_LEN + jnp.arange(ts))
    cos_s = jnp.tile(cos_s, (bs, 1))
    sin_s = jnp.tile(sin_s, (bs, 1))
    q_rs, k_rs = _rope(dil_s, cos_s, sin_s, ms_)

    pad_chunk = lambda a: jnp.pad(a.reshape(bs, ts, -1), ((0, 0), (0, GLA_ROWS_SHORT - ts), (0, 0)))
    oa_s, gla_s = _gla(pad_chunk(mix_s), pad_chunk(z_s), wg_p, bg, gn, state_gla[l],
                       chunk=GLA_ROWS_SHORT, n_chunks=1, t_valid=ts)
    oa_s = oa_s[:, :ts].reshape(ms_, GLA_VW)

    v_s = dil_s[:, 2 * DIL_W:]
    pad_rows = lambda a: jnp.pad(a.reshape(bs, ts, DIL_W), ((0, 0), (0, SAMPLE_ROWS - ts), (0, 0)))
    new_cols = lambda a: jnp.pad(a.reshape(bs, ts, DIL_HEADS, HEAD_DIM).transpose(0, 2, 3, 1),
                                 ((0, 0), (0, 0), (0, 0), (LANE - ts, 0)))
    time_minor = lambda a: a.transpose(0, 2, 3, 1)
    ob_s, k_s_t, v_s_t = _dil_sample(pad_rows(q_rs), pad_rows(k_rs), pad_rows(v_s), new_cols(k_rs), new_cols(v_s),
                                     time_minor(cache_dil_k[l]), time_minor(cache_dil_v[l]), ts)
    ob_s = ob_s[:, :ts].astype(BF16)

    h_s, xn_s = _out_proj(oa_s, ob_s.reshape(ms_, DIL_W), w_out[l], xs, norm_ffn[l], ms_)
    conv_hist = _split_pad_ff(state_ffn_conv[l].reshape(bs * (CONV_W - 1), 2 * D_FF))
    conv_hist = conv_hist.reshape(bs, CONV_W - 1, 2 * D_FF_P)
    zero_row = jnp.zeros((bs, 1, 2 * D_FF_P), F32)
    h1 = jnp.concatenate([conv_hist[:, 1:2]] + [zero_row] * (ts - 1), axis=1).reshape(ms_, 2 * D_FF_P)
    h2 = jnp.concatenate([conv_hist[:, 0:1], conv_hist[:, 1:2]] + [zero_row] * (ts - 2),
                         axis=1).reshape(ms_, 2 * D_FF_P)
    act_s, up_u, up_g = _ffn_up_hist(xn_s, w_ffn_up[l], cw_p, cb_p, h1, h2, ts)
    y_s = _ffn_down(act_s, w_dn, h_s, norm_final, ms_)

    y_sample = y_s.reshape(bs, ts, d)
    new_gla_s = gla_s[None]
    new_k_s = k_s_t.transpose(0, 3, 1, 2)[None]
    new_v_s = v_s_t.transpose(0, 3, 1, 2)[None]
    up_full = _unpad_ff(up_u, up_g).reshape(bs, ts, 2 * D_FF)
    new_conv_s = up_full[:, ts - (CONV_W - 1):][None]

    return (y_prompt, y_sample, new_gla_p, new_gla_s, new_k_p, new_k_s, new_v_p, new_v_s,
            new_conv_p, new_conv_s)
```

```python
import functools

import jax
import jax.numpy as jnp
from jax import lax
from jax.experimental import pallas as pl
from jax.experimental.pallas import tpu as pltpu

F32 = jnp.float32
BF16 = jnp.bfloat16

D_MODEL = 2048
HEAD_DIM = 64
GLA_HEADS = 10
GLA_DK = 64
GLA_DV = 128
GLA_RANK = 16
GLA_TAU = 16.0
GLA_ROWS = 128
GLA_ROWS_SHORT = 16
DIL_HEADS = 12
DIL_PAIRS = ((128, 1), (512, 4), (2048, 16))
DIL_WINDOW = 2048
ROPE_THETA = 10000.0
D_FF = 5504
PAST_LEN = 8192
CONV_W = 3
EPS = 1e-6

GLA_QK = GLA_HEADS * GLA_DK
GLA_VW = GLA_HEADS * GLA_DV
DIL_W = DIL_HEADS * HEAD_DIM

LANE = 128
SUBLANE = 8
VMEM_LIMIT = 56 * 1024 * 1024

Z_PAD = LANE
D_FF_P = 5632
FF_TN = 512
FF_NT = D_FF_P // FF_TN

NT_DIMS = (((1,), (1,)), ((), ()))


def _params(sem, vmem=VMEM_LIMIT, flags=None):
    return pltpu.CompilerParams(dimension_semantics=sem, vmem_limit_bytes=vmem, flags=flags)


def _rms_kernel(x_ref, g_ref, o_ref):
    x = x_ref[...]
    ms = jnp.mean(x * x, axis=-1, keepdims=True)
    o_ref[...] = (x * lax.rsqrt(ms + EPS) * g_ref[...]).astype(BF16)


def _rms(x, g, tm):
    m, d = x.shape
    return pl.pallas_call(
        _rms_kernel,
        grid=(m // tm,),
        in_specs=[pl.BlockSpec((tm, d), lambda i: (i, 0)), pl.BlockSpec((1, d), lambda i: (0, 0))],
        out_specs=pl.BlockSpec((tm, d), lambda i: (i, 0)),
        out_shape=jax.ShapeDtypeStruct((m, d), BF16),
        compiler_params=_params(("arbitrary",)),
        name="rms",
    )(x, g.reshape(1, d))


def _proj_kernel(x_ref, w_ref, o_ref, wbf_ref):
    @pl.when(pl.program_id(1) == 0)
    def _():
        wbf_ref[...] = w_ref[...].astype(BF16)

    o_ref[...] = lax.dot_general(x_ref[...], wbf_ref[...], NT_DIMS, preferred_element_type=F32)


def _proj(xn, wt, tm, tn, n=None):
    m, k = xn.shape
    n = wt.shape[0] if n is None else n
    return pl.pallas_call(
        _proj_kernel,
        grid=(n // tn, m // tm),
        in_specs=[pl.BlockSpec((tm, k), lambda j, i: (i, 0)), pl.BlockSpec((tn, k), lambda j, i: (j, 0))],
        out_specs=pl.BlockSpec((tm, tn), lambda j, i: (i, j)),
        out_shape=jax.ShapeDtypeStruct((m, n), F32),
        scratch_shapes=[pltpu.VMEM((tn, k), BF16)],
        compiler_params=_params(("arbitrary", "arbitrary")),
        name="proj",
    )(xn, wt)


CLASS_DILATIONS = tuple(r for _, r in DIL_PAIRS if r > 1)


def _rope_kernel(q_ref, k_ref, v_ref, cos_ref, sin_ref, *rest, by_class, tiles, first_kept):
    reps = DIL_W // LANE
    cos = jnp.concatenate([cos_ref[...]] * reps, axis=1)
    sin = jnp.concatenate([sin_ref[...]] * reps, axis=1)
    lane = lax.broadcasted_iota(jnp.int32, (1, DIL_W), 1)
    first_half = (lane % HEAD_DIM) < (HEAD_DIM // 2)

    def rot(x):
        partner = jnp.where(first_half,
                            pltpu.roll(x, DIL_W - HEAD_DIM // 2, axis=1),
                            pltpu.roll(x, HEAD_DIM // 2, axis=1))
        return x * cos + partner * sin

    q = rot(q_ref[...]) * (HEAD_DIM ** -0.5)
    k = rot(k_ref[...])
    if not by_class:
        qo_ref, ko_ref = rest
        qo_ref[...] = q
        ko_ref[...] = k
        return
    n_class = 3 * len(CLASS_DILATIONS)
    qb_ref, kb_ref, vb_ref = rest[:3]
    class_refs = rest[3:3 + n_class]
    kt_ref, vt_ref, sq_ref, sk_ref, sv_ref = rest[3 + n_class:]
    v = v_ref[...]
    qb_ref[...] = q.astype(BF16)
    kb_ref[...] = k.astype(BF16)
    vb_ref[...] = v.astype(BF16)

    @pl.when(pl.program_id(0) % tiles >= first_kept)
    def _():
        kt_ref[...] = k.T
        vt_ref[...] = v.T

    for src, val in ((sq_ref, q), (sk_ref, k), (sv_ref, v)):
        for j in range(reps):
            src[j] = val[:, j * LANE:(j + 1) * LANE]
    tm = q.shape[0]
    for di, r in enumerate(CLASS_DILATIONS):
        for src, dst in zip((sq_ref, sk_ref, sv_ref), class_refs[3 * di:3 * di + 3]):
            for c in range(r):
                rows = pl.ds(c, tm // r, stride=r)
                dst[c] = jnp.concatenate([src[j, rows, :] for j in range(reps)], axis=1).astype(BF16)


def _rope(proj, cos_t, sin_t, tm, n_seq=None):
    m = proj.shape[0]
    by_class = n_seq is not None
    blk = lambda c: pl.BlockSpec((tm, DIL_W), lambda i, c=c: (i, c))
    tab = pl.BlockSpec((tm, LANE), lambda i: (i, 0))
    out = pl.BlockSpec((tm, DIL_W), lambda i: (i, 0))
    tiles, first_kept, scratch = 1, 0, []
    if by_class:
        t_seq = m // n_seq
        tiles = t_seq // tm
        window = min(DIL_WINDOW, t_seq)
        first_kept = (t_seq - window) // tm
        out_specs = [out, out, out]
        out_shape = [jax.ShapeDtypeStruct((m, DIL_W), BF16)] * 3
        for r in CLASS_DILATIONS:
            spec = pl.BlockSpec((None, r, tm // r, DIL_W), lambda i: (i // tiles, 0, i % tiles, 0))
            out_specs += [spec] * 3
            out_shape += [jax.ShapeDtypeStruct((n_seq, r, t_seq // r, DIL_W), BF16)] * 3
        kept = pl.BlockSpec((None, DIL_W, tm), lambda i: (i // tiles, 0, jnp.maximum(i % tiles - first_kept, 0)))
        out_specs += [kept] * 2
        out_shape += [jax.ShapeDtypeStruct((n_seq, DIL_W, window), F32)] * 2
        scratch = [pltpu.VMEM((DIL_W // LANE, tm, LANE), F32)] * 3
    else:
        out_specs = [out, out]
        out_shape = [jax.ShapeDtypeStruct((m, DIL_W), F32)] * 2
    return pl.pallas_call(
        functools.partial(_rope_kernel, by_class=by_class, tiles=tiles, first_kept=first_kept),
        grid=(m // tm,),
        in_specs=[blk(0), blk(1), blk(2), tab, tab],
        out_specs=out_specs,
        out_shape=out_shape,
        scratch_shapes=scratch,
        compiler_params=_params(("arbitrary",)),
        name="rope",
    )(proj, proj, proj, cos_t, sin_t)


def _rope_tables(pos):
    half = HEAD_DIM // 2
    inv_freq = ROPE_THETA ** (-2.0 * jnp.arange(half, dtype=F32) / HEAD_DIM)
    ang = pos.astype(F32)[:, None] * inv_freq[None, :]
    cos = jnp.cos(ang)
    sin = jnp.sin(ang)
    cos_t = jnp.concatenate([cos, cos, cos, cos], axis=1)
    sin_t = jnp.concatenate([-sin, sin, -sin, sin], axis=1)
    return cos_t, sin_t


GLA_PAIRS = GLA_HEADS // 2


def _gla_kernel(q_ref, k_ref, v_ref, r_ref, z_ref, wg_ref, bg_ref, gn_ref, s0_ref,
                o_ref, sfin_ref, st_ref, *, chunk, n_chunks, t_valid):
    i = pl.program_id(1)
    pairs = range(GLA_PAIRS)

    @pl.when(i == 0)
    def _():
        for p in pairs:
            st_ref[p] = jnp.concatenate([s0_ref[2 * p].T, s0_ref[2 * p + 1].T], axis=1)

    row = lax.broadcasted_iota(jnp.int32, (chunk, chunk), 0)
    col = lax.broadcasted_iota(jnp.int32, (chunk, chunk), 1)
    tril = (row >= col).astype(BF16)
    t_in = lax.broadcasted_iota(jnp.int32, (chunk, 1), 0)
    both = lambda m: jnp.concatenate([m, m], axis=0)
    levels = []
    s_ = chunk
    while s_ >= 2:
        half = s_ // 2
        pivot = (row // s_) * s_ + (half - 1)
        levels.append(dict(
            size=s_,
            upto=(col <= pivot).astype(BF16),
            pair=both((row // s_ == col // s_) & (row % s_ >= half) & (col % s_ < half)),
            second=(t_in % s_) >= half))
        s_ = half
    diag = both(row == col)
    lo = lax.broadcasted_iota(jnp.int32, (1, LANE), 1) < GLA_DK
    mm = functools.partial(jnp.dot, preferred_element_type=F32)
    nt = functools.partial(lax.dot_general, dimension_numbers=NT_DIMS, preferred_element_type=F32)

    def split(a):
        hi = a.astype(BF16)
        return hi, (a - hi.astype(F32)).astype(BF16)

    def stack(a, p):
        t = a[:, p * LANE:(p + 1) * LANE]
        zero = jnp.zeros_like(t)
        return jnp.concatenate([jnp.where(lo, t, zero), jnp.where(lo, zero, t)], axis=0)

    for c in range(n_chunks):
        sl = pl.ds(c * chunk, chunk)
        z_hi, z_lo = split(z_ref[sl, :])
        x = mm(z_hi, wg_ref[0]) + mm(z_hi, wg_ref[1]) + mm(z_lo, wg_ref[0]) + bg_ref[...]
        log_a = -(jnp.maximum(-x, 0.0) + jnp.log1p(jnp.exp(-jnp.abs(x)))) / GLA_TAU
        t_abs = (i * n_chunks + c) * chunk + t_in
        log_a = jnp.where(t_abs < t_valid, log_a, 0.0)
        a_hi, a_lo = split(log_a)
        cum = mm(tril, a_hi) + mm(tril, a_lo)
        last = cum[chunk - 1:chunk, :]
        q = q_ref[sl, :] * (GLA_DK ** -0.5)
        k = k_ref[sl, :]
        q_dec = (q * jnp.exp(cum)).astype(BF16)
        k_end = (k * jnp.exp(last - cum)).astype(BF16)
        a_end = jnp.exp(last)
        heads = range(GLA_HEADS)
        vs = [slice(h * GLA_DV, (h + 1) * GLA_DV) for h in heads]
        v_f = [v_ref[sl, vs[h]] for h in heads]
        vh = [v_f[h].astype(BF16) for h in heads]
        pivot_cum = []
        for lv in levels:
            if lv["size"] >= SUBLANE:
                blocks = cum.reshape(chunk // lv["size"], lv["size"], cum.shape[1])
                pick = blocks[:, lv["size"] // 2 - 1:lv["size"] // 2, :]
                pivot_cum.append(jnp.broadcast_to(pick, blocks.shape).reshape(cum.shape))
            else:
                pivot_cum.append(mm(lv["upto"], a_hi) + mm(lv["upto"], a_lo))
        q_lv = [(q * jnp.exp(jnp.where(lv["second"], cum - pc, 0.0))).astype(BF16)
                for lv, pc in zip(levels, pivot_cum)]
        k_lv = [(k * jnp.exp(jnp.where(lv["second"], 0.0, pc - cum))).astype(BF16)
                for lv, pc in zip(levels, pivot_cum)]
        q_bf, k_bf = q.astype(BF16), k.astype(BF16)
        lanes = [slice(p * LANE, (p + 1) * LANE) for p in pairs]
        parts = [[nt(stack(ql, p), kl[:, lanes[p]]) for p in pairs] for ql, kl in zip(q_lv, k_lv)]
        scores = [jnp.where(diag, nt(stack(q_bf, p), k_bf[:, lanes[p]]), 0.0) for p in pairs]
        for lv, part in zip(levels, parts):
            scores = [jnp.where(lv["pair"], part[p], scores[p]) for p in pairs]
        scores = [scores[p].astype(BF16) for p in pairs]
        st = [st_ref[p] for p in pairs]
        carried = [nt(stack(q_dec, p), st[p].astype(BF16)) for p in pairs]
        o = [mm(scores[h // 2][(h % 2) * chunk:(h % 2 + 1) * chunk], vh[h])
             + carried[h // 2][(h % 2) * chunk:(h % 2 + 1) * chunk] for h in heads]
        for p in pairs:
            ke = k_end[:, lanes[p]]
            zero = jnp.zeros_like(ke)
            st_ref[p] = (st[p] * a_end[:, lanes[p]]
                         + mm(v_f[2 * p].T.astype(BF16), jnp.where(lo, ke, zero))
                         + mm(v_f[2 * p + 1].T.astype(BF16), jnp.where(lo, zero, ke)))
        ms = [jnp.mean(o[h] * o[h], axis=-1, keepdims=True) for h in heads]
        for h in heads:
            on = o[h] * lax.rsqrt(ms[h] + EPS) * gn_ref[...]
            r = r_ref[sl, vs[h]]
            o_ref[sl, vs[h]] = (on * (r / (1.0 + jnp.exp(-r)))).astype(BF16)

    @pl.when(i == pl.num_programs(1) - 1)
    def _():
        for p in pairs:
            st = st_ref[p]
            sfin_ref[2 * p] = st[:, :GLA_DK].T
            sfin_ref[2 * p + 1] = st[:, GLA_DK:].T


def _gla(src, zsrc, wg, bg, gn, s0, *, chunk, n_chunks, t_valid):
    b, t, _ = src.shape
    tb = chunk * n_chunks
    col = lambda w, c: pl.BlockSpec((None, tb, w), lambda bi, i, c=c: (bi, i, c))
    full = lambda shape: pl.BlockSpec(shape, lambda bi, i: (0,) * len(shape))
    state = pl.BlockSpec((None, GLA_HEADS, GLA_DK, GLA_DV), lambda bi, i: (bi, 0, 0, 0))
    kern = functools.partial(_gla_kernel, chunk=chunk, n_chunks=n_chunks, t_valid=t_valid)
    return pl.pallas_call(
        kern,
        grid=(b, t // tb),
        in_specs=[col(GLA_QK, 0), col(GLA_QK, 1), col(GLA_VW, 1), col(GLA_VW, 2), col(Z_PAD, 0),
                  full((2, Z_PAD, GLA_QK)), full((1, GLA_QK)), full((1, GLA_DV)), state],
        out_specs=[pl.BlockSpec((None, tb, GLA_VW), lambda bi, i: (bi, i, 0)), state],
        out_shape=[jax.ShapeDtypeStruct((b, t, GLA_VW), BF16),
                   jax.ShapeDtypeStruct((b, GLA_HEADS, GLA_DK, GLA_DV), F32)],
        scratch_shapes=[pltpu.VMEM((GLA_PAIRS, GLA_DV, 2 * GLA_DK), F32)],
        compiler_params=_params(("arbitrary", "arbitrary")),
        name="gla",
    )(src, src, src, src, zsrc, wg, bg, gn, s0)


QB = 128
DIL_SUB = 2


def _dil_branch_kernel(*refs, first, final, dilation):
    if first:
        q_ref, kp_ref, kc_ref, vp_ref, vc_ref = refs[:5]
        outs = refs[5:]
    else:
        q_ref, kp_ref, kc_ref, vp_ref, vc_ref, num_ref, m_ref, d_ref = refs[:8]
        outs = refs[8:]
    i = pl.program_id(1)
    c = pl.program_id(2)
    qi = QB + lax.broadcasted_iota(jnp.int32, (QB, 1), 0)
    kj = lax.broadcasted_iota(jnp.int32, (1, 2 * QB), 1)
    off = qi - kj
    in_window = (off >= 0) & (off <= QB)
    lane = lax.broadcasted_iota(jnp.int32, (1, LANE), 1)
    lo = lane < HEAD_DIM
    heads = range(DIL_HEADS)
    cs = [slice(p * LANE, (p + 1) * LANE) for p in range(DIL_HEADS // 2)]
    sel = [lo if h % 2 == 0 else jnp.logical_not(lo) for h in heads]
    for u in range(DIL_SUB):
        blk = slice(u * QB, (u + 1) * QB)
        rows = pl.ds(dilation * u * QB + c, QB, stride=dilation) if dilation > 1 else blk
        if u == 0:
            k_before, v_before = kp_ref[...], vp_ref[...]
            valid = in_window & ((kj >= QB) | (i > 0))
        else:
            before = slice((u - 1) * QB, u * QB)
            k_before, v_before = kc_ref[before, :], vc_ref[before, :]
            valid = in_window
        k2 = jnp.concatenate([k_before, kc_ref[blk, :]], axis=0)
        v2 = jnp.concatenate([v_before, vc_ref[blk, :]], axis=0)
        q = q_ref[blk, :]
        qm = [jnp.where(sel[h], q[:, cs[h // 2]], jnp.zeros((QB, LANE), BF16)) for h in heads]
        s = [lax.dot_general(qm[h], k2[:, cs[h // 2]], NT_DIMS, preferred_element_type=F32) for h in heads]
        s = [jnp.where(valid, s[h], -jnp.inf) for h in heads]
        m_new = [jnp.max(s[h], axis=-1, keepdims=True) for h in heads]
        if not first:
            m_in = m_ref[rows, :]
            d_in = d_ref[rows, :]
            m_old = [jnp.sum(jnp.where(lane == h, m_in, 0.0), axis=-1, keepdims=True) for h in heads]
            d_old = [jnp.sum(jnp.where(lane == h, d_in, 0.0), axis=-1, keepdims=True) for h in heads]
            m_new = [jnp.maximum(m_new[h], m_old[h]) for h in heads]
        pr = [jnp.exp(s[h] - m_new[h]) for h in heads]
        d_new = [jnp.sum(pr[h], axis=-1, keepdims=True) for h in heads]
        if not first:
            alpha = [jnp.exp(m_old[h] - m_new[h]) for h in heads]
            d_new = [d_new[h] + d_old[h] * alpha[h] for h in heads]
        pv = [jnp.dot(pr[h].astype(BF16), v2[:, cs[h // 2]], preferred_element_type=F32) for h in heads]
        pairs = []
        for p in range(DIL_HEADS // 2):
            num = jnp.where(lo, pv[2 * p], pv[2 * p + 1])
            if not first:
                num = num + num_ref[p, rows, :] * jnp.where(lo, alpha[2 * p], alpha[2 * p + 1])
            if final:
                num = num / jnp.where(lo, d_new[2 * p], d_new[2 * p + 1])
            else:
                outs[0][p, rows, :] = num
            pairs.append(num)
        if final:
            outs[0][rows, :] = jnp.concatenate(pairs, axis=1).astype(BF16)
        else:
            outs[1][rows, :] = functools.reduce(jnp.add, [jnp.where(lane == h, m_new[h], 0.0) for h in heads])
            outs[2][rows, :] = functools.reduce(jnp.add, [jnp.where(lane == h, d_new[h], 0.0) for h in heads])


def _dil_branch(q, k, v, state, *, first, final):
    b, r, ts, _ = q.shape
    t = r * ts
    step = DIL_SUB * QB
    nq = ts // step
    cur = pl.BlockSpec((None, None, step, DIL_W), lambda bi, i, c: (bi, c, i, 0))
    prev = pl.BlockSpec((None, None, QB, DIL_W), lambda bi, i, c: (bi, c, jnp.maximum(DIL_SUB * i - 1, 0), 0))
    nat = lambda w: pl.BlockSpec((None, r * step, w), lambda bi, i, c: (bi, i, 0))
    nat_num = pl.BlockSpec((None, DIL_W // LANE, r * step, LANE), lambda bi, i, c: (bi, 0, i, 0))
    args = [q, k, k, v, v]
    in_specs = [cur, prev, cur, prev, cur]
    if not first:
        args += list(state)
        in_specs += [nat_num, nat(LANE), nat(LANE)]
    if final:
        assert r == 1
        out_specs = [nat(DIL_W)]
        out_shape = [jax.ShapeDtypeStruct((b, t, DIL_W), BF16)]
    else:
        out_specs = [nat_num, nat(LANE), nat(LANE)]
        out_shape = [jax.ShapeDtypeStruct((b, DIL_W // LANE, t, LANE), F32),
                     jax.ShapeDtypeStruct((b, t, LANE), F32),
                     jax.ShapeDtypeStruct((b, t, LANE), F32)]
    return pl.pallas_call(
        functools.partial(_dil_branch_kernel, first=first, final=final, dilation=r),
        grid=(b, nq, r),
        in_specs=in_specs,
        out_specs=out_specs,
        out_shape=out_shape,
        compiler_params=_params(("arbitrary", "arbitrary", "arbitrary")),
        name=f"dil_branch_r{r}",
    )(*args)


SAMPLE_HB = 6
SAMPLE_ROWS = 16


def _dil_sample_kernel(q_ref, kn_ref, vn_ref, knt_ref, vnt_ref, kc_ref, vc_ref, o_ref, ko_ref, vo_ref,
                       *, t_new, hist):
    t_row = lax.broadcasted_iota(jnp.int32, (SAMPLE_ROWS, 1), 0)
    j = lax.broadcasted_iota(jnp.int32, (1, hist), 1)
    lane = lax.broadcasted_iota(jnp.int32, (1, LANE), 1)
    keep = lane < LANE - t_new

    def weight(off):
        w = jnp.zeros(off.shape, F32)
        for win, dil in DIL_PAIRS:
            w = w + jnp.where((off >= 0) & (off <= win) & (off % dil == 0), 1.0, 0.0)
        return w

    w_c = weight(hist + t_row - j)
    w_n = [weight(t_row - t) for t in range(t_new)]
    heads = range(SAMPLE_HB)
    hs = [slice(h * HEAD_DIM, (h + 1) * HEAD_DIM) for h in heads]
    qs = [q_ref[:, hs[h]] for h in heads]
    s_c = [jnp.dot(qs[h].astype(BF16), kc_ref[h].astype(BF16), preferred_element_type=F32) for h in heads]
    s_c = [jnp.where(w_c > 0, s, -jnp.inf) for s in s_c]
    s_n = [[jnp.where(w_n[t] > 0, jnp.sum(qs[h] * kn_ref[t:t + 1, hs[h]], axis=-1, keepdims=True), -jnp.inf)
            for t in range(t_new)] for h in heads]
    m = [functools.reduce(jnp.maximum, s_n[h], jnp.max(s_c[h], axis=-1, keepdims=True)) for h in heads]
    p_c = [w_c * jnp.exp(s_c[h] - m[h]) for h in heads]
    p_n = [[w_n[t] * jnp.exp(s_n[h][t] - m[h]) for t in range(t_new)] for h in heads]
    den = [functools.reduce(jnp.add, p_n[h], jnp.sum(p_c[h], axis=-1, keepdims=True)) for h in heads]
    num = [lax.dot_general(p_c[h].astype(BF16), vc_ref[h].astype(BF16), NT_DIMS, preferred_element_type=F32)
           for h in heads]
    num = [functools.reduce(jnp.add, [p_n[h][t] * vn_ref[t:t + 1, hs[h]] for t in range(t_new)], num[h])
           for h in heads]
    o_ref[...] = jnp.concatenate([num[h] / den[h] for h in heads], axis=1)

    for h in heads:
        for src, new_ref, dst in ((kc_ref, knt_ref, ko_ref), (vc_ref, vnt_ref, vo_ref)):
            nxt = pltpu.roll(src[h, :, 0:LANE], LANE - t_new, axis=1)
            for c in range(hist // LANE):
                cur_t = nxt
                if c + 1 < hist // LANE:
                    nxt = pltpu.roll(src[h, :, (c + 1) * LANE:(c + 2) * LANE], LANE - t_new, axis=1)
                else:
                    nxt = new_ref[h]
                dst[h, :, c * LANE:(c + 1) * LANE] = jnp.where(keep, cur_t, nxt)


def _dil_sample(q, kn, vn, knt, vnt, cache_kt, cache_vt, t_new):
    b, nh, _, hist = cache_kt.shape
    w = SAMPLE_HB * HEAD_DIM
    small = pl.BlockSpec((None, SAMPLE_ROWS, w), lambda bi, c: (bi, 0, c))
    newt = pl.BlockSpec((None, SAMPLE_HB, HEAD_DIM, LANE), lambda bi, c: (bi, c, 0, 0))
    big = pl.BlockSpec((None, SAMPLE_HB, HEAD_DIM, hist), lambda bi, c: (bi, c, 0, 0))
    return pl.pallas_call(
        functools.partial(_dil_sample_kernel, t_new=t_new, hist=hist),
        grid=(b, nh // SAMPLE_HB),
        in_specs=[small, small, small, newt, newt, big, big],
        out_specs=[small, big, big],
        out_shape=[jax.ShapeDtypeStruct((b, SAMPLE_ROWS, nh * HEAD_DIM), F32),
                   jax.ShapeDtypeStruct(cache_kt.shape, F32),
                   jax.ShapeDtypeStruct(cache_vt.shape, F32)],
        compiler_params=_params(("arbitrary", "arbitrary")),
        name="dil_sample",
    )(q, kn, vn, knt, vnt, cache_kt, cache_vt)


OUT_ROWS = 256

STRIDE_ROWS = 256
STRIDE_GROUPS = STRIDE_ROWS // SUBLANE


def _to_strided(val, stage_ref):
    for j in range(val.shape[1] // LANE):
        for s_ in range(SUBLANE):
            stage_ref[j, pl.ds(s_, STRIDE_GROUPS, stride=SUBLANE), :] = (
                val[s_ * STRIDE_GROUPS:(s_ + 1) * STRIDE_GROUPS, j * LANE:(j + 1) * LANE])
    return jnp.concatenate([stage_ref[j] for j in range(val.shape[1] // LANE)], axis=1)


def _from_strided(val, stage_ref):
    for j in range(val.shape[1] // LANE):
        stage_ref[j] = val[:, j * LANE:(j + 1) * LANE]
    return jnp.concatenate(
        [jnp.concatenate([stage_ref[j, pl.ds(s_, STRIDE_GROUPS, stride=SUBLANE), :]
                          for j in range(val.shape[1] // LANE)], axis=1) for s_ in range(SUBLANE)], axis=0)


def _out_proj_kernel(oa_ref, ob_ref, w_ref, x_ref, g_ref, h_ref, xn_ref, wbf_ref, *stage, strided):
    @pl.when(pl.program_id(0) == 0)
    def _():
        wbf_ref[...] = w_ref[...].astype(BF16)

    for r0 in range(0, x_ref.shape[0], OUT_ROWS):
        rows = slice(r0, min(r0 + OUT_ROWS, x_ref.shape[0]))
        y = (jnp.dot(oa_ref[rows, :], wbf_ref[:GLA_VW, :], preferred_element_type=F32)
             + jnp.dot(ob_ref[rows, :], wbf_ref[GLA_VW:, :], preferred_element_type=F32))
        h = x_ref[rows, :] + y
        h_ref[rows, :] = h
        ms = jnp.mean(h * h, axis=-1, keepdims=True)
        xn = h * lax.rsqrt(ms + EPS) * g_ref[...]
        if strided:
            xn = _to_strided(xn, stage[0])
        xn_ref[rows, :] = xn.astype(BF16)


def _out_proj(oa, ob, w, x, g, tm, strided=False):
    m, d = x.shape
    assert not strided or OUT_ROWS == STRIDE_ROWS
    row = lambda width: pl.BlockSpec((tm, width), lambda i: (i, 0))
    g2 = g.reshape(1, d)
    scratch = [pltpu.VMEM(w.shape, BF16)]
    if strided:
        scratch.append(pltpu.VMEM((d // LANE, STRIDE_ROWS, LANE), F32))
    return pl.pallas_call(
        functools.partial(_out_proj_kernel, strided=strided),
        grid=(m // tm,),
        in_specs=[row(GLA_VW), row(DIL_W),
                  pl.BlockSpec(w.shape, lambda i: (0, 0), pipeline_mode=pl.Buffered(1)),
                  row(d), pl.BlockSpec(g2.shape, lambda i: (0, 0))],
        out_specs=[row(d), row(d)],
        out_shape=[jax.ShapeDtypeStruct((m, d), F32), jax.ShapeDtypeStruct((m, d), BF16)],
        scratch_shapes=scratch,
        compiler_params=_params(("arbitrary",)),
        name="out_proj",
    )(oa, ob, w, x, g2)


FF_ROWS = STRIDE_ROWS
FF_TAIL = 2 * SUBLANE
FF_SUB = FF_TN // LANE
FF_TILES = D_FF // LANE


def _conv_taps(cw, cb, up, prev1, prev2):
    return cb + cw[0:1] * prev2 + cw[1:2] * prev1 + cw[2:3] * up


def _swiglu(u, g):
    return (g / (1.0 + jnp.exp(-g)) * u).astype(BF16)


def _stage_up_weights(w_blocks, wbf_ref, j):
    for half in range(2):
        for q in range(FF_SUB):
            blk = w_blocks[half * FF_SUB + q][...]
            if (FF_NT - 1) * FF_SUB + q >= FF_TILES:
                blk = jnp.where(j * FF_SUB + q < FF_TILES, blk, 0.0)
            wbf_ref[half, :, q * LANE:(q + 1) * LANE] = blk.astype(BF16)


def _up_weight_specs(tile_of_step):
    specs = []
    for half in range(2):
        for q in range(FF_SUB):
            def index(*ids, half=half, q=q):
                blk = jnp.minimum(tile_of_step(*ids) * FF_SUB + q, FF_TILES - 1)
                return (0, half * FF_TILES + blk)
            specs.append(pl.BlockSpec((D_MODEL, LANE), index))
    return specs


def _ffn_up_seq_kernel(*refs, tiles_per_seq):
    x_ref = refs[0]
    w_blocks = refs[1:1 + 2 * FF_SUB]
    cwu_ref, cwg_ref, cbu_ref, cbg_ref, act_ref, tu_ref, tg_ref, wbf_ref, carry_ref = refs[1 + 2 * FF_SUB:]
    j = pl.program_id(0)
    i = pl.program_id(1)
    tm = x_ref.shape[0]

    @pl.when(i == 0)
    def _():
        _stage_up_weights(w_blocks, wbf_ref, j)

    @pl.when(i % tiles_per_seq == 0)
    def _():
        carry_ref[...] = jnp.zeros_like(carry_ref)

    carry = [carry_ref[0], carry_ref[1]]
    shift = lambda before, cur: pltpu.roll(jnp.concatenate([before, cur], axis=0), 1, axis=0)[SUBLANE:]
    for r0 in range(0, tm, FF_ROWS):
        x = x_ref[r0:r0 + FF_ROWS, :]
        conv = []
        for idx, (cw_ref, cb_ref) in enumerate(((cwu_ref, cbu_ref), (cwg_ref, cbg_ref))):
            up = jnp.dot(x, wbf_ref[idx], preferred_element_type=F32)
            tail = up[FF_ROWS - FF_TAIL:]
            p_a = shift(carry[idx][:SUBLANE], tail[:SUBLANE])
            p_b = shift(carry[idx][SUBLANE:], tail[SUBLANE:])
            prev1 = jnp.concatenate([p_b, up[:FF_ROWS - SUBLANE]], axis=0)
            prev2 = jnp.concatenate([p_a, p_b, up[:FF_ROWS - 2 * SUBLANE]], axis=0)
            carry[idx] = tail
            conv.append(_conv_taps(cw_ref[...], cb_ref[...], up, prev1, prev2))
        act_ref[r0:r0 + FF_ROWS, :] = _swiglu(*conv)
    for idx, tail_ref in enumerate((tu_ref, tg_ref)):
        carry_ref[idx] = carry[idx]
        tail_ref[...] = carry[idx]


def _ffn_up_seq(xn, w_up, conv_w, conv_b, tm, t_seq):
    m, d = xn.shape
    tiles_per_seq = t_seq // tm
    xs = pl.BlockSpec((tm, d), lambda j, i: (i, 0))
    cu = lambda r: pl.BlockSpec((r, FF_TN), lambda j, i: (0, j))
    cg = lambda r: pl.BlockSpec((r, FF_TN), lambda j, i: (0, FF_NT + j))
    act = pl.BlockSpec((tm, FF_TN), lambda j, i: (i, j))
    tail = pl.BlockSpec((None, FF_TAIL, FF_TN), lambda j, i: (i // tiles_per_seq, 0, j))
    w_specs = _up_weight_specs(lambda j, i: j)
    return pl.pallas_call(
        functools.partial(_ffn_up_seq_kernel, tiles_per_seq=tiles_per_seq),
        grid=(FF_NT, m // tm),
        in_specs=[xs] + w_specs + [cu(CONV_W), cg(CONV_W), cu(1), cg(1)],
        out_specs=[act, tail, tail],
        out_shape=[jax.ShapeDtypeStruct((m, D_FF_P), BF16)]
        + [jax.ShapeDtypeStruct((m // t_seq, FF_TAIL, D_FF_P), F32)] * 2,
        scratch_shapes=[pltpu.VMEM((2, d, FF_TN), BF16), pltpu.VMEM((2, FF_TAIL, FF_TN), F32)],
        compiler_params=_params(("arbitrary", "arbitrary")),
        name="ffn_up",
    )(xn, *([w_up] * len(w_specs)), conv_w, conv_w, conv_b, conv_b)


def _ffn_up_hist_kernel(*refs, t_seq):
    x_ref = refs[0]
    w_blocks = refs[1:1 + 2 * FF_SUB]
    (cwu_ref, cwg_ref, cbu_ref, cbg_ref, h1u_ref, h1g_ref, h2u_ref, h2g_ref,
     act_ref, upu_ref, upg_ref, wbf_ref) = refs[1 + 2 * FF_SUB:]
    _stage_up_weights(w_blocks, wbf_ref, pl.program_id(0))
    x = x_ref[...]
    t_in = lax.broadcasted_iota(jnp.int32, (x.shape[0], 1), 0) % t_seq
    conv = []
    for idx, (cw_ref, cb_ref, h1_ref, h2_ref, up_ref) in enumerate((
            (cwu_ref, cbu_ref, h1u_ref, h2u_ref, upu_ref), (cwg_ref, cbg_ref, h1g_ref, h2g_ref, upg_ref))):
        up = jnp.dot(x, wbf_ref[idx], preferred_element_type=F32)
        prev1 = jnp.where(t_in >= 1, pltpu.roll(up, 1, axis=0), 0.0) + h1_ref[...]
        prev2 = jnp.where(t_in >= 2, pltpu.roll(up, 2, axis=0), 0.0) + h2_ref[...]
        up_ref[...] = up
        conv.append(_conv_taps(cw_ref[...], cb_ref[...], up, prev1, prev2))
    act_ref[...] = _swiglu(*conv)


def _ffn_up_hist(xn, w_up, conv_w, conv_b, h1, h2, t_seq):
    m, d = xn.shape
    xs = pl.BlockSpec((m, d), lambda j: (0, 0))
    u = lambda r: pl.BlockSpec((r, FF_TN), lambda j: (0, j))
    g = lambda r: pl.BlockSpec((r, FF_TN), lambda j: (0, FF_NT + j))
    w_specs = _up_weight_specs(lambda j: j)
    return pl.pallas_call(
        functools.partial(_ffn_up_hist_kernel, t_seq=t_seq),
        grid=(FF_NT,),
        in_specs=[xs] + w_specs + [u(CONV_W), g(CONV_W), u(1), g(1), u(m), g(m), u(m), g(m)],
        out_specs=[u(m), u(m), u(m)],
        out_shape=[jax.ShapeDtypeStruct((m, D_FF_P), BF16)] + [jax.ShapeDtypeStruct((m, D_FF_P), F32)] * 2,
        scratch_shapes=[pltpu.VMEM((2, d, FF_TN), BF16)],
        compiler_params=_params(("arbitrary",)),
        name="ffn_up_hist",
    )(xn, *([w_up] * len(w_specs)), conv_w, conv_w, conv_b, conv_b, h1, h1, h2, h2)


DOWN_ROWS = 256


def _ffn_down_kernel(a_ref, w_ref, h_ref, g_ref, y_ref, *stage, strided):
    kdim = w_ref.shape[0]
    for r0 in range(0, a_ref.shape[0], DOWN_ROWS):
        rows = slice(r0, min(r0 + DOWN_ROWS, a_ref.shape[0]))
        f = jnp.dot(a_ref[rows, :kdim], w_ref[...], preferred_element_type=F32)
        if strided:
            f = _from_strided(f, stage[0])
        h = h_ref[rows, :] + f
        ms = jnp.mean(h * h, axis=-1, keepdims=True)
        y_ref[rows, :] = h * lax.rsqrt(ms + EPS) * g_ref[...]


def _ffn_down(act, w, h, g, tm, strided=False):
    m, d = h.shape
    assert not strided or DOWN_ROWS == STRIDE_ROWS
    scratch = [pltpu.VMEM((d // LANE, STRIDE_ROWS, LANE), F32)] if strided else []
    return pl.pallas_call(
        functools.partial(_ffn_down_kernel, strided=strided),
        grid=(m // tm,),
        in_specs=[pl.BlockSpec((tm, act.shape[1]), lambda i: (i, 0)),
                  pl.BlockSpec(w.shape, lambda i: (0, 0), pipeline_mode=pl.Buffered(1)),
                  pl.BlockSpec((tm, d), lambda i: (i, 0)),
                  pl.BlockSpec((1, d), lambda i: (0, 0))],
        out_specs=pl.BlockSpec((tm, d), lambda i: (i, 0)),
        out_shape=jax.ShapeDtypeStruct((m, d), F32),
        scratch_shapes=scratch,
        compiler_params=_params(("arbitrary",)),
        name="ffn_down",
    )(act, w, h, g.reshape(1, d))


def _pad_cols(a, width):
    return jnp.pad(a, ((0, 0), (0, width - a.shape[1])))


def _split_pad_ff(a):
    return jnp.concatenate([_pad_cols(a[:, :D_FF], D_FF_P), _pad_cols(a[:, D_FF:], D_FF_P)], axis=1)


def _unpad_ff(u, g):
    return jnp.concatenate([u[..., :D_FF], g[..., :D_FF]], axis=-1)


def kernel(x_prompt, x_sample, state_gla, cache_dil_k, cache_dil_v, state_ffn_conv, norm_mix, w_in, w_gate_up,
           b_gate, gla_norm, w_out, norm_ffn, w_ffn_up, ffn_conv_w, ffn_conv_b, w_ffn_down, norm_final):
    bp, tp, d = x_prompt.shape
    bs, ts, _ = x_sample.shape
    l = 0

    z0 = 2 * GLA_QK + 2 * GLA_VW
    w_in_t = w_in[l].T
    w_dil_t = w_in_t[z0 + GLA_RANK:]
    w_z_t = jnp.pad(w_in_t[z0:z0 + GLA_RANK], ((0, Z_PAD - GLA_RANK), (0, 0)))
    wg_f = jnp.pad(w_gate_up[l], ((0, Z_PAD - GLA_RANK), (0, 0)))
    wg_hi = wg_f.astype(BF16)
    wg_p = jnp.stack([wg_hi, (wg_f - wg_hi.astype(F32)).astype(BF16)])
    bg = b_gate[l].reshape(1, GLA_QK)
    gn = gla_norm[l].reshape(1, GLA_DV)
    cw_p = _split_pad_ff(ffn_conv_w[l])
    cb_p = _split_pad_ff(ffn_conv_b[l].reshape(1, -1))
    w_dn = w_ffn_down[l].astype(BF16)

    mp = bp * tp
    xp = x_prompt.reshape(mp, d)
    xn_mix_p = _rms(xp, norm_mix[l], 512)
    mix_p = _proj(xn_mix_p, w_in_t, 2048, 768, n=z0)
    dil_p = _proj(xn_mix_p, w_dil_t, 2048, 768)
    z_p = _proj(xn_mix_p, w_z_t, 2048, Z_PAD)
    cos_p, sin_p = _rope_tables(jnp.arange(tp))
    cos_p = jnp.tile(cos_p, (bp, 1))
    sin_p = jnp.tile(sin_p, (bp, 1))
    rope_out = _rope(dil_p, cos_p, sin_p, 512, n_seq=bp)
    by_class = {1: tuple(a.reshape(bp, 1, tp, DIL_W) for a in rope_out[:3])}
    for di, r in enumerate(CLASS_DILATIONS):
        by_class[r] = tuple(rope_out[3 + 3 * di:6 + 3 * di])
    k_kept_t, v_kept_t = rope_out[-2:]

    oa_p, gla_p = _gla(mix_p.reshape(bp, tp, z0), z_p.reshape(bp, tp, Z_PAD), wg_p, bg, gn,
                       jnp.zeros((bp, GLA_HEADS, GLA_DK, GLA_DV), F32),
                       chunk=GLA_ROWS, n_chunks=2, t_valid=tp)

    state = _dil_branch(*by_class[16], None, first=True, final=False)
    state = _dil_branch(*by_class[4], state, first=False, final=False)
    (ob_p,) = _dil_branch(*by_class[1], state, first=False, final=True)

    h_p, xn_p = _out_proj(oa_p.reshape(mp, GLA_VW), ob_p.reshape(mp, DIL_W), w_out[l], xp, norm_ffn[l], 512, strided=True)
    act_p, tail_u, tail_g = _ffn_up_seq(xn_p, w_ffn_up[l], cw_p, cb_p, 2048, tp)
    y_p = _ffn_down(act_p, w_dn, h_p, norm_final, 512, strided=True)

    buf_p = min(DIL_WINDOW, tp)
    y_prompt = y_p.reshape(bp, tp, d)
    new_gla_p = gla_p[None]
    new_k_p = k_kept_t.reshape(bp, DIL_HEADS, HEAD_DIM, buf_p).transpose(0, 3, 1, 2)[None]
    new_v_p = v_kept_t.reshape(bp, DIL_HEADS, HEAD_DIM, buf_p).transpose(0, 3, 1, 2)[None]
    last_two = lambda a: a[:, SUBLANE - 1::SUBLANE]
    new_conv_p = _unpad_ff(last_two(tail_u), last_two(tail_g))[None]

    ms_ = bs * ts
    xs = x_sample.reshape(ms_, d)
    xn_mix_s = _rms(xs, norm_mix[l], ms_)
    mix_s = _proj(xn_mix_s, w_in_t, ms_, 768, n=z0)
    dil_s = _proj(xn_mix_s, w_dil_t, ms_, 768)
    z_s = _proj(xn_mix_s, w_z_t, ms_, Z_PAD)
    cos_s, sin_s = _rope_tables(PAST_LEN + jnp.arange(ts))
    cos_s = jnp.tile(cos_s, (bs, 1))
    sin_s = jnp.tile(sin_s, (bs, 1))
    q_rs, k_rs = _rope(dil_s, cos_s, sin_s, ms_)

    pad_chunk = lambda a: jnp.pad(a.reshape(bs, ts, -1), ((0, 0), (0, GLA_ROWS_SHORT - ts), (0, 0)))
    oa_s, gla_s = _gla(pad_chunk(mix_s), pad_chunk(z_s), wg_p, bg, gn, state_gla[l],
                       chunk=GLA_ROWS_SHORT, n_chunks=1, t_valid=ts)
    oa_s = oa_s[:, :ts].reshape(ms_, GLA_VW)

    v_s = dil_s[:, 2 * DIL_W:]
    pad_rows = lambda a: jnp.pad(a.reshape(bs, ts, DIL_W), ((0, 0), (0, SAMPLE_ROWS - ts), (0, 0)))
    new_cols = lambda a: jnp.pad(a.reshape(bs, ts, DIL_HEADS, HEAD_DIM).transpose(0, 2, 3, 1),
                                 ((0, 0), (0, 0), (0, 0), (LANE - ts, 0)))
    time_minor = lambda a: a.transpose(0, 2, 3, 1)
    ob_s, k_s_t, v_s_t = _dil_sample(pad_rows(q_rs), pad_rows(k_rs), pad_rows(v_s), new_cols(k_rs), new_cols(v_s),
                                     time_minor(cache_dil_k[l]), time_minor(cache_dil_v[l]), ts)
    ob_s = ob_s[:, :ts].astype(BF16)

    h_s, xn_s = _out_proj(oa_s, ob_s.reshape(ms_, DIL_W), w_out[l], xs, norm_ffn[l], ms_)
    conv_hist = _split_pad_ff(state_ffn_conv[l].reshape(bs * (CONV_W - 1), 2 * D_FF))
    conv_hist = conv_hist.reshape(bs, CONV_W - 1, 2 * D_FF_P)
    zero_row = jnp.zeros((bs, 1, 2 * D_FF_P), F32)
    h1 = jnp.concatenate([conv_hist[:, 1:2]] + [zero_row] * (ts - 1), axis=1).reshape(ms_, 2 * D_FF_P)
    h2 = jnp.concatenate([conv_hist[:, 0:1], conv_hist[:, 1:2]] + [zero_row] * (ts - 2),
                         axis=1).reshape(ms_, 2 * D_FF_P)
    act_s, up_u, up_g = _ffn_up_hist(xn_s, w_ffn_up[l], cw_p, cb_p, h1, h2, ts)
    y_s = _ffn_down(act_s, w_dn, h_s, norm_final, ms_)

    y_sample = y_s.reshape(bs, ts, d)
    new_gla_s = gla_s[None]
    new_k_s = k_s_t.transpose(0, 3, 1, 2)[None]
    new_v_s = v_s_t.transpose(0, 3, 1, 2)[None]
    up_full = _unpad_ff(up_u, up_g).reshape(bs, ts, 2 * D_FF)
    new_conv_s = up_full[:, ts - (CONV_W - 1):][None]

    return (y_prompt, y_sample, new_gla_p, new_gla_s, new_k_p, new_k_s, new_v_p, new_v_s,
            new_conv_p, new_conv_s)
```

```python
import functools

import jax
import jax.numpy as jnp
from jax import lax
from jax.experimental import pallas as pl
from jax.experimental.pallas import tpu as pltpu

F32 = jnp.float32
BF16 = jnp.bfloat16

D_MODEL = 2048
HEAD_DIM = 64
GLA_HEADS = 10
GLA_DK = 64
GLA_DV = 128
GLA_RANK = 16
GLA_TAU = 16.0
GLA_ROWS = 128
GLA_ROWS_SHORT = 16
DIL_HEADS = 12
DIL_PAIRS = ((128, 1), (512, 4), (2048, 16))
DIL_WINDOW = 2048
ROPE_THETA = 10000.0
D_FF = 5504
PAST_LEN = 8192
CONV_W = 3
EPS = 1e-6

GLA_QK = GLA_HEADS * GLA_DK
GLA_VW = GLA_HEADS * GLA_DV
DIL_W = DIL_HEADS * HEAD_DIM

LANE = 128
SUBLANE = 8
VMEM_LIMIT = 56 * 1024 * 1024

Z_PAD = LANE
D_FF_P = 5632
FF_TN = 512
FF_NT = D_FF_P // FF_TN

NT_DIMS = (((1,), (1,)), ((), ()))


def _params(sem, vmem=VMEM_LIMIT, flags=None):
    return pltpu.CompilerParams(dimension_semantics=sem, vmem_limit_bytes=vmem, flags=flags)


def _rms_kernel(x_ref, g_ref, o_ref):
    x = x_ref[...]
    ms = jnp.mean(x * x, axis=-1, keepdims=True)
    o_ref[...] = (x * lax.rsqrt(ms + EPS) * g_ref[...]).astype(BF16)


def _rms(x, g, tm):
    m, d = x.shape
    return pl.pallas_call(
        _rms_kernel,
        grid=(m // tm,),
        in_specs=[pl.BlockSpec((tm, d), lambda i: (i, 0)), pl.BlockSpec((1, d), lambda i: (0, 0))],
        out_specs=pl.BlockSpec((tm, d), lambda i: (i, 0)),
        out_shape=jax.ShapeDtypeStruct((m, d), BF16),
        compiler_params=_params(("arbitrary",)),
        name="rms",
    )(x, g.reshape(1, d))


def _proj_kernel(x_ref, w_ref, o_ref, wbf_ref):
    @pl.when(pl.program_id(1) == 0)
    def _():
        wbf_ref[...] = w_ref[...].astype(BF16)

    o_ref[...] = lax.dot_general(x_ref[...], wbf_ref[...], NT_DIMS, preferred_element_type=F32)


def _proj(xn, wt, tm, tn, n=None):
    m, k = xn.shape
    n = wt.shape[0] if n is None else n
    return pl.pallas_call(
        _proj_kernel,
        grid=(n // tn, m // tm),
        in_specs=[pl.BlockSpec((tm, k), lambda j, i: (i, 0)), pl.BlockSpec((tn, k), lambda j, i: (j, 0))],
        out_specs=pl.BlockSpec((tm, tn), lambda j, i: (i, j)),
        out_shape=jax.ShapeDtypeStruct((m, n), F32),
        scratch_shapes=[pltpu.VMEM((tn, k), BF16)],
        compiler_params=_params(("arbitrary", "arbitrary")),
        name="proj",
    )(xn, wt)


CLASS_DILATIONS = tuple(r for _, r in DIL_PAIRS if r > 1)


ROPE_ROWS = 128


def _rope_kernel(q_ref, k_ref, v_ref, cos_ref, sin_ref, *rest, by_class, tiles, first_kept):
    reps = DIL_W // LANE
    lane = lax.broadcasted_iota(jnp.int32, (1, DIL_W), 1)
    first_half = (lane % HEAD_DIM) < (HEAD_DIM // 2)
    tm = q_ref.shape[0]
    if by_class:
        n_class = 3 * len(CLASS_DILATIONS)
        qb_ref, kb_ref, vb_ref = rest[:3]
        class_refs = rest[3:3 + n_class]
        kt_ref, vt_ref, sq_ref, sk_ref, sv_ref = rest[3 + n_class:]
        kept = pl.program_id(0) % tiles >= first_kept
    else:
        qo_ref, ko_ref = rest

    for r0 in range(0, tm, ROPE_ROWS):
        rows = slice(r0, min(r0 + ROPE_ROWS, tm))
        cos = jnp.concatenate([cos_ref[rows, :]] * reps, axis=1)
        sin = jnp.concatenate([sin_ref[rows, :]] * reps, axis=1)

        def rot(x):
            partner = jnp.where(first_half,
                                pltpu.roll(x, DIL_W - HEAD_DIM // 2, axis=1),
                                pltpu.roll(x, HEAD_DIM // 2, axis=1))
            return x * cos + partner * sin

        q = rot(q_ref[rows, :]) * (HEAD_DIM ** -0.5)
        k = rot(k_ref[rows, :])
        if not by_class:
            qo_ref[rows, :] = q
            ko_ref[rows, :] = k
            continue
        v = v_ref[rows, :]
        qb_ref[rows, :] = q.astype(BF16)
        kb_ref[rows, :] = k.astype(BF16)
        vb_ref[rows, :] = v.astype(BF16)

        @pl.when(kept)
        def _():
            kt_ref[:, rows] = k.T
            vt_ref[:, rows] = v.T

        for src, val in ((sq_ref, q), (sk_ref, k), (sv_ref, v)):
            for j in range(reps):
                src[j, rows, :] = val[:, j * LANE:(j + 1) * LANE]
    if not by_class:
        return
    for di, r in enumerate(CLASS_DILATIONS):
        for src, dst in zip((sq_ref, sk_ref, sv_ref), class_refs[3 * di:3 * di + 3]):
            for c in range(r):
                rows = pl.ds(c, tm // r, stride=r)
                dst[c] = jnp.concatenate([src[j, rows, :] for j in range(reps)], axis=1).astype(BF16)


def _rope(proj, cos_t, sin_t, tm, n_seq=None):
    m = proj.shape[0]
    by_class = n_seq is not None
    blk = lambda c: pl.BlockSpec((tm, DIL_W), lambda i, c=c: (i, c))
    tab = pl.BlockSpec((tm, LANE), lambda i: (i, 0))
    out = pl.BlockSpec((tm, DIL_W), lambda i: (i, 0))
    tiles, first_kept, scratch = 1, 0, []
    if by_class:
        t_seq = m // n_seq
        tiles = t_seq // tm
        window = min(DIL_WINDOW, t_seq)
        first_kept = (t_seq - window) // tm
        out_specs = [out, out, out]
        out_shape = [jax.ShapeDtypeStruct((m, DIL_W), BF16)] * 3
        for r in CLASS_DILATIONS:
            spec = pl.BlockSpec((None, r, tm // r, DIL_W), lambda i: (i // tiles, 0, i % tiles, 0))
            out_specs += [spec] * 3
            out_shape += [jax.ShapeDtypeStruct((n_seq, r, t_seq // r, DIL_W), BF16)] * 3
        kept = pl.BlockSpec((None, DIL_W, tm), lambda i: (i // tiles, 0, jnp.maximum(i % tiles - first_kept, 0)))
        out_specs += [kept] * 2
        out_shape += [jax.ShapeDtypeStruct((n_seq, DIL_W, window), F32)] * 2
        scratch = [pltpu.VMEM((DIL_W // LANE, tm, LANE), F32)] * 3
    else:
        out_specs = [out, out]
        out_shape = [jax.ShapeDtypeStruct((m, DIL_W), F32)] * 2
    return pl.pallas_call(
        functools.partial(_rope_kernel, by_class=by_class, tiles=tiles, first_kept=first_kept),
        grid=(m // tm,),
        in_specs=[blk(0), blk(1), blk(2), tab, tab],
        out_specs=out_specs,
        out_shape=out_shape,
        scratch_shapes=scratch,
        compiler_params=_params(("arbitrary",)),
        name="rope",
    )(proj, proj, proj, cos_t, sin_t)


def _rope_tables(pos):
    half = HEAD_DIM // 2
    inv_freq = ROPE_THETA ** (-2.0 * jnp.arange(half, dtype=F32) / HEAD_DIM)
    ang = pos.astype(F32)[:, None] * inv_freq[None, :]
    cos = jnp.cos(ang)
    sin = jnp.sin(ang)
    cos_t = jnp.concatenate([cos, cos, cos, cos], axis=1)
    sin_t = jnp.concatenate([-sin, sin, -sin, sin], axis=1)
    return cos_t, sin_t


GLA_PAIRS = GLA_HEADS // 2


def _gla_kernel(q_ref, k_ref, v_ref, r_ref, z_ref, wg_ref, bg_ref, gn_ref, s0_ref,
                o_ref, sfin_ref, st_ref, *, chunk, n_chunks, t_valid):
    i = pl.program_id(1)
    pairs = range(GLA_PAIRS)

    @pl.when(i == 0)
    def _():
        for p in pairs:
            st_ref[p] = jnp.concatenate([s0_ref[2 * p].T, s0_ref[2 * p + 1].T], axis=1)

    row = lax.broadcasted_iota(jnp.int32, (chunk, chunk), 0)
    col = lax.broadcasted_iota(jnp.int32, (chunk, chunk), 1)
    tril = (row >= col).astype(BF16)
    t_in = lax.broadcasted_iota(jnp.int32, (chunk, 1), 0)
    both = lambda m: jnp.concatenate([m, m], axis=0)
    levels = []
    s_ = chunk
    while s_ >= 2:
        half = s_ // 2
        pivot = (row // s_) * s_ + (half - 1)
        levels.append(dict(
            size=s_,
            upto=(col <= pivot).astype(BF16),
            pair=both((row // s_ == col // s_) & (row % s_ >= half) & (col % s_ < half)),
            second=(t_in % s_) >= half))
        s_ = half
    diag = both(row == col)
    lo = lax.broadcasted_iota(jnp.int32, (1, LANE), 1) < GLA_DK
    mm = functools.partial(jnp.dot, preferred_element_type=F32)
    nt = functools.partial(lax.dot_general, dimension_numbers=NT_DIMS, preferred_element_type=F32)

    def split(a):
        hi = a.astype(BF16)
        return hi, (a - hi.astype(F32)).astype(BF16)

    def stack(a, p):
        t = a[:, p * LANE:(p + 1) * LANE]
        zero = jnp.zeros_like(t)
        return jnp.concatenate([jnp.where(lo, t, zero), jnp.where(lo, zero, t)], axis=0)

    for c in range(n_chunks):
        sl = pl.ds(c * chunk, chunk)
        z_hi, z_lo = split(z_ref[sl, :])
        x = mm(z_hi, wg_ref[0]) + mm(z_hi, wg_ref[1]) + mm(z_lo, wg_ref[0]) + bg_ref[...]
        log_a = -(jnp.maximum(-x, 0.0) + jnp.log1p(jnp.exp(-jnp.abs(x)))) / GLA_TAU
        t_abs = (i * n_chunks + c) * chunk + t_in
        log_a = jnp.where(t_abs < t_valid, log_a, 0.0)
        a_hi, a_lo = split(log_a)
        cum = mm(tril, a_hi) + mm(tril, a_lo)
        last = cum[chunk - 1:chunk, :]
        q = q_ref[sl, :] * (GLA_DK ** -0.5)
        k = k_ref[sl, :]
        q_dec = (q * jnp.exp(cum)).astype(BF16)
        k_end = (k * jnp.exp(last - cum)).astype(BF16)
        a_end = jnp.exp(last)
        heads = range(GLA_HEADS)
        vs = [slice(h * GLA_DV, (h + 1) * GLA_DV) for h in heads]
        v_f = [v_ref[sl, vs[h]] for h in heads]
        vh = [v_f[h].astype(BF16) for h in heads]
        pivot_cum = []
        for lv in levels:
            if lv["size"] >= SUBLANE:
                blocks = cum.reshape(chunk // lv["size"], lv["size"], cum.shape[1])
                pick = blocks[:, lv["size"] // 2 - 1:lv["size"] // 2, :]
                pivot_cum.append(jnp.broadcast_to(pick, blocks.shape).reshape(cum.shape))
            else:
                pivot_cum.append(mm(lv["upto"], a_hi) + mm(lv["upto"], a_lo))
        q_lv = [(q * jnp.exp(jnp.where(lv["second"], cum - pc, 0.0))).astype(BF16)
                for lv, pc in zip(levels, pivot_cum)]
        k_lv = [(k * jnp.exp(jnp.where(lv["second"], 0.0, pc - cum))).astype(BF16)
                for lv, pc in zip(levels, pivot_cum)]
        q_bf, k_bf = q.astype(BF16), k.astype(BF16)
        lanes = [slice(p * LANE, (p + 1) * LANE) for p in pairs]
        parts = [[nt(stack(ql, p), kl[:, lanes[p]]) for p in pairs] for ql, kl in zip(q_lv, k_lv)]
        scores = [jnp.where(diag, nt(stack(q_bf, p), k_bf[:, lanes[p]]), 0.0) for p in pairs]
        for lv, part in zip(levels, parts):
            scores = [jnp.where(lv["pair"], part[p], scores[p]) for p in pairs]
        scores = [scores[p].astype(BF16) for p in pairs]
        st = [st_ref[p] for p in pairs]
        carried = [nt(stack(q_dec, p), st[p].astype(BF16)) for p in pairs]
        o = [mm(scores[h // 2][(h % 2) * chunk:(h % 2 + 1) * chunk], vh[h])
             + carried[h // 2][(h % 2) * chunk:(h % 2 + 1) * chunk] for h in heads]
        for p in pairs:
            ke = k_end[:, lanes[p]]
            zero = jnp.zeros_like(ke)
            st_ref[p] = (st[p] * a_end[:, lanes[p]]
                         + mm(v_f[2 * p].T.astype(BF16), jnp.where(lo, ke, zero))
                         + mm(v_f[2 * p + 1].T.astype(BF16), jnp.where(lo, zero, ke)))
        ms = [jnp.mean(o[h] * o[h], axis=-1, keepdims=True) for h in heads]
        for h in heads:
            on = o[h] * lax.rsqrt(ms[h] + EPS) * gn_ref[...]
            half_r = 0.5 * r_ref[sl, vs[h]]
            o_ref[sl, vs[h]] = (on * (half_r + half_r * jnp.tanh(half_r))).astype(BF16)

    @pl.when(i == pl.num_programs(1) - 1)
    def _():
        for p in pairs:
            st = st_ref[p]
            sfin_ref[2 * p] = st[:, :GLA_DK].T
            sfin_ref[2 * p + 1] = st[:, GLA_DK:].T


def _gla(src, zsrc, wg, bg, gn, s0, *, chunk, n_chunks, t_valid):
    b, t, _ = src.shape
    tb = chunk * n_chunks
    col = lambda w, c: pl.BlockSpec((None, tb, w), lambda bi, i, c=c: (bi, i, c))
    full = lambda shape: pl.BlockSpec(shape, lambda bi, i: (0,) * len(shape))
    state = pl.BlockSpec((None, GLA_HEADS, GLA_DK, GLA_DV), lambda bi, i: (bi, 0, 0, 0))
    kern = functools.partial(_gla_kernel, chunk=chunk, n_chunks=n_chunks, t_valid=t_valid)
    return pl.pallas_call(
        kern,
        grid=(b, t // tb),
        in_specs=[col(GLA_QK, 0), col(GLA_QK, 1), col(GLA_VW, 1), col(GLA_VW, 2), col(Z_PAD, 0),
                  full((2, Z_PAD, GLA_QK)), full((1, GLA_QK)), full((1, GLA_DV)), state],
        out_specs=[pl.BlockSpec((None, tb, GLA_VW), lambda bi, i: (bi, i, 0)), state],
        out_shape=[jax.ShapeDtypeStruct((b, t, GLA_VW), BF16),
                   jax.ShapeDtypeStruct((b, GLA_HEADS, GLA_DK, GLA_DV), F32)],
        scratch_shapes=[pltpu.VMEM((GLA_PAIRS, GLA_DV, 2 * GLA_DK), F32)],
        compiler_params=_params(("arbitrary", "arbitrary")),
        name="gla",
    )(src, src, src, src, zsrc, wg, bg, gn, s0)


QB = 128
DIL_SUB = 2


def _dil_branch_kernel(*refs, first, final, dilation):
    if first:
        q_ref, kp_ref, kc_ref, vp_ref, vc_ref = refs[:5]
        outs = refs[5:]
    else:
        q_ref, kp_ref, kc_ref, vp_ref, vc_ref, num_ref, m_ref, d_ref = refs[:8]
        outs = refs[8:]
    i = pl.program_id(1)
    c = pl.program_id(2)
    qi = QB + lax.broadcasted_iota(jnp.int32, (QB, 1), 0)
    kj = lax.broadcasted_iota(jnp.int32, (1, 2 * QB), 1)
    off = qi - kj
    in_window = (off >= 0) & (off <= QB)
    lane = lax.broadcasted_iota(jnp.int32, (1, LANE), 1)
    lo = lane < HEAD_DIM
    heads = range(DIL_HEADS)
    cs = [slice(p * LANE, (p + 1) * LANE) for p in range(DIL_HEADS // 2)]
    sel = [lo if h % 2 == 0 else jnp.logical_not(lo) for h in heads]
    for u in range(DIL_SUB):
        blk = slice(u * QB, (u + 1) * QB)
        rows = pl.ds(dilation * u * QB + c, QB, stride=dilation) if dilation > 1 else blk
        if u == 0:
            k_before, v_before = kp_ref[...], vp_ref[...]
            valid = in_window & ((kj >= QB) | (i > 0))
        else:
            before = slice((u - 1) * QB, u * QB)
            k_before, v_before = kc_ref[before, :], vc_ref[before, :]
            valid = in_window
        k2 = jnp.concatenate([k_before, kc_ref[blk, :]], axis=0)
        v2 = jnp.concatenate([v_before, vc_ref[blk, :]], axis=0)
        q = q_ref[blk, :]
        qm = [jnp.where(sel[h], q[:, cs[h // 2]], jnp.zeros((QB, LANE), BF16)) for h in heads]
        s = [lax.dot_general(qm[h], k2[:, cs[h // 2]], NT_DIMS, preferred_element_type=F32) for h in heads]
        s = [jnp.where(valid, s[h], -jnp.inf) for h in heads]
        m_new = [jnp.max(s[h], axis=-1, keepdims=True) for h in heads]
        if not first:
            m_in = m_ref[rows, :]
            d_in = d_ref[rows, :]
            m_old = [jnp.sum(jnp.where(lane == h, m_in, 0.0), axis=-1, keepdims=True) for h in heads]
            d_old = [jnp.sum(jnp.where(lane == h, d_in, 0.0), axis=-1, keepdims=True) for h in heads]
            m_new = [jnp.maximum(m_new[h], m_old[h]) for h in heads]
        pr = [jnp.exp(s[h] - m_new[h]) for h in heads]
        d_new = [jnp.sum(pr[h], axis=-1, keepdims=True) for h in heads]
        if not first:
            alpha = [jnp.exp(m_old[h] - m_new[h]) for h in heads]
            d_new = [d_new[h] + d_old[h] * alpha[h] for h in heads]
        pv = [jnp.dot(pr[h].astype(BF16), v2[:, cs[h // 2]], preferred_element_type=F32) for h in heads]
        pairs = []
        for p in range(DIL_HEADS // 2):
            num = jnp.where(lo, pv[2 * p], pv[2 * p + 1])
            if not first:
                num = num + num_ref[p, rows, :] * jnp.where(lo, alpha[2 * p], alpha[2 * p + 1])
            if final:
                num = num / jnp.where(lo, d_new[2 * p], d_new[2 * p + 1])
            else:
                outs[0][p, rows, :] = num
            pairs.append(num)
        if final:
            outs[0][rows, :] = jnp.concatenate(pairs, axis=1).astype(BF16)
        else:
            outs[1][rows, :] = functools.reduce(jnp.add, [jnp.where(lane == h, m_new[h], 0.0) for h in heads])
            outs[2][rows, :] = functools.reduce(jnp.add, [jnp.where(lane == h, d_new[h], 0.0) for h in heads])


def _dil_branch(q, k, v, state, *, first, final):
    b, r, ts, _ = q.shape
    t = r * ts
    step = DIL_SUB * QB
    nq = ts // step
    cur = pl.BlockSpec((None, None, step, DIL_W), lambda bi, i, c: (bi, c, i, 0))
    prev = pl.BlockSpec((None, None, QB, DIL_W), lambda bi, i, c: (bi, c, jnp.maximum(DIL_SUB * i - 1, 0), 0))
    nat = lambda w: pl.BlockSpec((None, r * step, w), lambda bi, i, c: (bi, i, 0))
    nat_num = pl.BlockSpec((None, DIL_W // LANE, r * step, LANE), lambda bi, i, c: (bi, 0, i, 0))
    args = [q, k, k, v, v]
    in_specs = [cur, prev, cur, prev, cur]
    if not first:
        args += list(state)
        in_specs += [nat_num, nat(LANE), nat(LANE)]
    if final:
        assert r == 1
        out_specs = [nat(DIL_W)]
        out_shape = [jax.ShapeDtypeStruct((b, t, DIL_W), BF16)]
    else:
        out_specs = [nat_num, nat(LANE), nat(LANE)]
        out_shape = [jax.ShapeDtypeStruct((b, DIL_W // LANE, t, LANE), F32),
                     jax.ShapeDtypeStruct((b, t, LANE), F32),
                     jax.ShapeDtypeStruct((b, t, LANE), F32)]
    return pl.pallas_call(
        functools.partial(_dil_branch_kernel, first=first, final=final, dilation=r),
        grid=(b, nq, r),
        in_specs=in_specs,
        out_specs=out_specs,
        out_shape=out_shape,
        compiler_params=_params(("arbitrary", "arbitrary", "arbitrary")),
        name=f"dil_branch_r{r}",
    )(*args)


SAMPLE_HB = 6
SAMPLE_ROWS = 16


def _dil_sample_kernel(q_ref, kn_ref, vn_ref, knt_ref, vnt_ref, kc_ref, vc_ref, o_ref, ko_ref, vo_ref,
                       *, t_new, hist):
    t_row = lax.broadcasted_iota(jnp.int32, (SAMPLE_ROWS, 1), 0)
    j = lax.broadcasted_iota(jnp.int32, (1, hist), 1)
    lane = lax.broadcasted_iota(jnp.int32, (1, LANE), 1)
    keep = lane < LANE - t_new

    def weight(off):
        w = jnp.zeros(off.shape, F32)
        for win, dil in DIL_PAIRS:
            w = w + jnp.where((off >= 0) & (off <= win) & (off % dil == 0), 1.0, 0.0)
        return w

    w_c = weight(hist + t_row - j)
    w_n = [weight(t_row - t) for t in range(t_new)]
    heads = range(SAMPLE_HB)
    hs = [slice(h * HEAD_DIM, (h + 1) * HEAD_DIM) for h in heads]
    qs = [q_ref[:, hs[h]] for h in heads]
    s_c = [jnp.dot(qs[h].astype(BF16), kc_ref[h].astype(BF16), preferred_element_type=F32) for h in heads]
    s_c = [jnp.where(w_c > 0, s, -jnp.inf) for s in s_c]
    s_n = [[jnp.where(w_n[t] > 0, jnp.sum(qs[h] * kn_ref[t:t + 1, hs[h]], axis=-1, keepdims=True), -jnp.inf)
            for t in range(t_new)] for h in heads]
    m = [functools.reduce(jnp.maximum, s_n[h], jnp.max(s_c[h], axis=-1, keepdims=True)) for h in heads]
    p_c = [w_c * jnp.exp(s_c[h] - m[h]) for h in heads]
    p_n = [[w_n[t] * jnp.exp(s_n[h][t] - m[h]) for t in range(t_new)] for h in heads]
    den = [functools.reduce(jnp.add, p_n[h], jnp.sum(p_c[h], axis=-1, keepdims=True)) for h in heads]
    num = [lax.dot_general(p_c[h].astype(BF16), vc_ref[h].astype(BF16), NT_DIMS, preferred_element_type=F32)
           for h in heads]
    num = [functools.reduce(jnp.add, [p_n[h][t] * vn_ref[t:t + 1, hs[h]] for t in range(t_new)], num[h])
           for h in heads]
    o_ref[...] = jnp.concatenate([num[h] / den[h] for h in heads], axis=1)

    for h in heads:
        for src, new_ref, dst in ((kc_ref, knt_ref, ko_ref), (vc_ref, vnt_ref, vo_ref)):
            nxt = pltpu.roll(src[h, :, 0:LANE], LANE - t_new, axis=1)
            for c in range(hist // LANE):
                cur_t = nxt
                if c + 1 < hist // LANE:
                    nxt = pltpu.roll(src[h, :, (c + 1) * LANE:(c + 2) * LANE], LANE - t_new, axis=1)
                else:
                    nxt = new_ref[h]
                dst[h, :, c * LANE:(c + 1) * LANE] = jnp.where(keep, cur_t, nxt)


def _dil_sample(q, kn, vn, knt, vnt, cache_kt, cache_vt, t_new):
    b, nh, _, hist = cache_kt.shape
    w = SAMPLE_HB * HEAD_DIM
    small = pl.BlockSpec((None, SAMPLE_ROWS, w), lambda bi, c: (bi, 0, c))
    newt = pl.BlockSpec((None, SAMPLE_HB, HEAD_DIM, LANE), lambda bi, c: (bi, c, 0, 0))
    big = pl.BlockSpec((None, SAMPLE_HB, HEAD_DIM, hist), lambda bi, c: (bi, c, 0, 0))
    return pl.pallas_call(
        functools.partial(_dil_sample_kernel, t_new=t_new, hist=hist),
        grid=(b, nh // SAMPLE_HB),
        in_specs=[small, small, small, newt, newt, big, big],
        out_specs=[small, big, big],
        out_shape=[jax.ShapeDtypeStruct((b, SAMPLE_ROWS, nh * HEAD_DIM), F32),
                   jax.ShapeDtypeStruct(cache_kt.shape, F32),
                   jax.ShapeDtypeStruct(cache_vt.shape, F32)],
        compiler_params=_params(("arbitrary", "arbitrary")),
        name="dil_sample",
    )(q, kn, vn, knt, vnt, cache_kt, cache_vt)


OUT_ROWS = 256

STRIDE_ROWS = 256
STRIDE_GROUPS = STRIDE_ROWS // SUBLANE


def _to_strided(val, stage_ref):
    for j in range(val.shape[1] // LANE):
        for s_ in range(SUBLANE):
            stage_ref[j, pl.ds(s_, STRIDE_GROUPS, stride=SUBLANE), :] = (
                val[s_ * STRIDE_GROUPS:(s_ + 1) * STRIDE_GROUPS, j * LANE:(j + 1) * LANE])
    return jnp.concatenate([stage_ref[j] for j in range(val.shape[1] // LANE)], axis=1)


def _from_strided(val, stage_ref):
    for j in range(val.shape[1] // LANE):
        stage_ref[j] = val[:, j * LANE:(j + 1) * LANE]
    return jnp.concatenate(
        [jnp.concatenate([stage_ref[j, pl.ds(s_, STRIDE_GROUPS, stride=SUBLANE), :]
                          for j in range(val.shape[1] // LANE)], axis=1) for s_ in range(SUBLANE)], axis=0)


def _out_proj_kernel(oa_ref, ob_ref, w_ref, x_ref, g_ref, h_ref, xn_ref, wbf_ref, *stage, strided):
    @pl.when(pl.program_id(0) == 0)
    def _():
        wbf_ref[...] = w_ref[...].astype(BF16)

    for r0 in range(0, x_ref.shape[0], OUT_ROWS):
        rows = slice(r0, min(r0 + OUT_ROWS, x_ref.shape[0]))
        y = (jnp.dot(oa_ref[rows, :], wbf_ref[:GLA_VW, :], preferred_element_type=F32)
             + jnp.dot(ob_ref[rows, :], wbf_ref[GLA_VW:, :], preferred_element_type=F32))
        h = x_ref[rows, :] + y
        h_ref[rows, :] = h
        ms = jnp.mean(h * h, axis=-1, keepdims=True)
        xn = h * lax.rsqrt(ms + EPS) * g_ref[...]
        if strided:
            xn = _to_strided(xn, stage[0])
        xn_ref[rows, :] = xn.astype(BF16)


def _out_proj(oa, ob, w, x, g, tm, strided=False):
    m, d = x.shape
    assert not strided or OUT_ROWS == STRIDE_ROWS
    row = lambda width: pl.BlockSpec((tm, width), lambda i: (i, 0))
    g2 = g.reshape(1, d)
    scratch = [pltpu.VMEM(w.shape, BF16)]
    if strided:
        scratch.append(pltpu.VMEM((d // LANE, STRIDE_ROWS, LANE), F32))
    return pl.pallas_call(
        functools.partial(_out_proj_kernel, strided=strided),
        grid=(m // tm,),
        in_specs=[row(GLA_VW), row(DIL_W),
                  pl.BlockSpec(w.shape, lambda i: (0, 0), pipeline_mode=pl.Buffered(1)),
                  row(d), pl.BlockSpec(g2.shape, lambda i: (0, 0))],
        out_specs=[row(d), row(d)],
        out_shape=[jax.ShapeDtypeStruct((m, d), F32), jax.ShapeDtypeStruct((m, d), BF16)],
        scratch_shapes=scratch,
        compiler_params=_params(("arbitrary",)),
        name="out_proj",
    )(oa, ob, w, x, g2)


FF_ROWS = STRIDE_ROWS
FF_TAIL = 2 * SUBLANE
FF_SUB = FF_TN // LANE
FF_TILES = D_FF // LANE


def _conv_taps(cw, cb, up, prev1, prev2):
    return cb + cw[0:1] * prev2 + cw[1:2] * prev1 + cw[2:3] * up


def _swiglu(u, g):
    h = 0.5 * g
    return ((h + h * jnp.tanh(h)) * u).astype(BF16)


def _stage_up_weights(w_blocks, wbf_ref, j):
    for half in range(2):
        for q in range(FF_SUB):
            blk = w_blocks[half * FF_SUB + q][...]
            if (FF_NT - 1) * FF_SUB + q >= FF_TILES:
                blk = jnp.where(j * FF_SUB + q < FF_TILES, blk, 0.0)
            wbf_ref[half, :, q * LANE:(q + 1) * LANE] = blk.astype(BF16)


def _up_weight_specs(tile_of_step):
    specs = []
    for half in range(2):
        for q in range(FF_SUB):
            def index(*ids, half=half, q=q):
                blk = jnp.minimum(tile_of_step(*ids) * FF_SUB + q, FF_TILES - 1)
                return (0, half * FF_TILES + blk)
            specs.append(pl.BlockSpec((D_MODEL, LANE), index))
    return specs


def _ffn_up_seq_kernel(*refs, tiles_per_seq):
    x_ref = refs[0]
    w_blocks = refs[1:1 + 2 * FF_SUB]
    cwu_ref, cwg_ref, cbu_ref, cbg_ref, act_ref, tu_ref, tg_ref, wbf_ref, carry_ref = refs[1 + 2 * FF_SUB:]
    j = pl.program_id(0)
    i = pl.program_id(1)
    tm = x_ref.shape[0]

    @pl.when(i == 0)
    def _():
        _stage_up_weights(w_blocks, wbf_ref, j)

    @pl.when(i % tiles_per_seq == 0)
    def _():
        carry_ref[...] = jnp.zeros_like(carry_ref)

    carry = [carry_ref[0], carry_ref[1]]
    shift = lambda before, cur: pltpu.roll(jnp.concatenate([before, cur], axis=0), 1, axis=0)[SUBLANE:]
    for r0 in range(0, tm, FF_ROWS):
        x = x_ref[r0:r0 + FF_ROWS, :]
        conv = []
        for idx, (cw_ref, cb_ref) in enumerate(((cwu_ref, cbu_ref), (cwg_ref, cbg_ref))):
            up = jnp.dot(x, wbf_ref[idx], preferred_element_type=F32)
            tail = up[FF_ROWS - FF_TAIL:]
            p_a = shift(carry[idx][:SUBLANE], tail[:SUBLANE])
            p_b = shift(carry[idx][SUBLANE:], tail[SUBLANE:])
            prev1 = jnp.concatenate([p_b, up[:FF_ROWS - SUBLANE]], axis=0)
            prev2 = jnp.concatenate([p_a, p_b, up[:FF_ROWS - 2 * SUBLANE]], axis=0)
            carry[idx] = tail
            conv.append(_conv_taps(cw_ref[...], cb_ref[...], up, prev1, prev2))
        act_ref[r0:r0 + FF_ROWS, :] = _swiglu(*conv)
    for idx, tail_ref in enumerate((tu_ref, tg_ref)):
        carry_ref[idx] = carry[idx]
        tail_ref[...] = carry[idx]


def _ffn_up_seq(xn, w_up, conv_w, conv_b, tm, t_seq):
    m, d = xn.shape
    tiles_per_seq = t_seq // tm
    xs = pl.BlockSpec((tm, d), lambda j, i: (i, 0))
    cu = lambda r: pl.BlockSpec((r, FF_TN), lambda j, i: (0, j))
    cg = lambda r: pl.BlockSpec((r, FF_TN), lambda j, i: (0, FF_NT + j))
    act = pl.BlockSpec((tm, FF_TN), lambda j, i: (i, j))
    tail = pl.BlockSpec((None, FF_TAIL, FF_TN), lambda j, i: (i // tiles_per_seq, 0, j))
    w_specs = _up_weight_specs(lambda j, i: j)
    return pl.pallas_call(
        functools.partial(_ffn_up_seq_kernel, tiles_per_seq=tiles_per_seq),
        grid=(FF_NT, m // tm),
        in_specs=[xs] + w_specs + [cu(CONV_W), cg(CONV_W), cu(1), cg(1)],
        out_specs=[act, tail, tail],
        out_shape=[jax.ShapeDtypeStruct((m, D_FF_P), BF16)]
        + [jax.ShapeDtypeStruct((m // t_seq, FF_TAIL, D_FF_P), F32)] * 2,
        scratch_shapes=[pltpu.VMEM((2, d, FF_TN), BF16), pltpu.VMEM((2, FF_TAIL, FF_TN), F32)],
        compiler_params=_params(("arbitrary", "arbitrary")),
        name="ffn_up",
    )(xn, *([w_up] * len(w_specs)), conv_w, conv_w, conv_b, conv_b)


def _ffn_up_hist_kernel(*refs, t_seq):
    x_ref = refs[0]
    w_blocks = refs[1:1 + 2 * FF_SUB]
    (cwu_ref, cwg_ref, cbu_ref, cbg_ref, h1u_ref, h1g_ref, h2u_ref, h2g_ref,
     act_ref, upu_ref, upg_ref, wbf_ref) = refs[1 + 2 * FF_SUB:]
    _stage_up_weights(w_blocks, wbf_ref, pl.program_id(0))
    x = x_ref[...]
    t_in = lax.broadcasted_iota(jnp.int32, (x.shape[0], 1), 0) % t_seq
    conv = []
    for idx, (cw_ref, cb_ref, h1_ref, h2_ref, up_ref) in enumerate((
            (cwu_ref, cbu_ref, h1u_ref, h2u_ref, upu_ref), (cwg_ref, cbg_ref, h1g_ref, h2g_ref, upg_ref))):
        up = jnp.dot(x, wbf_ref[idx], preferred_element_type=F32)
        prev1 = jnp.where(t_in >= 1, pltpu.roll(up, 1, axis=0), 0.0) + h1_ref[...]
        prev2 = jnp.where(t_in >= 2, pltpu.roll(up, 2, axis=0), 0.0) + h2_ref[...]
        up_ref[...] = up
        conv.append(_conv_taps(cw_ref[...], cb_ref[...], up, prev1, prev2))
    act_ref[...] = _swiglu(*conv)


def _ffn_up_hist(xn, w_up, conv_w, conv_b, h1, h2, t_seq):
    m, d = xn.shape
    xs = pl.BlockSpec((m, d), lambda j: (0, 0))
    u = lambda r: pl.BlockSpec((r, FF_TN), lambda j: (0, j))
    g = lambda r: pl.BlockSpec((r, FF_TN), lambda j: (0, FF_NT + j))
    w_specs = _up_weight_specs(lambda j: j)
    return pl.pallas_call(
        functools.partial(_ffn_up_hist_kernel, t_seq=t_seq),
        grid=(FF_NT,),
        in_specs=[xs] + w_specs + [u(CONV_W), g(CONV_W), u(1), g(1), u(m), g(m), u(m), g(m)],
        out_specs=[u(m), u(m), u(m)],
        out_shape=[jax.ShapeDtypeStruct((m, D_FF_P), BF16)] + [jax.ShapeDtypeStruct((m, D_FF_P), F32)] * 2,
        scratch_shapes=[pltpu.VMEM((2, d, FF_TN), BF16)],
        compiler_params=_params(("arbitrary",)),
        name="ffn_up_hist",
    )(xn, *([w_up] * len(w_specs)), conv_w, conv_w, conv_b, conv_b, h1, h1, h2, h2)


DOWN_ROWS = 256


def _ffn_down_kernel(a_ref, w_ref, h_ref, g_ref, y_ref, *stage, strided):
    kdim = w_ref.shape[0]
    for r0 in range(0, a_ref.shape[0], DOWN_ROWS):
        rows = slice(r0, min(r0 + DOWN_ROWS, a_ref.shape[0]))
        f = jnp.dot(a_ref[rows, :kdim], w_ref[...], preferred_element_type=F32)
        if strided:
            f = _from_strided(f, stage[0])
        h = h_ref[rows, :] + f
        ms = jnp.mean(h * h, axis=-1, keepdims=True)
        y_ref[rows, :] = h * lax.rsqrt(ms + EPS) * g_ref[...]


def _ffn_down(act, w, h, g, tm, strided=False):
    m, d = h.shape
    assert not strided or DOWN_ROWS == STRIDE_ROWS
    scratch = [pltpu.VMEM((d // LANE, STRIDE_ROWS, LANE), F32)] if strided else []
    return pl.pallas_call(
        functools.partial(_ffn_down_kernel, strided=strided),
        grid=(m // tm,),
        in_specs=[pl.BlockSpec((tm, act.shape[1]), lambda i: (i, 0)),
                  pl.BlockSpec(w.shape, lambda i: (0, 0), pipeline_mode=pl.Buffered(1)),
                  pl.BlockSpec((tm, d), lambda i: (i, 0)),
                  pl.BlockSpec((1, d), lambda i: (0, 0))],
        out_specs=pl.BlockSpec((tm, d), lambda i: (i, 0)),
        out_shape=jax.ShapeDtypeStruct((m, d), F32),
        scratch_shapes=scratch,
        compiler_params=_params(("arbitrary",)),
        name="ffn_down",
    )(act, w, h, g.reshape(1, d))


def _pad_cols(a, width):
    return jnp.pad(a, ((0, 0), (0, width - a.shape[1])))


def _split_pad_ff(a):
    return jnp.concatenate([_pad_cols(a[:, :D_FF], D_FF_P), _pad_cols(a[:, D_FF:], D_FF_P)], axis=1)


def _unpad_ff(u, g):
    return jnp.concatenate([u[..., :D_FF], g[..., :D_FF]], axis=-1)


def kernel(x_prompt, x_sample, state_gla, cache_dil_k, cache_dil_v, state_ffn_conv, norm_mix, w_in, w_gate_up,
           b_gate, gla_norm, w_out, norm_ffn, w_ffn_up, ffn_conv_w, ffn_conv_b, w_ffn_down, norm_final):
    bp, tp, d = x_prompt.shape
    bs, ts, _ = x_sample.shape
    l = 0

    z0 = 2 * GLA_QK + 2 * GLA_VW
    w_in_t = w_in[l].T
    w_dil_t = w_in_t[z0 + GLA_RANK:]
    w_z_t = jnp.pad(w_in_t[z0:z0 + GLA_RANK], ((0, Z_PAD - GLA_RANK), (0, 0)))
    wg_f = jnp.pad(w_gate_up[l], ((0, Z_PAD - GLA_RANK), (0, 0)))
    wg_hi = wg_f.astype(BF16)
    wg_p = jnp.stack([wg_hi, (wg_f - wg_hi.astype(F32)).astype(BF16)])
    bg = b_gate[l].reshape(1, GLA_QK)
    gn = gla_norm[l].reshape(1, GLA_DV)
    cw_p = _split_pad_ff(ffn_conv_w[l])
    cb_p = _split_pad_ff(ffn_conv_b[l].reshape(1, -1))
    w_dn = w_ffn_down[l].astype(BF16)

    mp = bp * tp
    xp = x_prompt.reshape(mp, d)
    xn_mix_p = _rms(xp, norm_mix[l], 512)
    mix_p = _proj(xn_mix_p, w_in_t, 2048, 768, n=z0)
    dil_p = _proj(xn_mix_p, w_dil_t, 2048, 768)
    z_p = _proj(xn_mix_p, w_z_t, 2048, Z_PAD)
    cos_p, sin_p = _rope_tables(jnp.arange(tp))
    cos_p = jnp.tile(cos_p, (bp, 1))
    sin_p = jnp.tile(sin_p, (bp, 1))
    rope_out = _rope(dil_p, cos_p, sin_p, 512, n_seq=bp)
    by_class = {1: tuple(a.reshape(bp, 1, tp, DIL_W) for a in rope_out[:3])}
    for di, r in enumerate(CLASS_DILATIONS):
        by_class[r] = tuple(rope_out[3 + 3 * di:6 + 3 * di])
    k_kept_t, v_kept_t = rope_out[-2:]

    oa_p, gla_p = _gla(mix_p.reshape(bp, tp, z0), z_p.reshape(bp, tp, Z_PAD), wg_p, bg, gn,
                       jnp.zeros((bp, GLA_HEADS, GLA_DK, GLA_DV), F32),
                       chunk=GLA_ROWS, n_chunks=2, t_valid=tp)

    state = _dil_branch(*by_class[16], None, first=True, final=False)
    state = _dil_branch(*by_class[4], state, first=False, final=False)
    (ob_p,) = _dil_branch(*by_class[1], state, first=False, final=True)

    h_p, xn_p = _out_proj(oa_p.reshape(mp, GLA_VW), ob_p.reshape(mp, DIL_W), w_out[l], xp, norm_ffn[l], 512, strided=True)
    act_p, tail_u, tail_g = _ffn_up_seq(xn_p, w_ffn_up[l], cw_p, cb_p, 2048, tp)
    y_p = _ffn_down(act_p, w_dn, h_p, norm_final, 512, strided=True)

    buf_p = min(DIL_WINDOW, tp)
    y_prompt = y_p.reshape(bp, tp, d)
    new_gla_p = gla_p[None]
    new_k_p = k_kept_t.reshape(bp, DIL_HEADS, HEAD_DIM, buf_p).transpose(0, 3, 1, 2)[None]
    new_v_p = v_kept_t.reshape(bp, DIL_HEADS, HEAD_DIM, buf_p).transpose(0, 3, 1, 2)[None]
    last_two = lambda a: a[:, SUBLANE - 1::SUBLANE]
    new_conv_p = _unpad_ff(last_two(tail_u), last_two(tail_g))[None]

    ms_ = bs * ts
    xs = x_sample.reshape(ms_, d)
    xn_mix_s = _rms(xs, norm_mix[l], ms_)
    mix_s = _proj(xn_mix_s, w_in_t, ms_, 768, n=z0)
    dil_s = _proj(xn_mix_s, w_dil_t, ms_, 768)
    z_s = _proj(xn_mix_s, w_z_t, ms_, Z_PAD)
    cos_s, sin_s = _rope_tables(PAST_LEN + jnp.arange(ts))
    cos_s = jnp.tile(cos_s, (bs, 1))
    sin_s = jnp.tile(sin_s, (bs, 1))
    q_rs, k_rs = _rope(dil_s, cos_s, sin_s, ms_)

    pad_chunk = lambda a: jnp.pad(a.reshape(bs, ts, -1), ((0, 0), (0, GLA_ROWS_SHORT - ts), (0, 0)))
    oa_s, gla_s = _gla(pad_chunk(mix_s), pad_chunk(z_s), wg_p, bg, gn, state_gla[l],
                       chunk=GLA_ROWS_SHORT, n_chunks=1, t_valid=ts)
    oa_s = oa_s[:, :ts].reshape(ms_, GLA_VW)

    v_s = dil_s[:, 2 * DIL_W:]
    pad_rows = lambda a: jnp.pad(a.reshape(bs, ts, DIL_W), ((0, 0), (0, SAMPLE_ROWS - ts), (0, 0)))
    new_cols = lambda a: jnp.pad(a.reshape(bs, ts, DIL_HEADS, HEAD_DIM).transpose(0, 2, 3, 1),
                                 ((0, 0), (0, 0), (0, 0), (LANE - ts, 0)))
    time_minor = lambda a: a.transpose(0, 2, 3, 1)
    ob_s, k_s_t, v_s_t = _dil_sample(pad_rows(q_rs), pad_rows(k_rs), pad_rows(v_s), new_cols(k_rs), new_cols(v_s),
                                     time_minor(cache_dil_k[l]), time_minor(cache_dil_v[l]), ts)
    ob_s = ob_s[:, :ts].astype(BF16)

    h_s, xn_s = _out_proj(oa_s, ob_s.reshape(ms_, DIL_W), w_out[l], xs, norm_ffn[l], ms_)
    conv_hist = _split_pad_ff(state_ffn_conv[l].reshape(bs * (CONV_W - 1), 2 * D_FF))
    conv_hist = conv_hist.reshape(bs, CONV_W - 1, 2 * D_FF_P)
    zero_row = jnp.zeros((bs, 1, 2 * D_FF_P), F32)
    h1 = jnp.concatenate([conv_hist[:, 1:2]] + [zero_row] * (ts - 1), axis=1).reshape(ms_, 2 * D_FF_P)
    h2 = jnp.concatenate([conv_hist[:, 0:1], conv_hist[:, 1:2]] + [zero_row] * (ts - 2),
                         axis=1).reshape(ms_, 2 * D_FF_P)
    act_s, up_u, up_g = _ffn_up_hist(xn_s, w_ffn_up[l], cw_p, cb_p, h1, h2, ts)
    y_s = _ffn_down(act_s, w_dn, h_s, norm_final, ms_)

    y_sample = y_s.reshape(bs, ts, d)
    new_gla_s = gla_s[None]
    new_k_s = k_s_t.transpose(0, 3, 1, 2)[None]
    new_v_s = v_s_t.transpose(0, 3, 1, 2)[None]
    up_full = _unpad_ff(up_u, up_g).reshape(bs, ts, 2 * D_FF)
    new_conv_s = up_full[:, ts - (CONV_W - 1):][None]

    return (y_prompt, y_sample, new_gla_p, new_gla_s, new_k_p, new_k_s, new_v_p, new_v_s,
            new_conv_p, new_conv_s)
```

```python
import functools

import jax
import jax.numpy as jnp
from jax import lax
from jax.experimental import pallas as pl
from jax.experimental.pallas import tpu as pltpu

F32 = jnp.float32
BF16 = jnp.bfloat16

D_MODEL = 2048
HEAD_DIM = 64
GLA_HEADS = 10
GLA_DK = 64
GLA_DV = 128
GLA_RANK = 16
GLA_TAU = 16.0
GLA_ROWS = 128
GLA_ROWS_SHORT = 16
DIL_HEADS = 12
DIL_PAIRS = ((128, 1), (512, 4), (2048, 16))
DIL_WINDOW = 2048
ROPE_THETA = 10000.0
D_FF = 5504
PAST_LEN = 8192
CONV_W = 3
EPS = 1e-6

GLA_QK = GLA_HEADS * GLA_DK
GLA_VW = GLA_HEADS * GLA_DV
DIL_W = DIL_HEADS * HEAD_DIM

LANE = 128
SUBLANE = 8
VMEM_LIMIT = 56 * 1024 * 1024

Z_PAD = LANE
D_FF_P = 5632
FF_TN = 512
FF_NT = D_FF_P // FF_TN

NT_DIMS = (((1,), (1,)), ((), ()))


def _params(sem, vmem=VMEM_LIMIT, flags=None):
    return pltpu.CompilerParams(dimension_semantics=sem, vmem_limit_bytes=vmem, flags=flags)


def _rms_kernel(x_ref, g_ref, o_ref):
    x = x_ref[...]
    ms = jnp.mean(x * x, axis=-1, keepdims=True)
    o_ref[...] = (x * lax.rsqrt(ms + EPS) * g_ref[...]).astype(BF16)


def _rms(x, g, tm):
    m, d = x.shape
    return pl.pallas_call(
        _rms_kernel,
        grid=(m // tm,),
        in_specs=[pl.BlockSpec((tm, d), lambda i: (i, 0)), pl.BlockSpec((1, d), lambda i: (0, 0))],
        out_specs=pl.BlockSpec((tm, d), lambda i: (i, 0)),
        out_shape=jax.ShapeDtypeStruct((m, d), BF16),
        compiler_params=_params(("arbitrary",)),
        name="rms",
    )(x, g.reshape(1, d))


def _proj_kernel(x_ref, w_ref, o_ref, wbf_ref):
    @pl.when(pl.program_id(1) == 0)
    def _():
        wbf_ref[...] = w_ref[...].astype(BF16)

    o_ref[...] = lax.dot_general(x_ref[...], wbf_ref[...], NT_DIMS, preferred_element_type=F32)


def _proj(xn, wt, tm, tn, n=None, row0=0):
    m, k = xn.shape
    n = wt.shape[0] if n is None else n
    if row0 % tn == 0:
        w_spec = pl.BlockSpec((tn, k), lambda j, i: (row0 // tn + j, 0))
    else:
        assert row0 % SUBLANE == 0 and tn % SUBLANE == 0
        w_spec = pl.BlockSpec((pl.Element(tn), pl.Element(k)),
                              lambda j, i: (pl.multiple_of(row0 + j * tn, SUBLANE), 0))
    return pl.pallas_call(
        _proj_kernel,
        grid=(n // tn, m // tm),
        in_specs=[pl.BlockSpec((tm, k), lambda j, i: (i, 0)), w_spec],
        out_specs=pl.BlockSpec((tm, tn), lambda j, i: (i, j)),
        out_shape=jax.ShapeDtypeStruct((m, n), F32),
        scratch_shapes=[pltpu.VMEM((tn, k), BF16)],
        compiler_params=_params(("arbitrary", "arbitrary")),
        name="proj",
    )(xn, wt)


CLASS_DILATIONS = tuple(r for _, r in DIL_PAIRS if r > 1)


ROPE_ROWS = 128


def _rope_kernel(q_ref, k_ref, v_ref, cos_ref, sin_ref, *rest, by_class, tiles, first_kept):
    reps = DIL_W // LANE
    lane = lax.broadcasted_iota(jnp.int32, (1, DIL_W), 1)
    first_half = (lane % HEAD_DIM) < (HEAD_DIM // 2)
    tm = q_ref.shape[0]
    if by_class:
        n_class = 3 * len(CLASS_DILATIONS)
        qb_ref, kb_ref, vb_ref = rest[:3]
        class_refs = rest[3:3 + n_class]
        kt_ref, vt_ref, sq_ref, sk_ref, sv_ref = rest[3 + n_class:]
        kept = pl.program_id(0) % tiles >= first_kept
    else:
        qo_ref, ko_ref = rest

    for r0 in range(0, tm, ROPE_ROWS):
        rows = slice(r0, min(r0 + ROPE_ROWS, tm))
        cos = jnp.concatenate([cos_ref[rows, :]] * reps, axis=1)
        sin = jnp.concatenate([sin_ref[rows, :]] * reps, axis=1)

        def rot(x):
            partner = jnp.where(first_half,
                                pltpu.roll(x, DIL_W - HEAD_DIM // 2, axis=1),
                                pltpu.roll(x, HEAD_DIM // 2, axis=1))
            return x * cos + partner * sin

        q = rot(q_ref[rows, :]) * (HEAD_DIM ** -0.5)
        k = rot(k_ref[rows, :])
        if not by_class:
            qo_ref[rows, :] = q
            ko_ref[rows, :] = k
            continue
        v = v_ref[rows, :]
        qb_ref[rows, :] = q.astype(BF16)
        kb_ref[rows, :] = k.astype(BF16)
        vb_ref[rows, :] = v.astype(BF16)

        @pl.when(kept)
        def _():
            kt_ref[:, rows] = k.T
            vt_ref[:, rows] = v.T

        for src, val in ((sq_ref, q), (sk_ref, k), (sv_ref, v)):
            for j in range(reps):
                src[j, rows, :] = val[:, j * LANE:(j + 1) * LANE]
    if not by_class:
        return
    for di, r in enumerate(CLASS_DILATIONS):
        for src, dst in zip((sq_ref, sk_ref, sv_ref), class_refs[3 * di:3 * di + 3]):
            for c in range(r):
                rows = pl.ds(c, tm // r, stride=r)
                dst[c] = jnp.concatenate([src[j, rows, :] for j in range(reps)], axis=1).astype(BF16)


def _rope(proj, cos_t, sin_t, tm, n_seq=None):
    m = proj.shape[0]
    by_class = n_seq is not None
    blk = lambda c: pl.BlockSpec((tm, DIL_W), lambda i, c=c: (i, c))
    tab = pl.BlockSpec((tm, LANE), lambda i: (i, 0))
    out = pl.BlockSpec((tm, DIL_W), lambda i: (i, 0))
    tiles, first_kept, scratch = 1, 0, []
    if by_class:
        t_seq = m // n_seq
        tiles = t_seq // tm
        window = min(DIL_WINDOW, t_seq)
        first_kept = (t_seq - window) // tm
        out_specs = [out, out, out]
        out_shape = [jax.ShapeDtypeStruct((m, DIL_W), BF16)] * 3
        for r in CLASS_DILATIONS:
            spec = pl.BlockSpec((None, r, tm // r, DIL_W), lambda i: (i // tiles, 0, i % tiles, 0))
            out_specs += [spec] * 3
            out_shape += [jax.ShapeDtypeStruct((n_seq, r, t_seq // r, DIL_W), BF16)] * 3
        kept = pl.BlockSpec((None, DIL_W, tm), lambda i: (i // tiles, 0, jnp.maximum(i % tiles - first_kept, 0)))
        out_specs += [kept] * 2
        out_shape += [jax.ShapeDtypeStruct((n_seq, DIL_W, window), F32)] * 2
        scratch = [pltpu.VMEM((DIL_W // LANE, tm, LANE), F32)] * 3
    else:
        out_specs = [out, out]
        out_shape = [jax.ShapeDtypeStruct((m, DIL_W), F32)] * 2
    return pl.pallas_call(
        functools.partial(_rope_kernel, by_class=by_class, tiles=tiles, first_kept=first_kept),
        grid=(m // tm,),
        in_specs=[blk(0), blk(1), blk(2), tab, tab],
        out_specs=out_specs,
        out_shape=out_shape,
        scratch_shapes=scratch,
        compiler_params=_params(("arbitrary",)),
        name="rope",
    )(proj, proj, proj, cos_t, sin_t)


def _rope_tables(pos):
    half = HEAD_DIM // 2
    inv_freq = ROPE_THETA ** (-2.0 * jnp.arange(half, dtype=F32) / HEAD_DIM)
    ang = pos.astype(F32)[:, None] * inv_freq[None, :]
    cos = jnp.cos(ang)
    sin = jnp.sin(ang)
    cos_t = jnp.concatenate([cos, cos, cos, cos], axis=1)
    sin_t = jnp.concatenate([-sin, sin, -sin, sin], axis=1)
    return cos_t, sin_t


GLA_PAIRS = GLA_HEADS // 2


def _gla_kernel(q_ref, k_ref, v_ref, r_ref, z_ref, wg_ref, bg_ref, gn_ref, s0_ref,
                o_ref, sfin_ref, st_ref, *, chunk, n_chunks, t_valid):
    i = pl.program_id(1)
    pairs = range(GLA_PAIRS)

    @pl.when(i == 0)
    def _():
        for p in pairs:
            st_ref[p] = jnp.concatenate([s0_ref[2 * p].T, s0_ref[2 * p + 1].T], axis=1)

    row = lax.broadcasted_iota(jnp.int32, (chunk, chunk), 0)
    col = lax.broadcasted_iota(jnp.int32, (chunk, chunk), 1)
    tril = (row >= col).astype(BF16)
    t_in = lax.broadcasted_iota(jnp.int32, (chunk, 1), 0)
    both = lambda m: jnp.concatenate([m, m], axis=0)
    levels = []
    s_ = chunk
    while s_ >= 2:
        half = s_ // 2
        pivot = (row // s_) * s_ + (half - 1)
        levels.append(dict(
            size=s_,
            upto=(col <= pivot).astype(BF16),
            pair=both((row // s_ == col // s_) & (row % s_ >= half) & (col % s_ < half)),
            second=(t_in % s_) >= half))
        s_ = half
    diag = both(row == col)
    lo = lax.broadcasted_iota(jnp.int32, (1, LANE), 1) < GLA_DK
    mm = functools.partial(jnp.dot, preferred_element_type=F32)
    nt = functools.partial(lax.dot_general, dimension_numbers=NT_DIMS, preferred_element_type=F32)

    def split(a):
        hi = a.astype(BF16)
        return hi, (a - hi.astype(F32)).astype(BF16)

    def stack(a, p):
        t = a[:, p * LANE:(p + 1) * LANE]
        zero = jnp.zeros_like(t)
        return jnp.concatenate([jnp.where(lo, t, zero), jnp.where(lo, zero, t)], axis=0)

    for c in range(n_chunks):
        sl = pl.ds(c * chunk, chunk)
        z_hi, z_lo = split(z_ref[sl, :])
        x = mm(z_hi, wg_ref[0]) + mm(z_hi, wg_ref[1]) + mm(z_lo, wg_ref[0]) + bg_ref[...]
        log_a = -(jnp.maximum(-x, 0.0) + jnp.log1p(jnp.exp(-jnp.abs(x)))) / GLA_TAU
        t_abs = (i * n_chunks + c) * chunk + t_in
        log_a = jnp.where(t_abs < t_valid, log_a, 0.0)
        a_hi, a_lo = split(log_a)
        cum = mm(tril, a_hi) + mm(tril, a_lo)
        last = cum[chunk - 1:chunk, :]
        q = q_ref[sl, :] * (GLA_DK ** -0.5)
        k = k_ref[sl, :]
        q_dec = (q * jnp.exp(cum)).astype(BF16)
        k_end = (k * jnp.exp(last - cum)).astype(BF16)
        a_end = jnp.exp(last)
        heads = range(GLA_HEADS)
        vs = [slice(h * GLA_DV, (h + 1) * GLA_DV) for h in heads]
        v_f = [v_ref[sl, vs[h]] for h in heads]
        vh = [v_f[h].astype(BF16) for h in heads]
        pivot_cum = []
        for lv in levels:
            if lv["size"] >= SUBLANE:
                blocks = cum.reshape(chunk // lv["size"], lv["size"], cum.shape[1])
                pick = blocks[:, lv["size"] // 2 - 1:lv["size"] // 2, :]
                pivot_cum.append(jnp.broadcast_to(pick, blocks.shape).reshape(cum.shape))
            else:
                pivot_cum.append(mm(lv["upto"], a_hi) + mm(lv["upto"], a_lo))
        q_lv = [(q * jnp.exp(jnp.where(lv["second"], cum - pc, 0.0))).astype(BF16)
                for lv, pc in zip(levels, pivot_cum)]
        k_lv = [(k * jnp.exp(jnp.where(lv["second"], 0.0, pc - cum))).astype(BF16)
                for lv, pc in zip(levels, pivot_cum)]
        q_bf, k_bf = q.astype(BF16), k.astype(BF16)
        lanes = [slice(p * LANE, (p + 1) * LANE) for p in pairs]
        parts = [[nt(stack(ql, p), kl[:, lanes[p]]) for p in pairs] for ql, kl in zip(q_lv, k_lv)]
        scores = [jnp.where(diag, nt(stack(q_bf, p), k_bf[:, lanes[p]]), 0.0) for p in pairs]
        for lv, part in zip(levels, parts):
            scores = [jnp.where(lv["pair"], part[p], scores[p]) for p in pairs]
        scores = [scores[p].astype(BF16) for p in pairs]
        st = [st_ref[p] for p in pairs]
        carried = [nt(stack(q_dec, p), st[p].astype(BF16)) for p in pairs]
        o = [mm(scores[h // 2][(h % 2) * chunk:(h % 2 + 1) * chunk], vh[h])
             + carried[h // 2][(h % 2) * chunk:(h % 2 + 1) * chunk] for h in heads]
        for p in pairs:
            ke = k_end[:, lanes[p]]
            zero = jnp.zeros_like(ke)
            st_ref[p] = (st[p] * a_end[:, lanes[p]]
                         + mm(v_f[2 * p].T.astype(BF16), jnp.where(lo, ke, zero))
                         + mm(v_f[2 * p + 1].T.astype(BF16), jnp.where(lo, zero, ke)))
        ms = [jnp.mean(o[h] * o[h], axis=-1, keepdims=True) for h in heads]
        for h in heads:
            on = o[h] * lax.rsqrt(ms[h] + EPS) * gn_ref[...]
            half_r = 0.5 * r_ref[sl, vs[h]]
            o_ref[sl, vs[h]] = (on * (half_r + half_r * jnp.tanh(half_r))).astype(BF16)

    @pl.when(i == pl.num_programs(1) - 1)
    def _():
        for p in pairs:
            st = st_ref[p]
            sfin_ref[2 * p] = st[:, :GLA_DK].T
            sfin_ref[2 * p + 1] = st[:, GLA_DK:].T


def _gla(src, zsrc, wg, bg, gn, s0, *, chunk, n_chunks, t_valid):
    b, t, _ = src.shape
    tb = chunk * n_chunks
    col = lambda w, c: pl.BlockSpec((None, tb, w), lambda bi, i, c=c: (bi, i, c))
    full = lambda shape: pl.BlockSpec(shape, lambda bi, i: (0,) * len(shape))
    state = pl.BlockSpec((None, GLA_HEADS, GLA_DK, GLA_DV), lambda bi, i: (bi, 0, 0, 0))
    kern = functools.partial(_gla_kernel, chunk=chunk, n_chunks=n_chunks, t_valid=t_valid)
    return pl.pallas_call(
        kern,
        grid=(b, t // tb),
        in_specs=[col(GLA_QK, 0), col(GLA_QK, 1), col(GLA_VW, 1), col(GLA_VW, 2), col(Z_PAD, 0),
                  full((2, Z_PAD, GLA_QK)), full((1, GLA_QK)), full((1, GLA_DV)), state],
        out_specs=[pl.BlockSpec((None, tb, GLA_VW), lambda bi, i: (bi, i, 0)), state],
        out_shape=[jax.ShapeDtypeStruct((b, t, GLA_VW), BF16),
                   jax.ShapeDtypeStruct((b, GLA_HEADS, GLA_DK, GLA_DV), F32)],
        scratch_shapes=[pltpu.VMEM((GLA_PAIRS, GLA_DV, 2 * GLA_DK), F32)],
        compiler_params=_params(("arbitrary", "arbitrary")),
        name="gla",
    )(src, src, src, src, zsrc, wg, bg, gn, s0)


QB = 128
DIL_SUB = 2


def _dil_branch_kernel(*refs, first, final, dilation):
    if first:
        q_ref, kp_ref, kc_ref, vp_ref, vc_ref = refs[:5]
        outs = refs[5:]
    else:
        q_ref, kp_ref, kc_ref, vp_ref, vc_ref, num_ref, m_ref, d_ref = refs[:8]
        outs = refs[8:]
    i = pl.program_id(1)
    c = pl.program_id(2)
    qi = QB + lax.broadcasted_iota(jnp.int32, (QB, 1), 0)
    kj = lax.broadcasted_iota(jnp.int32, (1, 2 * QB), 1)
    off = qi - kj
    in_window = (off >= 0) & (off <= QB)
    lane = lax.broadcasted_iota(jnp.int32, (1, LANE), 1)
    lo = lane < HEAD_DIM
    heads = range(DIL_HEADS)
    cs = [slice(p * LANE, (p + 1) * LANE) for p in range(DIL_HEADS // 2)]
    sel = [lo if h % 2 == 0 else jnp.logical_not(lo) for h in heads]
    for u in range(DIL_SUB):
        blk = slice(u * QB, (u + 1) * QB)
        rows = pl.ds(dilation * u * QB + c, QB, stride=dilation) if dilation > 1 else blk
        if u == 0:
            k_before, v_before = kp_ref[...], vp_ref[...]
            valid = in_window & ((kj >= QB) | (i > 0))
        else:
            before = slice((u - 1) * QB, u * QB)
            k_before, v_before = kc_ref[before, :], vc_ref[before, :]
            valid = in_window
        k2 = jnp.concatenate([k_before, kc_ref[blk, :]], axis=0)
        v2 = jnp.concatenate([v_before, vc_ref[blk, :]], axis=0)
        q = q_ref[blk, :]
        qm = [jnp.where(sel[h], q[:, cs[h // 2]], jnp.zeros((QB, LANE), BF16)) for h in heads]
        s = [lax.dot_general(qm[h], k2[:, cs[h // 2]], NT_DIMS, preferred_element_type=F32) for h in heads]
        s = [jnp.where(valid, s[h], -jnp.inf) for h in heads]
        m_new = [jnp.max(s[h], axis=-1, keepdims=True) for h in heads]
        if not first:
            m_in = m_ref[rows, :]
            d_in = d_ref[rows, :]
            m_old = [jnp.sum(jnp.where(lane == h, m_in, 0.0), axis=-1, keepdims=True) for h in heads]
            d_old = [jnp.sum(jnp.where(lane == h, d_in, 0.0), axis=-1, keepdims=True) for h in heads]
            m_new = [jnp.maximum(m_new[h], m_old[h]) for h in heads]
        pr = [jnp.exp(s[h] - m_new[h]) for h in heads]
        d_new = [jnp.sum(pr[h], axis=-1, keepdims=True) for h in heads]
        if not first:
            alpha = [jnp.exp(m_old[h] - m_new[h]) for h in heads]
            d_new = [d_new[h] + d_old[h] * alpha[h] for h in heads]
        pv = [jnp.dot(pr[h].astype(BF16), v2[:, cs[h // 2]], preferred_element_type=F32) for h in heads]
        pairs = []
        for p in range(DIL_HEADS // 2):
            num = jnp.where(lo, pv[2 * p], pv[2 * p + 1])
            if not first:
                num = num + num_ref[p, rows, :] * jnp.where(lo, alpha[2 * p], alpha[2 * p + 1])
            if final:
                num = num / jnp.where(lo, d_new[2 * p], d_new[2 * p + 1])
            else:
                outs[0][p, rows, :] = num
            pairs.append(num)
        if final:
            outs[0][rows, :] = jnp.concatenate(pairs, axis=1).astype(BF16)
        else:
            outs[1][rows, :] = functools.reduce(jnp.add, [jnp.where(lane == h, m_new[h], 0.0) for h in heads])
            outs[2][rows, :] = functools.reduce(jnp.add, [jnp.where(lane == h, d_new[h], 0.0) for h in heads])


def _dil_branch(q, k, v, state, *, first, final):
    b, r, ts, _ = q.shape
    t = r * ts
    step = DIL_SUB * QB
    nq = ts // step
    cur = pl.BlockSpec((None, None, step, DIL_W), lambda bi, i, c: (bi, c, i, 0))
    prev = pl.BlockSpec((None, None, QB, DIL_W), lambda bi, i, c: (bi, c, jnp.maximum(DIL_SUB * i - 1, 0), 0))
    nat = lambda w: pl.BlockSpec((None, r * step, w), lambda bi, i, c: (bi, i, 0))
    nat_num = pl.BlockSpec((None, DIL_W // LANE, r * step, LANE), lambda bi, i, c: (bi, 0, i, 0))
    args = [q, k, k, v, v]
    in_specs = [cur, prev, cur, prev, cur]
    if not first:
        args += list(state)
        in_specs += [nat_num, nat(LANE), nat(LANE)]
    if final:
        assert r == 1
        out_specs = [nat(DIL_W)]
        out_shape = [jax.ShapeDtypeStruct((b, t, DIL_W), BF16)]
    else:
        out_specs = [nat_num, nat(LANE), nat(LANE)]
        out_shape = [jax.ShapeDtypeStruct((b, DIL_W // LANE, t, LANE), F32),
                     jax.ShapeDtypeStruct((b, t, LANE), F32),
                     jax.ShapeDtypeStruct((b, t, LANE), F32)]
    return pl.pallas_call(
        functools.partial(_dil_branch_kernel, first=first, final=final, dilation=r),
        grid=(b, nq, r),
        in_specs=in_specs,
        out_specs=out_specs,
        out_shape=out_shape,
        compiler_params=_params(("arbitrary", "arbitrary", "arbitrary")),
        name=f"dil_branch_r{r}",
    )(*args)


SAMPLE_HB = 6
SAMPLE_ROWS = 16


def _dil_sample_kernel(q_ref, kn_ref, vn_ref, knt_ref, vnt_ref, kc_ref, vc_ref, o_ref, ko_ref, vo_ref,
                       *, t_new, hist):
    t_row = lax.broadcasted_iota(jnp.int32, (SAMPLE_ROWS, 1), 0)
    j = lax.broadcasted_iota(jnp.int32, (1, hist), 1)
    lane = lax.broadcasted_iota(jnp.int32, (1, LANE), 1)
    keep = lane < LANE - t_new

    def weight(off):
        w = jnp.zeros(off.shape, F32)
        for win, dil in DIL_PAIRS:
            w = w + jnp.where((off >= 0) & (off <= win) & (off % dil == 0), 1.0, 0.0)
        return w

    w_c = weight(hist + t_row - j)
    w_n = [weight(t_row - t) for t in range(t_new)]
    heads = range(SAMPLE_HB)
    hs = [slice(h * HEAD_DIM, (h + 1) * HEAD_DIM) for h in heads]
    qs = [q_ref[:, hs[h]] for h in heads]
    s_c = [jnp.dot(qs[h].astype(BF16), kc_ref[h].astype(BF16), preferred_element_type=F32) for h in heads]
    s_c = [jnp.where(w_c > 0, s, -jnp.inf) for s in s_c]
    s_n = [[jnp.where(w_n[t] > 0, jnp.sum(qs[h] * kn_ref[t:t + 1, hs[h]], axis=-1, keepdims=True), -jnp.inf)
            for t in range(t_new)] for h in heads]
    m = [functools.reduce(jnp.maximum, s_n[h], jnp.max(s_c[h], axis=-1, keepdims=True)) for h in heads]
    p_c = [w_c * jnp.exp(s_c[h] - m[h]) for h in heads]
    p_n = [[w_n[t] * jnp.exp(s_n[h][t] - m[h]) for t in range(t_new)] for h in heads]
    den = [functools.reduce(jnp.add, p_n[h], jnp.sum(p_c[h], axis=-1, keepdims=True)) for h in heads]
    num = [lax.dot_general(p_c[h].astype(BF16), vc_ref[h].astype(BF16), NT_DIMS, preferred_element_type=F32)
           for h in heads]
    num = [functools.reduce(jnp.add, [p_n[h][t] * vn_ref[t:t + 1, hs[h]] for t in range(t_new)], num[h])
           for h in heads]
    o_ref[...] = jnp.concatenate([num[h] / den[h] for h in heads], axis=1)

    for h in heads:
        for src, new_ref, dst in ((kc_ref, knt_ref, ko_ref), (vc_ref, vnt_ref, vo_ref)):
            nxt = pltpu.roll(src[h, :, 0:LANE], LANE - t_new, axis=1)
            for c in range(hist // LANE):
                cur_t = nxt
                if c + 1 < hist // LANE:
                    nxt = pltpu.roll(src[h, :, (c + 1) * LANE:(c + 2) * LANE], LANE - t_new, axis=1)
                else:
                    nxt = new_ref[h]
                dst[h, :, c * LANE:(c + 1) * LANE] = jnp.where(keep, cur_t, nxt)


def _dil_sample(q, kn, vn, knt, vnt, cache_kt, cache_vt, t_new):
    b, nh, _, hist = cache_kt.shape
    w = SAMPLE_HB * HEAD_DIM
    small = pl.BlockSpec((None, SAMPLE_ROWS, w), lambda bi, c: (bi, 0, c))
    newt = pl.BlockSpec((None, SAMPLE_HB, HEAD_DIM, LANE), lambda bi, c: (bi, c, 0, 0))
    big = pl.BlockSpec((None, SAMPLE_HB, HEAD_DIM, hist), lambda bi, c: (bi, c, 0, 0))
    return pl.pallas_call(
        functools.partial(_dil_sample_kernel, t_new=t_new, hist=hist),
        grid=(b, nh // SAMPLE_HB),
        in_specs=[small, small, small, newt, newt, big, big],
        out_specs=[small, big, big],
        out_shape=[jax.ShapeDtypeStruct((b, SAMPLE_ROWS, nh * HEAD_DIM), F32),
                   jax.ShapeDtypeStruct(cache_kt.shape, F32),
                   jax.ShapeDtypeStruct(cache_vt.shape, F32)],
        compiler_params=_params(("arbitrary", "arbitrary")),
        name="dil_sample",
    )(q, kn, vn, knt, vnt, cache_kt, cache_vt)


OUT_ROWS = 256

STRIDE_ROWS = 256
STRIDE_GROUPS = STRIDE_ROWS // SUBLANE


def _to_strided(val, stage_ref):
    for j in range(val.shape[1] // LANE):
        for s_ in range(SUBLANE):
            stage_ref[j, pl.ds(s_, STRIDE_GROUPS, stride=SUBLANE), :] = (
                val[s_ * STRIDE_GROUPS:(s_ + 1) * STRIDE_GROUPS, j * LANE:(j + 1) * LANE])
    return jnp.concatenate([stage_ref[j] for j in range(val.shape[1] // LANE)], axis=1)


def _from_strided(val, stage_ref):
    for j in range(val.shape[1] // LANE):
        stage_ref[j] = val[:, j * LANE:(j + 1) * LANE]
    return jnp.concatenate(
        [jnp.concatenate([stage_ref[j, pl.ds(s_, STRIDE_GROUPS, stride=SUBLANE), :]
                          for j in range(val.shape[1] // LANE)], axis=1) for s_ in range(SUBLANE)], axis=0)


def _out_proj_kernel(oa_ref, ob_ref, w_ref, x_ref, g_ref, h_ref, xn_ref, wbf_ref, *stage, strided):
    @pl.when(pl.program_id(0) == 0)
    def _():
        wbf_ref[...] = w_ref[...].astype(BF16)

    for r0 in range(0, x_ref.shape[0], OUT_ROWS):
        rows = slice(r0, min(r0 + OUT_ROWS, x_ref.shape[0]))
        y = (jnp.dot(oa_ref[rows, :], wbf_ref[:GLA_VW, :], preferred_element_type=F32)
             + jnp.dot(ob_ref[rows, :], wbf_ref[GLA_VW:, :], preferred_element_type=F32))
        h = x_ref[rows, :] + y
        h_ref[rows, :] = h
        ms = jnp.mean(h * h, axis=-1, keepdims=True)
        xn = h * lax.rsqrt(ms + EPS) * g_ref[...]
        if strided:
            xn = _to_strided(xn, stage[0])
        xn_ref[rows, :] = xn.astype(BF16)


def _out_proj(oa, ob, w, x, g, tm, strided=False):
    m, d = x.shape
    assert not strided or OUT_ROWS == STRIDE_ROWS
    row = lambda width: pl.BlockSpec((tm, width), lambda i: (i, 0))
    g2 = g.reshape(1, d)
    scratch = [pltpu.VMEM(w.shape, BF16)]
    if strided:
        scratch.append(pltpu.VMEM((d // LANE, STRIDE_ROWS, LANE), F32))
    return pl.pallas_call(
        functools.partial(_out_proj_kernel, strided=strided),
        grid=(m // tm,),
        in_specs=[row(GLA_VW), row(DIL_W),
                  pl.BlockSpec(w.shape, lambda i: (0, 0), pipeline_mode=pl.Buffered(1)),
                  row(d), pl.BlockSpec(g2.shape, lambda i: (0, 0))],
        out_specs=[row(d), row(d)],
        out_shape=[jax.ShapeDtypeStruct((m, d), F32), jax.ShapeDtypeStruct((m, d), BF16)],
        scratch_shapes=scratch,
        compiler_params=_params(("arbitrary",)),
        name="out_proj",
    )(oa, ob, w, x, g2)


FF_ROWS = STRIDE_ROWS
FF_TAIL = 2 * SUBLANE
FF_SUB = FF_TN // LANE
FF_TILES = D_FF // LANE


def _conv_taps(cw, cb, up, prev1, prev2):
    return cb + cw[0:1] * prev2 + cw[1:2] * prev1 + cw[2:3] * up


def _swiglu(u, g):
    h = 0.5 * g
    return ((h + h * jnp.tanh(h)) * u).astype(BF16)


def _stage_up_weights(w_blocks, wbf_ref, j):
    for half in range(2):
        for q in range(FF_SUB):
            blk = w_blocks[half * FF_SUB + q][...]
            if (FF_NT - 1) * FF_SUB + q >= FF_TILES:
                blk = jnp.where(j * FF_SUB + q < FF_TILES, blk, 0.0)
            wbf_ref[half, :, q * LANE:(q + 1) * LANE] = blk.astype(BF16)


def _up_weight_specs(tile_of_step):
    specs = []
    for half in range(2):
        for q in range(FF_SUB):
            def index(*ids, half=half, q=q):
                blk = jnp.minimum(tile_of_step(*ids) * FF_SUB + q, FF_TILES - 1)
                return (0, half * FF_TILES + blk)
            specs.append(pl.BlockSpec((D_MODEL, LANE), index))
    return specs


def _ffn_up_seq_kernel(*refs, tiles_per_seq):
    x_ref = refs[0]
    w_blocks = refs[1:1 + 2 * FF_SUB]
    cwu_ref, cwg_ref, cbu_ref, cbg_ref, act_ref, tu_ref, tg_ref, wbf_ref, carry_ref = refs[1 + 2 * FF_SUB:]
    j = pl.program_id(0)
    i = pl.program_id(1)
    tm = x_ref.shape[0]

    @pl.when(i == 0)
    def _():
        _stage_up_weights(w_blocks, wbf_ref, j)

    @pl.when(i % tiles_per_seq == 0)
    def _():
        carry_ref[...] = jnp.zeros_like(carry_ref)

    carry = [carry_ref[0], carry_ref[1]]
    shift = lambda before, cur: pltpu.roll(jnp.concatenate([before, cur], axis=0), 1, axis=0)[SUBLANE:]
    for r0 in range(0, tm, FF_ROWS):
        x = x_ref[r0:r0 + FF_ROWS, :]
        conv = []
        for idx, (cw_ref, cb_ref) in enumerate(((cwu_ref, cbu_ref), (cwg_ref, cbg_ref))):
            up = jnp.dot(x, wbf_ref[idx], preferred_element_type=F32)
            tail = up[FF_ROWS - FF_TAIL:]
            p_a = shift(carry[idx][:SUBLANE], tail[:SUBLANE])
            p_b = shift(carry[idx][SUBLANE:], tail[SUBLANE:])
            prev1 = jnp.concatenate([p_b, up[:FF_ROWS - SUBLANE]], axis=0)
            prev2 = jnp.concatenate([p_a, p_b, up[:FF_ROWS - 2 * SUBLANE]], axis=0)
            carry[idx] = tail
            conv.append(_conv_taps(cw_ref[...], cb_ref[...], up, prev1, prev2))
        act_ref[r0:r0 + FF_ROWS, :] = _swiglu(*conv)
    for idx, tail_ref in enumerate((tu_ref, tg_ref)):
        carry_ref[idx] = carry[idx]
        tail_ref[...] = carry[idx]


def _ffn_up_seq(xn, w_up, conv_w, conv_b, tm, t_seq):
    m, d = xn.shape
    tiles_per_seq = t_seq // tm
    xs = pl.BlockSpec((tm, d), lambda j, i: (i, 0))
    cu = lambda r: pl.BlockSpec((r, FF_TN), lambda j, i: (0, j))
    cg = lambda r: pl.BlockSpec((r, FF_TN), lambda j, i: (0, FF_NT + j))
    act = pl.BlockSpec((tm, FF_TN), lambda j, i: (i, j))
    tail = pl.BlockSpec((None, FF_TAIL, FF_TN), lambda j, i: (i // tiles_per_seq, 0, j))
    w_specs = _up_weight_specs(lambda j, i: j)
    return pl.pallas_call(
        functools.partial(_ffn_up_seq_kernel, tiles_per_seq=tiles_per_seq),
        grid=(FF_NT, m // tm),
        in_specs=[xs] + w_specs + [cu(CONV_W), cg(CONV_W), cu(1), cg(1)],
        out_specs=[act, tail, tail],
        out_shape=[jax.ShapeDtypeStruct((m, D_FF_P), BF16)]
        + [jax.ShapeDtypeStruct((m // t_seq, FF_TAIL, D_FF_P), F32)] * 2,
        scratch_shapes=[pltpu.VMEM((2, d, FF_TN), BF16), pltpu.VMEM((2, FF_TAIL, FF_TN), F32)],
        compiler_params=_params(("arbitrary", "arbitrary")),
        name="ffn_up",
    )(xn, *([w_up] * len(w_specs)), conv_w, conv_w, conv_b, conv_b)


def _ffn_up_hist_kernel(*refs, n_seq):
    x_ref = refs[0]
    w_blocks = refs[1:1 + 2 * FF_SUB]
    cwu_ref, cwg_ref, cbu_ref, cbg_ref, hu_ref, hg_ref, act_ref, upu_ref, upg_ref, wbf_ref = refs[1 + 2 * FF_SUB:]
    _stage_up_weights(w_blocks, wbf_ref, pl.program_id(0))
    x = x_ref[...]
    m = x.shape[0]
    conv = []
    for idx, (cw_ref, cb_ref, h_ref, up_ref) in enumerate((
            (cwu_ref, cbu_ref, hu_ref, upu_ref), (cwg_ref, cbg_ref, hg_ref, upg_ref))):
        up = jnp.dot(x, wbf_ref[idx], preferred_element_type=F32)
        full = jnp.concatenate([h_ref[...], up], axis=0)
        prev1 = full[n_seq:n_seq + m]
        prev2 = full[:m]
        up_ref[...] = up
        conv.append(_conv_taps(cw_ref[...], cb_ref[...], up, prev1, prev2))
    act_ref[...] = _swiglu(*conv)


def _ffn_up_hist(xn, w_up, conv_w, conv_b, hist, n_seq):
    m, d = xn.shape
    assert n_seq % SUBLANE == 0
    xs = pl.BlockSpec((m, d), lambda j: (0, 0))
    u = lambda r: pl.BlockSpec((r, FF_TN), lambda j: (0, j))
    g = lambda r: pl.BlockSpec((r, FF_TN), lambda j: (0, FF_NT + j))
    w_specs = _up_weight_specs(lambda j: j)
    nh = hist.shape[0]
    return pl.pallas_call(
        functools.partial(_ffn_up_hist_kernel, n_seq=n_seq),
        grid=(FF_NT,),
        in_specs=[xs] + w_specs + [u(CONV_W), g(CONV_W), u(1), g(1), u(nh), g(nh)],
        out_specs=[u(m), u(m), u(m)],
        out_shape=[jax.ShapeDtypeStruct((m, D_FF_P), BF16)] + [jax.ShapeDtypeStruct((m, D_FF_P), F32)] * 2,
        scratch_shapes=[pltpu.VMEM((2, d, FF_TN), BF16)],
        compiler_params=_params(("arbitrary",)),
        name="ffn_up_hist",
    )(xn, *([w_up] * len(w_specs)), conv_w, conv_w, conv_b, conv_b, hist, hist)


DOWN_ROWS = 256


def _ffn_down_kernel(a_ref, w_ref, h_ref, g_ref, y_ref, *stage, strided):
    kdim = w_ref.shape[0]
    for r0 in range(0, a_ref.shape[0], DOWN_ROWS):
        rows = slice(r0, min(r0 + DOWN_ROWS, a_ref.shape[0]))
        f = jnp.dot(a_ref[rows, :kdim], w_ref[...], preferred_element_type=F32)
        if strided:
            f = _from_strided(f, stage[0])
        h = h_ref[rows, :] + f
        ms = jnp.mean(h * h, axis=-1, keepdims=True)
        y_ref[rows, :] = h * lax.rsqrt(ms + EPS) * g_ref[...]


def _ffn_down(act, w, h, g, tm, strided=False):
    m, d = h.shape
    assert not strided or DOWN_ROWS == STRIDE_ROWS
    scratch = [pltpu.VMEM((d // LANE, STRIDE_ROWS, LANE), F32)] if strided else []
    return pl.pallas_call(
        functools.partial(_ffn_down_kernel, strided=strided),
        grid=(m // tm,),
        in_specs=[pl.BlockSpec((tm, act.shape[1]), lambda i: (i, 0)),
                  pl.BlockSpec(w.shape, lambda i: (0, 0), pipeline_mode=pl.Buffered(1)),
                  pl.BlockSpec((tm, d), lambda i: (i, 0)),
                  pl.BlockSpec((1, d), lambda i: (0, 0))],
        out_specs=pl.BlockSpec((tm, d), lambda i: (i, 0)),
        out_shape=jax.ShapeDtypeStruct((m, d), F32),
        scratch_shapes=scratch,
        compiler_params=_params(("arbitrary",)),
        name="ffn_down",
    )(act, w, h, g.reshape(1, d))


def _pad_cols(a, width):
    return jnp.pad(a, ((0, 0), (0, width - a.shape[1])))


def _split_pad_ff(a):
    return jnp.concatenate([_pad_cols(a[:, :D_FF], D_FF_P), _pad_cols(a[:, D_FF:], D_FF_P)], axis=1)


def _unpad_ff(u, g):
    return jnp.concatenate([u[..., :D_FF], g[..., :D_FF]], axis=-1)


def kernel(x_prompt, x_sample, state_gla, cache_dil_k, cache_dil_v, state_ffn_conv, norm_mix, w_in, w_gate_up,
           b_gate, gla_norm, w_out, norm_ffn, w_ffn_up, ffn_conv_w, ffn_conv_b, w_ffn_down, norm_final):
    bp, tp, d = x_prompt.shape
    bs, ts, _ = x_sample.shape
    l = 0

    z0 = 2 * GLA_QK + 2 * GLA_VW
    w_in_t = w_in[l].T
    w_z_t = jnp.pad(w_in_t[z0:z0 + GLA_RANK], ((0, Z_PAD - GLA_RANK), (0, 0)))
    wg_f = jnp.pad(w_gate_up[l], ((0, Z_PAD - GLA_RANK), (0, 0)))
    wg_hi = wg_f.astype(BF16)
    wg_p = jnp.stack([wg_hi, (wg_f - wg_hi.astype(F32)).astype(BF16)])
    bg = b_gate[l].reshape(1, GLA_QK)
    gn = gla_norm[l].reshape(1, GLA_DV)
    cw_p = _split_pad_ff(ffn_conv_w[l])
    cb_p = _split_pad_ff(ffn_conv_b[l].reshape(1, -1))
    w_dn = w_ffn_down[l].astype(BF16)

    mp = bp * tp
    xp = x_prompt.reshape(mp, d)
    xn_mix_p = _rms(xp, norm_mix[l], 512)
    mix_p = _proj(xn_mix_p, w_in_t, 2048, 768, n=z0)
    dil_p = _proj(xn_mix_p, w_in_t, 2048, 768, n=3 * DIL_W, row0=z0 + GLA_RANK)
    z_p = _proj(xn_mix_p, w_z_t, 2048, Z_PAD)
    cos_p, sin_p = _rope_tables(jnp.arange(tp))
    cos_p = jnp.tile(cos_p, (bp, 1))
    sin_p = jnp.tile(sin_p, (bp, 1))
    rope_out = _rope(dil_p, cos_p, sin_p, 512, n_seq=bp)
    by_class = {1: tuple(a.reshape(bp, 1, tp, DIL_W) for a in rope_out[:3])}
    for di, r in enumerate(CLASS_DILATIONS):
        by_class[r] = tuple(rope_out[3 + 3 * di:6 + 3 * di])
    k_kept_t, v_kept_t = rope_out[-2:]

    oa_p, gla_p = _gla(mix_p.reshape(bp, tp, z0), z_p.reshape(bp, tp, Z_PAD), wg_p, bg, gn,
                       jnp.zeros((bp, GLA_HEADS, GLA_DK, GLA_DV), F32),
                       chunk=GLA_ROWS, n_chunks=2, t_valid=tp)

    state = _dil_branch(*by_class[16], None, first=True, final=False)
    state = _dil_branch(*by_class[4], state, first=False, final=False)
    (ob_p,) = _dil_branch(*by_class[1], state, first=False, final=True)

    h_p, xn_p = _out_proj(oa_p.reshape(mp, GLA_VW), ob_p.reshape(mp, DIL_W), w_out[l], xp, norm_ffn[l], 512, strided=True)
    act_p, tail_u, tail_g = _ffn_up_seq(xn_p, w_ffn_up[l], cw_p, cb_p, 2048, tp)
    y_p = _ffn_down(act_p, w_dn, h_p, norm_final, 512, strided=True)

    buf_p = min(DIL_WINDOW, tp)
    y_prompt = y_p.reshape(bp, tp, d)
    new_gla_p = gla_p[None]
    new_k_p = k_kept_t.reshape(bp, DIL_HEADS, HEAD_DIM, buf_p).transpose(0, 3, 1, 2)[None]
    new_v_p = v_kept_t.reshape(bp, DIL_HEADS, HEAD_DIM, buf_p).transpose(0, 3, 1, 2)[None]
    last_two = lambda a: a[:, SUBLANE - 1::SUBLANE]
    new_conv_p = _unpad_ff(last_two(tail_u), last_two(tail_g))[None]

    ms_ = bs * ts
    xs = x_sample.reshape(ms_, d)
    xn_mix_s = _rms(xs, norm_mix[l], ms_)
    mix_s = _proj(xn_mix_s, w_in_t, ms_, 768, n=z0)
    dil_s = _proj(xn_mix_s, w_in_t, ms_, 768, n=3 * DIL_W, row0=z0 + GLA_RANK)
    z_s = _proj(xn_mix_s, w_z_t, ms_, Z_PAD)
    cos_s, sin_s = _rope_tables(PAST_LEN + jnp.arange(ts))
    cos_s = jnp.tile(cos_s, (bs, 1))
    sin_s = jnp.tile(sin_s, (bs, 1))
    q_rs, k_rs = _rope(dil_s, cos_s, sin_s, ms_)

    pad_chunk = lambda a: jnp.pad(a.reshape(bs, ts, -1), ((0, 0), (0, GLA_ROWS_SHORT - ts), (0, 0)))
    oa_s, gla_s = _gla(pad_chunk(mix_s), pad_chunk(z_s), wg_p, bg, gn, state_gla[l],
                       chunk=GLA_ROWS_SHORT, n_chunks=1, t_valid=ts)
    oa_s = oa_s[:, :ts].reshape(ms_, GLA_VW)

    v_s = dil_s[:, 2 * DIL_W:]
    pad_rows = lambda a: jnp.pad(a.reshape(bs, ts, DIL_W), ((0, 0), (0, SAMPLE_ROWS - ts), (0, 0)))
    new_cols = lambda a: jnp.pad(a.reshape(bs, ts, DIL_HEADS, HEAD_DIM).transpose(0, 2, 3, 1),
                                 ((0, 0), (0, 0), (0, 0), (LANE - ts, 0)))
    time_minor = lambda a: a.transpose(0, 2, 3, 1)
    ob_s, k_s_t, v_s_t = _dil_sample(pad_rows(q_rs), pad_rows(k_rs), pad_rows(v_s), new_cols(k_rs), new_cols(v_s),
                                     time_minor(cache_dil_k[l]), time_minor(cache_dil_v[l]), ts)
    ob_s = ob_s[:, :ts].astype(BF16)

    h_s, xn_s = _out_proj(oa_s, ob_s.reshape(ms_, DIL_W), w_out[l], xs, norm_ffn[l], ms_)
    time_major = lambda a: a.reshape(bs, -1, a.shape[-1]).transpose(1, 0, 2).reshape(-1, a.shape[-1])
    batch_major = lambda a: a.reshape(-1, bs, a.shape[-1]).transpose(1, 0, 2)
    conv_hist = time_major(_split_pad_ff(state_ffn_conv[l].reshape(bs * (CONV_W - 1), 2 * D_FF)))
    act_s, up_u, up_g = _ffn_up_hist(time_major(xn_s), w_ffn_up[l], cw_p, cb_p, conv_hist, bs)
    act_s = batch_major(act_s).reshape(ms_, D_FF_P)
    y_s = _ffn_down(act_s, w_dn, h_s, norm_final, ms_)

    y_sample = y_s.reshape(bs, ts, d)
    new_gla_s = gla_s[None]
    new_k_s = k_s_t.transpose(0, 3, 1, 2)[None]
    new_v_s = v_s_t.transpose(0, 3, 1, 2)[None]
    new_conv_s = batch_major(_unpad_ff(up_u, up_g))[None, :, ts - (CONV_W - 1):]

    return (y_prompt, y_sample, new_gla_p, new_gla_s, new_k_p, new_k_s, new_v_p, new_v_s,
            new_conv_p, new_conv_s)
```

```python
import functools

import jax
import jax.numpy as jnp
from jax import lax
from jax.experimental import pallas as pl
from jax.experimental.pallas import tpu as pltpu

F32 = jnp.float32
BF16 = jnp.bfloat16

D_MODEL = 2048
HEAD_DIM = 64
GLA_HEADS = 10
GLA_DK = 64
GLA_DV = 128
GLA_RANK = 16
GLA_TAU = 16.0
GLA_ROWS = 128
GLA_ROWS_SHORT = 16
DIL_HEADS = 12
DIL_PAIRS = ((128, 1), (512, 4), (2048, 16))
DIL_WINDOW = 2048
ROPE_THETA = 10000.0
D_FF = 5504
PAST_LEN = 8192
CONV_W = 3
EPS = 1e-6

GLA_QK = GLA_HEADS * GLA_DK
GLA_VW = GLA_HEADS * GLA_DV
DIL_W = DIL_HEADS * HEAD_DIM

LANE = 128
SUBLANE = 8
VMEM_LIMIT = 56 * 1024 * 1024

Z_PAD = LANE
D_FF_P = 5632
FF_TN = 512
FF_NT = D_FF_P // FF_TN

NT_DIMS = (((1,), (1,)), ((), ()))


def _params(sem, vmem=VMEM_LIMIT, flags=None):
    return pltpu.CompilerParams(dimension_semantics=sem, vmem_limit_bytes=vmem, flags=flags)


def _rms_kernel(x_ref, g_ref, o_ref):
    x = x_ref[...]
    ms = jnp.mean(x * x, axis=-1, keepdims=True)
    o_ref[...] = (x * lax.rsqrt(ms + EPS) * g_ref[...]).astype(BF16)


def _rms(x, g, tm):
    m, d = x.shape
    return pl.pallas_call(
        _rms_kernel,
        grid=(m // tm,),
        in_specs=[pl.BlockSpec((tm, d), lambda i: (i, 0)), pl.BlockSpec((1, d), lambda i: (0, 0))],
        out_specs=pl.BlockSpec((tm, d), lambda i: (i, 0)),
        out_shape=jax.ShapeDtypeStruct((m, d), BF16),
        compiler_params=_params(("arbitrary",)),
        name="rms",
    )(x, g.reshape(1, d))


def _proj_kernel(x_ref, w_ref, o_ref, wbf_ref):
    @pl.when(pl.program_id(1) == 0)
    def _():
        wbf_ref[...] = w_ref[...].astype(BF16)

    o_ref[...] = lax.dot_general(x_ref[...], wbf_ref[...], NT_DIMS, preferred_element_type=F32)


def _proj(xn, wt, tm, tn, n=None, row0=0):
    m, k = xn.shape
    n = wt.shape[0] if n is None else n
    if row0 % tn == 0:
        w_spec = pl.BlockSpec((tn, k), lambda j, i: (row0 // tn + j, 0))
    else:
        assert row0 % SUBLANE == 0 and tn % SUBLANE == 0
        w_spec = pl.BlockSpec((pl.Element(tn), pl.Element(k)),
                              lambda j, i: (pl.multiple_of(row0 + j * tn, SUBLANE), 0))
    return pl.pallas_call(
        _proj_kernel,
        grid=(n // tn, m // tm),
        in_specs=[pl.BlockSpec((tm, k), lambda j, i: (i, 0)), w_spec],
        out_specs=pl.BlockSpec((tm, tn), lambda j, i: (i, j)),
        out_shape=jax.ShapeDtypeStruct((m, n), F32),
        scratch_shapes=[pltpu.VMEM((tn, k), BF16)],
        compiler_params=_params(("arbitrary", "arbitrary")),
        name="proj",
    )(xn, wt)


CLASS_DILATIONS = tuple(r for _, r in DIL_PAIRS if r > 1)


ROPE_ROWS = 128


def _rope_kernel(q_ref, k_ref, v_ref, cos_ref, sin_ref, *rest, by_class, tiles, first_kept):
    reps = DIL_W // LANE
    lane = lax.broadcasted_iota(jnp.int32, (1, DIL_W), 1)
    first_half = (lane % HEAD_DIM) < (HEAD_DIM // 2)
    tm = q_ref.shape[0]
    if by_class:
        n_class = 3 * len(CLASS_DILATIONS)
        qb_ref, kb_ref, vb_ref = rest[:3]
        class_refs = rest[3:3 + n_class]
        kt_ref, vt_ref, sq_ref, sk_ref, sv_ref = rest[3 + n_class:]
        kept = pl.program_id(0) % tiles >= first_kept
    else:
        qo_ref, ko_ref = rest

    for r0 in range(0, tm, ROPE_ROWS):
        rows = slice(r0, min(r0 + ROPE_ROWS, tm))
        cos = jnp.concatenate([cos_ref[rows, :]] * reps, axis=1)
        sin = jnp.concatenate([sin_ref[rows, :]] * reps, axis=1)

        def rot(x):
            partner = jnp.where(first_half,
                                pltpu.roll(x, DIL_W - HEAD_DIM // 2, axis=1),
                                pltpu.roll(x, HEAD_DIM // 2, axis=1))
            return x * cos + partner * sin

        q = rot(q_ref[rows, :]) * (HEAD_DIM ** -0.5)
        k = rot(k_ref[rows, :])
        if not by_class:
            qo_ref[rows, :] = q
            ko_ref[rows, :] = k
            continue
        v = v_ref[rows, :]
        qb_ref[rows, :] = q.astype(BF16)
        kb_ref[rows, :] = k.astype(BF16)
        vb_ref[rows, :] = v.astype(BF16)

        @pl.when(kept)
        def _():
            kt_ref[:, rows] = k.T
            vt_ref[:, rows] = v.T

        for src, val in ((sq_ref, q), (sk_ref, k), (sv_ref, v)):
            for j in range(reps):
                src[j, rows, :] = val[:, j * LANE:(j + 1) * LANE]
    if not by_class:
        return
    for di, r in enumerate(CLASS_DILATIONS):
        for src, dst in zip((sq_ref, sk_ref, sv_ref), class_refs[3 * di:3 * di + 3]):
            for c in range(r):
                rows = pl.ds(c, tm // r, stride=r)
                dst[c] = jnp.concatenate([src[j, rows, :] for j in range(reps)], axis=1).astype(BF16)


def _rope(proj, cos_t, sin_t, tm, n_seq=None):
    m = proj.shape[0]
    by_class = n_seq is not None
    blk = lambda c: pl.BlockSpec((tm, DIL_W), lambda i, c=c: (i, c))
    tab = pl.BlockSpec((tm, LANE), lambda i: (i, 0))
    out = pl.BlockSpec((tm, DIL_W), lambda i: (i, 0))
    tiles, first_kept, scratch = 1, 0, []
    if by_class:
        t_seq = m // n_seq
        tiles = t_seq // tm
        window = min(DIL_WINDOW, t_seq)
        first_kept = (t_seq - window) // tm
        out_specs = [out, out, out]
        out_shape = [jax.ShapeDtypeStruct((m, DIL_W), BF16)] * 3
        for r in CLASS_DILATIONS:
            spec = pl.BlockSpec((None, r, tm // r, DIL_W), lambda i: (i // tiles, 0, i % tiles, 0))
            out_specs += [spec] * 3
            out_shape += [jax.ShapeDtypeStruct((n_seq, r, t_seq // r, DIL_W), BF16)] * 3
        kept = pl.BlockSpec((None, DIL_W, tm), lambda i: (i // tiles, 0, jnp.maximum(i % tiles - first_kept, 0)))
        out_specs += [kept] * 2
        out_shape += [jax.ShapeDtypeStruct((n_seq, DIL_W, window), F32)] * 2
        scratch = [pltpu.VMEM((DIL_W // LANE, tm, LANE), F32)] * 3
    else:
        out_specs = [out, out]
        out_shape = [jax.ShapeDtypeStruct((m, DIL_W), F32)] * 2
    return pl.pallas_call(
        functools.partial(_rope_kernel, by_class=by_class, tiles=tiles, first_kept=first_kept),
        grid=(m // tm,),
        in_specs=[blk(0), blk(1), blk(2), tab, tab],
        out_specs=out_specs,
        out_shape=out_shape,
        scratch_shapes=scratch,
        compiler_params=_params(("arbitrary",)),
        name="rope",
    )(proj, proj, proj, cos_t, sin_t)


def _rope_tables(pos):
    half = HEAD_DIM // 2
    inv_freq = ROPE_THETA ** (-2.0 * jnp.arange(half, dtype=F32) / HEAD_DIM)
    ang = pos.astype(F32)[:, None] * inv_freq[None, :]
    cos = jnp.cos(ang)
    sin = jnp.sin(ang)
    cos_t = jnp.concatenate([cos, cos, cos, cos], axis=1)
    sin_t = jnp.concatenate([-sin, sin, -sin, sin], axis=1)
    return cos_t, sin_t


GLA_PAIRS = GLA_HEADS // 2


def _gla_kernel(q_ref, k_ref, v_ref, r_ref, z_ref, wg_ref, bg_ref, gn_ref, s0_ref,
                o_ref, sfin_ref, st_ref, *, chunk, n_chunks, t_valid):
    i = pl.program_id(1)
    pairs = range(GLA_PAIRS)

    @pl.when(i == 0)
    def _():
        for p in pairs:
            st_ref[p] = jnp.concatenate([s0_ref[2 * p].T, s0_ref[2 * p + 1].T], axis=1)

    row = lax.broadcasted_iota(jnp.int32, (chunk, chunk), 0)
    col = lax.broadcasted_iota(jnp.int32, (chunk, chunk), 1)
    tril = (row >= col).astype(BF16)
    t_in = lax.broadcasted_iota(jnp.int32, (chunk, 1), 0)
    both = lambda m: jnp.concatenate([m, m], axis=0)
    levels = []
    s_ = chunk
    while s_ >= 2:
        half = s_ // 2
        pivot = (row // s_) * s_ + (half - 1)
        levels.append(dict(
            size=s_,
            upto=(col <= pivot).astype(BF16),
            pair=both((row // s_ == col // s_) & (row % s_ >= half) & (col % s_ < half)),
            second=(t_in % s_) >= half))
        s_ = half
    diag = both(row == col)
    lo = lax.broadcasted_iota(jnp.int32, (1, LANE), 1) < GLA_DK
    mm = functools.partial(jnp.dot, preferred_element_type=F32)
    nt = functools.partial(lax.dot_general, dimension_numbers=NT_DIMS, preferred_element_type=F32)

    def split(a):
        hi = a.astype(BF16)
        return hi, (a - hi.astype(F32)).astype(BF16)

    def stack(a, p):
        t = a[:, p * LANE:(p + 1) * LANE]
        zero = jnp.zeros_like(t)
        return jnp.concatenate([jnp.where(lo, t, zero), jnp.where(lo, zero, t)], axis=0)

    for c in range(n_chunks):
        sl = pl.ds(c * chunk, chunk)
        z_hi, z_lo = split(z_ref[sl, :])
        x = mm(z_hi, wg_ref[0]) + mm(z_hi, wg_ref[1]) + mm(z_lo, wg_ref[0]) + bg_ref[...]
        log_a = -(jnp.maximum(-x, 0.0) + jnp.log1p(jnp.exp(-jnp.abs(x)))) / GLA_TAU
        t_abs = (i * n_chunks + c) * chunk + t_in
        log_a = jnp.where(t_abs < t_valid, log_a, 0.0)
        a_hi, a_lo = split(log_a)
        cum = mm(tril, a_hi) + mm(tril, a_lo)
        last = cum[chunk - 1:chunk, :]
        q = q_ref[sl, :] * (GLA_DK ** -0.5)
        k = k_ref[sl, :]
        q_dec = (q * jnp.exp(cum)).astype(BF16)
        k_end = (k * jnp.exp(last - cum)).astype(BF16)
        a_end = jnp.exp(last)
        heads = range(GLA_HEADS)
        vs = [slice(h * GLA_DV, (h + 1) * GLA_DV) for h in heads]
        v_f = [v_ref[sl, vs[h]] for h in heads]
        vh = [v_f[h].astype(BF16) for h in heads]
        pivot_cum = []
        for lv in levels:
            if lv["size"] >= SUBLANE:
                blocks = cum.reshape(chunk // lv["size"], lv["size"], cum.shape[1])
                pick = blocks[:, lv["size"] // 2 - 1:lv["size"] // 2, :]
                pivot_cum.append(jnp.broadcast_to(pick, blocks.shape).reshape(cum.shape))
            else:
                pivot_cum.append(mm(lv["upto"], a_hi) + mm(lv["upto"], a_lo))
        q_lv = [(q * jnp.exp(jnp.where(lv["second"], cum - pc, 0.0))).astype(BF16)
                for lv, pc in zip(levels, pivot_cum)]
        k_lv = [(k * jnp.exp(jnp.where(lv["second"], 0.0, pc - cum))).astype(BF16)
                for lv, pc in zip(levels, pivot_cum)]
        q_bf, k_bf = q.astype(BF16), k.astype(BF16)
        lanes = [slice(p * LANE, (p + 1) * LANE) for p in pairs]
        parts = [[nt(stack(ql, p), kl[:, lanes[p]]) for p in pairs] for ql, kl in zip(q_lv, k_lv)]
        scores = [jnp.where(diag, nt(stack(q_bf, p), k_bf[:, lanes[p]]), 0.0) for p in pairs]
        for lv, part in zip(levels, parts):
            scores = [jnp.where(lv["pair"], part[p], scores[p]) for p in pairs]
        scores = [scores[p].astype(BF16) for p in pairs]
        st = [st_ref[p] for p in pairs]
        carried = [nt(stack(q_dec, p), st[p].astype(BF16)) for p in pairs]
        o = [mm(scores[h // 2][(h % 2) * chunk:(h % 2 + 1) * chunk], vh[h])
             + carried[h // 2][(h % 2) * chunk:(h % 2 + 1) * chunk] for h in heads]
        for p in pairs:
            ke = k_end[:, lanes[p]]
            zero = jnp.zeros_like(ke)
            st_ref[p] = (st[p] * a_end[:, lanes[p]]
                         + mm(v_f[2 * p].T.astype(BF16), jnp.where(lo, ke, zero))
                         + mm(v_f[2 * p + 1].T.astype(BF16), jnp.where(lo, zero, ke)))
        ms = [jnp.mean(o[h] * o[h], axis=-1, keepdims=True) for h in heads]
        for h in heads:
            on = o[h] * lax.rsqrt(ms[h] + EPS) * gn_ref[...]
            half_r = 0.5 * r_ref[sl, vs[h]]
            o_ref[sl, vs[h]] = (on * (half_r + half_r * jnp.tanh(half_r))).astype(BF16)

    @pl.when(i == pl.num_programs(1) - 1)
    def _():
        for p in pairs:
            st = st_ref[p]
            sfin_ref[2 * p] = st[:, :GLA_DK].T
            sfin_ref[2 * p + 1] = st[:, GLA_DK:].T


def _gla(src, zsrc, wg, bg, gn, s0, *, chunk, n_chunks, t_valid):
    b, t, _ = src.shape
    tb = chunk * n_chunks
    col = lambda w, c: pl.BlockSpec((None, tb, w), lambda bi, i, c=c: (bi, i, c))
    full = lambda shape: pl.BlockSpec(shape, lambda bi, i: (0,) * len(shape))
    state = pl.BlockSpec((None, GLA_HEADS, GLA_DK, GLA_DV), lambda bi, i: (bi, 0, 0, 0))
    kern = functools.partial(_gla_kernel, chunk=chunk, n_chunks=n_chunks, t_valid=t_valid)
    return pl.pallas_call(
        kern,
        grid=(b, t // tb),
        in_specs=[col(GLA_QK, 0), col(GLA_QK, 1), col(GLA_VW, 1), col(GLA_VW, 2), col(Z_PAD, 0),
                  full((2, Z_PAD, GLA_QK)), full((1, GLA_QK)), full((1, GLA_DV)), state],
        out_specs=[pl.BlockSpec((None, tb, GLA_VW), lambda bi, i: (bi, i, 0)), state],
        out_shape=[jax.ShapeDtypeStruct((b, t, GLA_VW), BF16),
                   jax.ShapeDtypeStruct((b, GLA_HEADS, GLA_DK, GLA_DV), F32)],
        scratch_shapes=[pltpu.VMEM((GLA_PAIRS, GLA_DV, 2 * GLA_DK), F32)],
        compiler_params=_params(("arbitrary", "arbitrary")),
        name="gla",
    )(src, src, src, src, zsrc, wg, bg, gn, s0)


QB = 128
DIL_SUB = 2


def _dil_branch_kernel(*refs, first, final, dilation):
    if first:
        q_ref, kp_ref, kc_ref, vp_ref, vc_ref = refs[:5]
        outs = refs[5:]
    else:
        q_ref, kp_ref, kc_ref, vp_ref, vc_ref, num_ref, m_ref, d_ref = refs[:8]
        outs = refs[8:]
    i = pl.program_id(1)
    c = pl.program_id(2)
    qi = QB + lax.broadcasted_iota(jnp.int32, (QB, 1), 0)
    kj = lax.broadcasted_iota(jnp.int32, (1, 2 * QB), 1)
    off = qi - kj
    in_window = (off >= 0) & (off <= QB)
    lane = lax.broadcasted_iota(jnp.int32, (1, LANE), 1)
    lo = lane < HEAD_DIM
    heads = range(DIL_HEADS)
    cs = [slice(p * LANE, (p + 1) * LANE) for p in range(DIL_HEADS // 2)]
    sel = [lo if h % 2 == 0 else jnp.logical_not(lo) for h in heads]
    for u in range(DIL_SUB):
        blk = slice(u * QB, (u + 1) * QB)
        rows = pl.ds(dilation * u * QB + c, QB, stride=dilation) if dilation > 1 else blk
        if u == 0:
            k_before, v_before = kp_ref[...], vp_ref[...]
            valid = in_window & ((kj >= QB) | (i > 0))
        else:
            before = slice((u - 1) * QB, u * QB)
            k_before, v_before = kc_ref[before, :], vc_ref[before, :]
            valid = in_window
        k2 = jnp.concatenate([k_before, kc_ref[blk, :]], axis=0)
        v2 = jnp.concatenate([v_before, vc_ref[blk, :]], axis=0)
        q = q_ref[blk, :]
        qm = [jnp.where(sel[h], q[:, cs[h // 2]], jnp.zeros((QB, LANE), BF16)) for h in heads]
        s = [lax.dot_general(qm[h], k2[:, cs[h // 2]], NT_DIMS, preferred_element_type=F32) for h in heads]
        s = [jnp.where(valid, s[h], -jnp.inf) for h in heads]
        m_new = [jnp.max(s[h], axis=-1, keepdims=True) for h in heads]
        if not first:
            m_in = m_ref[rows, :]
            d_in = d_ref[rows, :]
            m_old = [jnp.sum(jnp.where(lane == h, m_in, 0.0), axis=-1, keepdims=True) for h in heads]
            d_old = [jnp.sum(jnp.where(lane == h, d_in, 0.0), axis=-1, keepdims=True) for h in heads]
            m_new = [jnp.maximum(m_new[h], m_old[h]) for h in heads]
        pr = [jnp.exp(s[h] - m_new[h]) for h in heads]
        d_new = [jnp.sum(pr[h], axis=-1, keepdims=True) for h in heads]
        if not first:
            alpha = [jnp.exp(m_old[h] - m_new[h]) for h in heads]
            d_new = [d_new[h] + d_old[h] * alpha[h] for h in heads]
        pv = [jnp.dot(pr[h].astype(BF16), v2[:, cs[h // 2]], preferred_element_type=F32) for h in heads]
        pairs = []
        for p in range(DIL_HEADS // 2):
            num = jnp.where(lo, pv[2 * p], pv[2 * p + 1])
            if not first:
                num = num + num_ref[p, rows, :] * jnp.where(lo, alpha[2 * p], alpha[2 * p + 1])
            if final:
                num = num / jnp.where(lo, d_new[2 * p], d_new[2 * p + 1])
            else:
                outs[0][p, rows, :] = num
            pairs.append(num)
        if final:
            outs[0][rows, :] = jnp.concatenate(pairs, axis=1).astype(BF16)
        else:
            outs[1][rows, :] = functools.reduce(jnp.add, [jnp.where(lane == h, m_new[h], 0.0) for h in heads])
            outs[2][rows, :] = functools.reduce(jnp.add, [jnp.where(lane == h, d_new[h], 0.0) for h in heads])


def _dil_branch(q, k, v, state, *, first, final):
    b, r, ts, _ = q.shape
    t = r * ts
    step = DIL_SUB * QB
    nq = ts // step
    cur = pl.BlockSpec((None, None, step, DIL_W), lambda bi, i, c: (bi, c, i, 0))
    prev = pl.BlockSpec((None, None, QB, DIL_W), lambda bi, i, c: (bi, c, jnp.maximum(DIL_SUB * i - 1, 0), 0))
    nat = lambda w: pl.BlockSpec((None, r * step, w), lambda bi, i, c: (bi, i, 0))
    nat_num = pl.BlockSpec((None, DIL_W // LANE, r * step, LANE), lambda bi, i, c: (bi, 0, i, 0))
    args = [q, k, k, v, v]
    in_specs = [cur, prev, cur, prev, cur]
    if not first:
        args += list(state)
        in_specs += [nat_num, nat(LANE), nat(LANE)]
    if final:
        assert r == 1
        out_specs = [nat(DIL_W)]
        out_shape = [jax.ShapeDtypeStruct((b, t, DIL_W), BF16)]
    else:
        out_specs = [nat_num, nat(LANE), nat(LANE)]
        out_shape = [jax.ShapeDtypeStruct((b, DIL_W // LANE, t, LANE), F32),
                     jax.ShapeDtypeStruct((b, t, LANE), F32),
                     jax.ShapeDtypeStruct((b, t, LANE), F32)]
    return pl.pallas_call(
        functools.partial(_dil_branch_kernel, first=first, final=final, dilation=r),
        grid=(b, nq, r),
        in_specs=in_specs,
        out_specs=out_specs,
        out_shape=out_shape,
        compiler_params=_params(("arbitrary", "arbitrary", "arbitrary")),
        name=f"dil_branch_r{r}",
    )(*args)


SAMPLE_HB = 6
SAMPLE_ROWS = 16


def _dil_sample_kernel(q_ref, kn_ref, vn_ref, kc_ref, vc_ref, o_ref, ko_ref, vo_ref, *, t_new, hist):
    new0 = SAMPLE_ROWS - t_new
    t_row = lax.broadcasted_iota(jnp.int32, (SAMPLE_ROWS, 1), 0)
    j = lax.broadcasted_iota(jnp.int32, (1, hist), 1)
    lane = lax.broadcasted_iota(jnp.int32, (1, LANE), 1)
    keep = lane < LANE - t_new

    def weight(off):
        w = jnp.zeros(off.shape, F32)
        for win, dil in DIL_PAIRS:
            w = w + jnp.where((off >= 0) & (off <= win) & (off % dil == 0), 1.0, 0.0)
        return w

    w_c = weight(hist + t_row - j)
    w_n = [weight(t_row - t) for t in range(t_new)]
    heads = range(SAMPLE_HB)
    hs = [slice(h * HEAD_DIM, (h + 1) * HEAD_DIM) for h in heads]
    qs = [q_ref[:, hs[h]] for h in heads]
    s_c = [jnp.dot(qs[h].astype(BF16), kc_ref[h].astype(BF16), preferred_element_type=F32) for h in heads]
    s_c = [jnp.where(w_c > 0, s, -jnp.inf) for s in s_c]
    s_n = [[jnp.where(w_n[t] > 0, jnp.sum(qs[h] * kn_ref[new0 + t:new0 + t + 1, hs[h]], axis=-1, keepdims=True), -jnp.inf)
            for t in range(t_new)] for h in heads]
    m = [functools.reduce(jnp.maximum, s_n[h], jnp.max(s_c[h], axis=-1, keepdims=True)) for h in heads]
    p_c = [w_c * jnp.exp(s_c[h] - m[h]) for h in heads]
    p_n = [[w_n[t] * jnp.exp(s_n[h][t] - m[h]) for t in range(t_new)] for h in heads]
    den = [functools.reduce(jnp.add, p_n[h], jnp.sum(p_c[h], axis=-1, keepdims=True)) for h in heads]
    num = [lax.dot_general(p_c[h].astype(BF16), vc_ref[h].astype(BF16), NT_DIMS, preferred_element_type=F32)
           for h in heads]
    num = [functools.reduce(jnp.add, [p_n[h][t] * vn_ref[new0 + t:new0 + t + 1, hs[h]] for t in range(t_new)], num[h])
           for h in heads]
    o_ref[...] = jnp.concatenate([num[h] / den[h] for h in heads], axis=1)

    blank = jnp.zeros((LANE - SAMPLE_ROWS, HEAD_DIM), F32)
    for h in heads:
        for src, new_ref, dst in ((kc_ref, kn_ref, ko_ref), (vc_ref, vn_ref, vo_ref)):
            new_tile = jnp.concatenate([blank, new_ref[:, hs[h]]], axis=0).T
            nxt = pltpu.roll(src[h, :, 0:LANE], LANE - t_new, axis=1)
            for c in range(hist // LANE):
                cur_t = nxt
                if c + 1 < hist // LANE:
                    nxt = pltpu.roll(src[h, :, (c + 1) * LANE:(c + 2) * LANE], LANE - t_new, axis=1)
                else:
                    nxt = new_tile
                dst[h, :, c * LANE:(c + 1) * LANE] = jnp.where(keep, cur_t, nxt)


def _dil_sample(q, kn, vn, cache_kt, cache_vt, t_new):
    b, nh, _, hist = cache_kt.shape
    w = SAMPLE_HB * HEAD_DIM
    small = pl.BlockSpec((None, SAMPLE_ROWS, w), lambda bi, c: (bi, 0, c))
    big = pl.BlockSpec((None, SAMPLE_HB, HEAD_DIM, hist), lambda bi, c: (bi, c, 0, 0))
    return pl.pallas_call(
        functools.partial(_dil_sample_kernel, t_new=t_new, hist=hist),
        grid=(b, nh // SAMPLE_HB),
        in_specs=[small, small, small, big, big],
        out_specs=[small, big, big],
        out_shape=[jax.ShapeDtypeStruct((b, SAMPLE_ROWS, nh * HEAD_DIM), F32),
                   jax.ShapeDtypeStruct(cache_kt.shape, F32),
                   jax.ShapeDtypeStruct(cache_vt.shape, F32)],
        compiler_params=_params(("arbitrary", "arbitrary")),
        name="dil_sample",
    )(q, kn, vn, cache_kt, cache_vt)


OUT_ROWS = 256

STRIDE_ROWS = 256
STRIDE_GROUPS = STRIDE_ROWS // SUBLANE


def _to_strided(val, stage_ref):
    for j in range(val.shape[1] // LANE):
        for s_ in range(SUBLANE):
            stage_ref[j, pl.ds(s_, STRIDE_GROUPS, stride=SUBLANE), :] = (
                val[s_ * STRIDE_GROUPS:(s_ + 1) * STRIDE_GROUPS, j * LANE:(j + 1) * LANE])
    return jnp.concatenate([stage_ref[j] for j in range(val.shape[1] // LANE)], axis=1)


def _from_strided(val, stage_ref):
    for j in range(val.shape[1] // LANE):
        stage_ref[j] = val[:, j * LANE:(j + 1) * LANE]
    return jnp.concatenate(
        [jnp.concatenate([stage_ref[j, pl.ds(s_, STRIDE_GROUPS, stride=SUBLANE), :]
                          for j in range(val.shape[1] // LANE)], axis=1) for s_ in range(SUBLANE)], axis=0)


def _out_proj_kernel(oa_ref, ob_ref, w_ref, x_ref, g_ref, h_ref, xn_ref, wbf_ref, *stage, strided):
    @pl.when(pl.program_id(0) == 0)
    def _():
        wbf_ref[...] = w_ref[...].astype(BF16)

    for r0 in range(0, x_ref.shape[0], OUT_ROWS):
        rows = slice(r0, min(r0 + OUT_ROWS, x_ref.shape[0]))
        y = (jnp.dot(oa_ref[rows, :], wbf_ref[:GLA_VW, :], preferred_element_type=F32)
             + jnp.dot(ob_ref[rows, :], wbf_ref[GLA_VW:, :], preferred_element_type=F32))
        h = x_ref[rows, :] + y
        h_ref[rows, :] = h
        ms = jnp.mean(h * h, axis=-1, keepdims=True)
        xn = h * lax.rsqrt(ms + EPS) * g_ref[...]
        if strided:
            xn = _to_strided(xn, stage[0])
        xn_ref[rows, :] = xn.astype(BF16)


def _out_proj(oa, ob, w, x, g, tm, strided=False):
    m, d = x.shape
    assert not strided or OUT_ROWS == STRIDE_ROWS
    row = lambda width: pl.BlockSpec((tm, width), lambda i: (i, 0))
    g2 = g.reshape(1, d)
    scratch = [pltpu.VMEM(w.shape, BF16)]
    if strided:
        scratch.append(pltpu.VMEM((d // LANE, STRIDE_ROWS, LANE), F32))
    return pl.pallas_call(
        functools.partial(_out_proj_kernel, strided=strided),
        grid=(m // tm,),
        in_specs=[row(GLA_VW), row(DIL_W),
                  pl.BlockSpec(w.shape, lambda i: (0, 0), pipeline_mode=pl.Buffered(1)),
                  row(d), pl.BlockSpec(g2.shape, lambda i: (0, 0))],
        out_specs=[row(d), row(d)],
        out_shape=[jax.ShapeDtypeStruct((m, d), F32), jax.ShapeDtypeStruct((m, d), BF16)],
        scratch_shapes=scratch,
        compiler_params=_params(("arbitrary",)),
        name="out_proj",
    )(oa, ob, w, x, g2)


FF_ROWS = STRIDE_ROWS
FF_TAIL = 2 * SUBLANE
FF_SUB = FF_TN // LANE
FF_TILES = D_FF // LANE


def _conv_taps(cw, cb, up, prev1, prev2):
    return cb + cw[0:1] * prev2 + cw[1:2] * prev1 + cw[2:3] * up


def _swiglu(u, g):
    h = 0.5 * g
    return ((h + h * jnp.tanh(h)) * u).astype(BF16)


def _stage_up_weights(w_blocks, wbf_ref, j):
    for half in range(2):
        for q in range(FF_SUB):
            blk = w_blocks[half * FF_SUB + q][...]
            if (FF_NT - 1) * FF_SUB + q >= FF_TILES:
                blk = jnp.where(j * FF_SUB + q < FF_TILES, blk, 0.0)
            wbf_ref[half, :, q * LANE:(q + 1) * LANE] = blk.astype(BF16)


def _up_weight_specs(tile_of_step):
    specs = []
    for half in range(2):
        for q in range(FF_SUB):
            def index(*ids, half=half, q=q):
                blk = jnp.minimum(tile_of_step(*ids) * FF_SUB + q, FF_TILES - 1)
                return (0, half * FF_TILES + blk)
            specs.append(pl.BlockSpec((D_MODEL, LANE), index))
    return specs


def _ffn_up_seq_kernel(*refs, tiles_per_seq):
    x_ref = refs[0]
    w_blocks = refs[1:1 + 2 * FF_SUB]
    cwu_ref, cwg_ref, cbu_ref, cbg_ref, act_ref, tu_ref, tg_ref, wbf_ref, carry_ref = refs[1 + 2 * FF_SUB:]
    j = pl.program_id(0)
    i = pl.program_id(1)
    tm = x_ref.shape[0]

    @pl.when(i == 0)
    def _():
        _stage_up_weights(w_blocks, wbf_ref, j)

    @pl.when(i % tiles_per_seq == 0)
    def _():
        carry_ref[...] = jnp.zeros_like(carry_ref)

    carry = [carry_ref[0], carry_ref[1]]
    shift = lambda before, cur: pltpu.roll(jnp.concatenate([before, cur], axis=0), 1, axis=0)[SUBLANE:]
    for r0 in range(0, tm, FF_ROWS):
        x = x_ref[r0:r0 + FF_ROWS, :]
        conv = []
        for idx, (cw_ref, cb_ref) in enumerate(((cwu_ref, cbu_ref), (cwg_ref, cbg_ref))):
            up = jnp.dot(x, wbf_ref[idx], preferred_element_type=F32)
            tail = up[FF_ROWS - FF_TAIL:]
            p_a = shift(carry[idx][:SUBLANE], tail[:SUBLANE])
            p_b = shift(carry[idx][SUBLANE:], tail[SUBLANE:])
            prev1 = jnp.concatenate([p_b, up[:FF_ROWS - SUBLANE]], axis=0)
            prev2 = jnp.concatenate([p_a, p_b, up[:FF_ROWS - 2 * SUBLANE]], axis=0)
            carry[idx] = tail
            conv.append(_conv_taps(cw_ref[...], cb_ref[...], up, prev1, prev2))
        act_ref[r0:r0 + FF_ROWS, :] = _swiglu(*conv)
    for idx, tail_ref in enumerate((tu_ref, tg_ref)):
        carry_ref[idx] = carry[idx]
        tail_ref[...] = carry[idx]


def _ffn_up_seq(xn, w_up, conv_w, conv_b, tm, t_seq):
    m, d = xn.shape
    tiles_per_seq = t_seq // tm
    xs = pl.BlockSpec((tm, d), lambda j, i: (i, 0))
    cu = lambda r: pl.BlockSpec((r, FF_TN), lambda j, i: (0, j))
    cg = lambda r: pl.BlockSpec((r, FF_TN), lambda j, i: (0, FF_NT + j))
    act = pl.BlockSpec((tm, FF_TN), lambda j, i: (i, j))
    tail = pl.BlockSpec((None, FF_TAIL, FF_TN), lambda j, i: (i // tiles_per_seq, 0, j))
    w_specs = _up_weight_specs(lambda j, i: j)
    return pl.pallas_call(
        functools.partial(_ffn_up_seq_kernel, tiles_per_seq=tiles_per_seq),
        grid=(FF_NT, m // tm),
        in_specs=[xs] + w_specs + [cu(CONV_W), cg(CONV_W), cu(1), cg(1)],
        out_specs=[act, tail, tail],
        out_shape=[jax.ShapeDtypeStruct((m, D_FF_P), BF16)]
        + [jax.ShapeDtypeStruct((m // t_seq, FF_TAIL, D_FF_P), F32)] * 2,
        scratch_shapes=[pltpu.VMEM((2, d, FF_TN), BF16), pltpu.VMEM((2, FF_TAIL, FF_TN), F32)],
        compiler_params=_params(("arbitrary", "arbitrary")),
        name="ffn_up",
    )(xn, *([w_up] * len(w_specs)), conv_w, conv_w, conv_b, conv_b)


def _ffn_up_hist_kernel(*refs, n_seq):
    x_ref = refs[0]
    w_blocks = refs[1:1 + 2 * FF_SUB]
    cwu_ref, cwg_ref, cbu_ref, cbg_ref, hu_ref, hg_ref, act_ref, upu_ref, upg_ref, wbf_ref = refs[1 + 2 * FF_SUB:]
    _stage_up_weights(w_blocks, wbf_ref, pl.program_id(0))
    x = x_ref[...]
    m = x.shape[0]
    conv = []
    for idx, (cw_ref, cb_ref, h_ref, up_ref) in enumerate((
            (cwu_ref, cbu_ref, hu_ref, upu_ref), (cwg_ref, cbg_ref, hg_ref, upg_ref))):
        up = jnp.dot(x, wbf_ref[idx], preferred_element_type=F32)
        full = jnp.concatenate([h_ref[...], up], axis=0)
        prev1 = full[n_seq:n_seq + m]
        prev2 = full[:m]
        up_ref[...] = up
        conv.append(_conv_taps(cw_ref[...], cb_ref[...], up, prev1, prev2))
    act_ref[...] = _swiglu(*conv)


def _ffn_up_hist(xn, w_up, conv_w, conv_b, hist, n_seq):
    m, d = xn.shape
    assert n_seq % SUBLANE == 0
    xs = pl.BlockSpec((m, d), lambda j: (0, 0))
    u = lambda r: pl.BlockSpec((r, FF_TN), lambda j: (0, j))
    g = lambda r: pl.BlockSpec((r, FF_TN), lambda j: (0, FF_NT + j))
    w_specs = _up_weight_specs(lambda j: j)
    nh = hist.shape[0]
    return pl.pallas_call(
        functools.partial(_ffn_up_hist_kernel, n_seq=n_seq),
        grid=(FF_NT,),
        in_specs=[xs] + w_specs + [u(CONV_W), g(CONV_W), u(1), g(1), u(nh), g(nh)],
        out_specs=[u(m), u(m), u(m)],
        out_shape=[jax.ShapeDtypeStruct((m, D_FF_P), BF16)] + [jax.ShapeDtypeStruct((m, D_FF_P), F32)] * 2,
        scratch_shapes=[pltpu.VMEM((2, d, FF_TN), BF16)],
        compiler_params=_params(("arbitrary",)),
        name="ffn_up_hist",
    )(xn, *([w_up] * len(w_specs)), conv_w, conv_w, conv_b, conv_b, hist, hist)


DOWN_ROWS = 256


def _ffn_down_kernel(a_ref, w_ref, h_ref, g_ref, y_ref, *stage, strided):
    kdim = w_ref.shape[0]
    for r0 in range(0, a_ref.shape[0], DOWN_ROWS):
        rows = slice(r0, min(r0 + DOWN_ROWS, a_ref.shape[0]))
        f = jnp.dot(a_ref[rows, :kdim], w_ref[...], preferred_element_type=F32)
        if strided:
            f = _from_strided(f, stage[0])
        h = h_ref[rows, :] + f
        ms = jnp.mean(h * h, axis=-1, keepdims=True)
        y_ref[rows, :] = h * lax.rsqrt(ms + EPS) * g_ref[...]


def _ffn_down(act, w, h, g, tm, strided=False):
    m, d = h.shape
    assert not strided or DOWN_ROWS == STRIDE_ROWS
    scratch = [pltpu.VMEM((d // LANE, STRIDE_ROWS, LANE), F32)] if strided else []
    return pl.pallas_call(
        functools.partial(_ffn_down_kernel, strided=strided),
        grid=(m // tm,),
        in_specs=[pl.BlockSpec((tm, act.shape[1]), lambda i: (i, 0)),
                  pl.BlockSpec(w.shape, lambda i: (0, 0), pipeline_mode=pl.Buffered(1)),
                  pl.BlockSpec((tm, d), lambda i: (i, 0)),
                  pl.BlockSpec((1, d), lambda i: (0, 0))],
        out_specs=pl.BlockSpec((tm, d), lambda i: (i, 0)),
        out_shape=jax.ShapeDtypeStruct((m, d), F32),
        scratch_shapes=scratch,
        compiler_params=_params(("arbitrary",)),
        name="ffn_down",
    )(act, w, h, g.reshape(1, d))


def _pad_cols(a, width):
    return jnp.pad(a, ((0, 0), (0, width - a.shape[1])))


def _split_pad_ff(a):
    return jnp.concatenate([_pad_cols(a[:, :D_FF], D_FF_P), _pad_cols(a[:, D_FF:], D_FF_P)], axis=1)


def _unpad_ff(u, g):
    return jnp.concatenate([u[..., :D_FF], g[..., :D_FF]], axis=-1)


def kernel(x_prompt, x_sample, state_gla, cache_dil_k, cache_dil_v, state_ffn_conv, norm_mix, w_in, w_gate_up,
           b_gate, gla_norm, w_out, norm_ffn, w_ffn_up, ffn_conv_w, ffn_conv_b, w_ffn_down, norm_final):
    bp, tp, d = x_prompt.shape
    bs, ts, _ = x_sample.shape
    l = 0

    z0 = 2 * GLA_QK + 2 * GLA_VW
    w_in_t = w_in[l].T
    w_z_t = jnp.pad(w_in_t[z0:z0 + GLA_RANK], ((0, Z_PAD - GLA_RANK), (0, 0)))
    wg_f = jnp.pad(w_gate_up[l], ((0, Z_PAD - GLA_RANK), (0, 0)))
    wg_hi = wg_f.astype(BF16)
    wg_p = jnp.stack([wg_hi, (wg_f - wg_hi.astype(F32)).astype(BF16)])
    bg = b_gate[l].reshape(1, GLA_QK)
    gn = gla_norm[l].reshape(1, GLA_DV)
    cw_p = _split_pad_ff(ffn_conv_w[l])
    cb_p = _split_pad_ff(ffn_conv_b[l].reshape(1, -1))
    w_dn = w_ffn_down[l].astype(BF16)

    mp = bp * tp
    xp = x_prompt.reshape(mp, d)
    xn_mix_p = _rms(xp, norm_mix[l], 512)
    mix_p = _proj(xn_mix_p, w_in_t, 2048, 768, n=z0)
    dil_p = _proj(xn_mix_p, w_in_t, 2048, 768, n=3 * DIL_W, row0=z0 + GLA_RANK)
    z_p = _proj(xn_mix_p, w_z_t, 2048, Z_PAD)
    cos_p, sin_p = _rope_tables(jnp.arange(tp))
    cos_p = jnp.tile(cos_p, (bp, 1))
    sin_p = jnp.tile(sin_p, (bp, 1))
    rope_out = _rope(dil_p, cos_p, sin_p, 512, n_seq=bp)
    by_class = {1: tuple(a.reshape(bp, 1, tp, DIL_W) for a in rope_out[:3])}
    for di, r in enumerate(CLASS_DILATIONS):
        by_class[r] = tuple(rope_out[3 + 3 * di:6 + 3 * di])
    k_kept_t, v_kept_t = rope_out[-2:]

    oa_p, gla_p = _gla(mix_p.reshape(bp, tp, z0), z_p.reshape(bp, tp, Z_PAD), wg_p, bg, gn,
                       jnp.zeros((bp, GLA_HEADS, GLA_DK, GLA_DV), F32),
                       chunk=GLA_ROWS, n_chunks=2, t_valid=tp)

    state = _dil_branch(*by_class[16], None, first=True, final=False)
    state = _dil_branch(*by_class[4], state, first=False, final=False)
    (ob_p,) = _dil_branch(*by_class[1], state, first=False, final=True)

    h_p, xn_p = _out_proj(oa_p.reshape(mp, GLA_VW), ob_p.reshape(mp, DIL_W), w_out[l], xp, norm_ffn[l], 512, strided=True)
    act_p, tail_u, tail_g = _ffn_up_seq(xn_p, w_ffn_up[l], cw_p, cb_p, 2048, tp)
    y_p = _ffn_down(act_p, w_dn, h_p, norm_final, 512, strided=True)

    buf_p = min(DIL_WINDOW, tp)
    y_prompt = y_p.reshape(bp, tp, d)
    new_gla_p = gla_p[None]
    new_k_p = k_kept_t.reshape(bp, DIL_HEADS, HEAD_DIM, buf_p).transpose(0, 3, 1, 2)[None]
    new_v_p = v_kept_t.reshape(bp, DIL_HEADS, HEAD_DIM, buf_p).transpose(0, 3, 1, 2)[None]
    last_two = lambda a: a[:, SUBLANE - 1::SUBLANE]
    new_conv_p = _unpad_ff(last_two(tail_u), last_two(tail_g))[None]

    ms_ = bs * ts
    xs = x_sample.reshape(ms_, d)
    xn_mix_s = _rms(xs, norm_mix[l], ms_)
    mix_s = _proj(xn_mix_s, w_in_t, ms_, 768, n=z0)
    dil_s = _proj(xn_mix_s, w_in_t, ms_, 768, n=3 * DIL_W, row0=z0 + GLA_RANK)
    z_s = _proj(xn_mix_s, w_z_t, ms_, Z_PAD)
    cos_s, sin_s = _rope_tables(PAST_LEN + jnp.arange(ts))
    cos_s = jnp.tile(cos_s, (bs, 1))
    sin_s = jnp.tile(sin_s, (bs, 1))
    q_rs, k_rs = _rope(dil_s, cos_s, sin_s, ms_)

    pad_chunk = lambda a: jnp.pad(a.reshape(bs, ts, -1), ((0, 0), (0, GLA_ROWS_SHORT - ts), (0, 0)))
    oa_s, gla_s = _gla(pad_chunk(mix_s), pad_chunk(z_s), wg_p, bg, gn, state_gla[l],
                       chunk=GLA_ROWS_SHORT, n_chunks=1, t_valid=ts)
    oa_s = oa_s[:, :ts].reshape(ms_, GLA_VW)

    v_s = dil_s[:, 2 * DIL_W:]
    rows_first = lambda a: jnp.pad(a.reshape(bs, ts, DIL_W), ((0, 0), (0, SAMPLE_ROWS - ts), (0, 0)))
    rows_last = lambda a: jnp.pad(a.reshape(bs, ts, DIL_W), ((0, 0), (SAMPLE_ROWS - ts, 0), (0, 0)))
    time_minor = lambda a: a.transpose(0, 2, 3, 1)
    ob_s, k_s_t, v_s_t = _dil_sample(rows_first(q_rs), rows_last(k_rs), rows_last(v_s),
                                     time_minor(cache_dil_k[l]), time_minor(cache_dil_v[l]), ts)
    ob_s = ob_s[:, :ts].astype(BF16)

    h_s, xn_s = _out_proj(oa_s, ob_s.reshape(ms_, DIL_W), w_out[l], xs, norm_ffn[l], ms_)
    time_major = lambda a: a.reshape(bs, -1, a.shape[-1]).transpose(1, 0, 2).reshape(-1, a.shape[-1])
    batch_major = lambda a: a.reshape(-1, bs, a.shape[-1]).transpose(1, 0, 2)
    conv_hist = time_major(_split_pad_ff(state_ffn_conv[l].reshape(bs * (CONV_W - 1), 2 * D_FF)))
    act_s, up_u, up_g = _ffn_up_hist(time_major(xn_s), w_ffn_up[l], cw_p, cb_p, conv_hist, bs)
    act_s = batch_major(act_s).reshape(ms_, D_FF_P)
    y_s = _ffn_down(act_s, w_dn, h_s, norm_final, ms_)

    y_sample = y_s.reshape(bs, ts, d)
    new_gla_s = gla_s[None]
    new_k_s = k_s_t.transpose(0, 3, 1, 2)[None]
    new_v_s = v_s_t.transpose(0, 3, 1, 2)[None]
    new_conv_s = batch_major(_unpad_ff(up_u, up_g))[None, :, ts - (CONV_W - 1):]

    return (y_prompt, y_sample, new_gla_p, new_gla_s, new_k_p, new_k_s, new_v_p, new_v_s,
            new_conv_p, new_conv_s)
```

```python
import functools

import jax
import jax.numpy as jnp
from jax import lax
from jax.experimental import pallas as pl
from jax.experimental.pallas import tpu as pltpu

F32 = jnp.float32
BF16 = jnp.bfloat16

D_MODEL = 2048
HEAD_DIM = 64
GLA_HEADS = 10
GLA_DK = 64
GLA_DV = 128
GLA_RANK = 16
GLA_TAU = 16.0
GLA_ROWS = 128
GLA_ROWS_SHORT = 16
DIL_HEADS = 12
DIL_PAIRS = ((128, 1), (512, 4), (2048, 16))
DIL_WINDOW = 2048
ROPE_THETA = 10000.0
D_FF = 5504
PAST_LEN = 8192
CONV_W = 3
EPS = 1e-6

GLA_QK = GLA_HEADS * GLA_DK
GLA_VW = GLA_HEADS * GLA_DV
DIL_W = DIL_HEADS * HEAD_DIM

LANE = 128
SUBLANE = 8
VMEM_LIMIT = 56 * 1024 * 1024

Z_PAD = LANE
D_FF_P = 5632
FF_TN = 512
FF_NT = D_FF_P // FF_TN

NT_DIMS = (((1,), (1,)), ((), ()))


def _params(sem, vmem=VMEM_LIMIT, flags=None):
    return pltpu.CompilerParams(dimension_semantics=sem, vmem_limit_bytes=vmem, flags=flags)


def _rms_kernel(x_ref, g_ref, o_ref):
    x = x_ref[...]
    ms = jnp.mean(x * x, axis=-1, keepdims=True)
    o_ref[...] = (x * lax.rsqrt(ms + EPS) * g_ref[...]).astype(BF16)


def _rms(x, g, tm):
    m, d = x.shape
    return pl.pallas_call(
        _rms_kernel,
        grid=(m // tm,),
        in_specs=[pl.BlockSpec((tm, d), lambda i: (i, 0)), pl.BlockSpec((1, d), lambda i: (0, 0))],
        out_specs=pl.BlockSpec((tm, d), lambda i: (i, 0)),
        out_shape=jax.ShapeDtypeStruct((m, d), BF16),
        compiler_params=_params(("arbitrary",)),
        name="rms",
    )(x, g.reshape(1, d))


def _proj_kernel(x_ref, w_ref, o_ref, wbf_ref):
    @pl.when(pl.program_id(1) == 0)
    def _():
        wbf_ref[...] = w_ref[...].astype(BF16)

    o_ref[...] = lax.dot_general(x_ref[...], wbf_ref[...], NT_DIMS, preferred_element_type=F32)


def _proj(xn, wt, tm, tn, n=None, row0=0):
    m, k = xn.shape
    n = wt.shape[0] if n is None else n
    if row0 % tn == 0:
        w_spec = pl.BlockSpec((tn, k), lambda j, i: (row0 // tn + j, 0))
    else:
        assert row0 % SUBLANE == 0 and tn % SUBLANE == 0
        w_spec = pl.BlockSpec((pl.Element(tn), pl.Element(k)),
                              lambda j, i: (pl.multiple_of(row0 + j * tn, SUBLANE), 0))
    return pl.pallas_call(
        _proj_kernel,
        grid=(n // tn, m // tm),
        in_specs=[pl.BlockSpec((tm, k), lambda j, i: (i, 0)), w_spec],
        out_specs=pl.BlockSpec((tm, tn), lambda j, i: (i, j)),
        out_shape=jax.ShapeDtypeStruct((m, n), F32),
        scratch_shapes=[pltpu.VMEM((tn, k), BF16)],
        compiler_params=_params(("arbitrary", "arbitrary")),
        name="proj",
    )(xn, wt)


CLASS_DILATIONS = tuple(r for _, r in DIL_PAIRS if r > 1)


ROPE_ROWS = 128


def _rope_kernel(q_ref, k_ref, v_ref, cos_ref, sin_ref, *rest, by_class, tiles, first_kept):
    reps = DIL_W // LANE
    lane = lax.broadcasted_iota(jnp.int32, (1, DIL_W), 1)
    first_half = (lane % HEAD_DIM) < (HEAD_DIM // 2)
    tm = q_ref.shape[0]
    if by_class:
        n_class = 3 * len(CLASS_DILATIONS)
        qb_ref, kb_ref, vb_ref = rest[:3]
        class_refs = rest[3:3 + n_class]
        kt_ref, vt_ref, sq_ref, sk_ref, sv_ref = rest[3 + n_class:]
        kept = pl.program_id(0) % tiles >= first_kept
    else:
        qo_ref, ko_ref = rest

    for r0 in range(0, tm, ROPE_ROWS):
        rows = slice(r0, min(r0 + ROPE_ROWS, tm))
        cos = jnp.concatenate([cos_ref[rows, :]] * reps, axis=1)
        sin = jnp.concatenate([sin_ref[rows, :]] * reps, axis=1)

        def rot(x):
            partner = jnp.where(first_half,
                                pltpu.roll(x, DIL_W - HEAD_DIM // 2, axis=1),
                                pltpu.roll(x, HEAD_DIM // 2, axis=1))
            return x * cos + partner * sin

        q = rot(q_ref[rows, :]) * (HEAD_DIM ** -0.5)
        k = rot(k_ref[rows, :])
        if not by_class:
            qo_ref[rows, :] = q
            ko_ref[rows, :] = k
            continue
        v = v_ref[rows, :]
        qb_ref[rows, :] = q.astype(BF16)
        kb_ref[rows, :] = k.astype(BF16)
        vb_ref[rows, :] = v.astype(BF16)

        @pl.when(kept)
        def _():
            kt_ref[:, rows] = k.T
            vt_ref[:, rows] = v.T

        for src, val in ((sq_ref, q), (sk_ref, k), (sv_ref, v)):
            for j in range(reps):
                src[j, rows, :] = val[:, j * LANE:(j + 1) * LANE]
    if not by_class:
        return
    for di, r in enumerate(CLASS_DILATIONS):
        for src, dst in zip((sq_ref, sk_ref, sv_ref), class_refs[3 * di:3 * di + 3]):
            for c in range(r):
                rows = pl.ds(c, tm // r, stride=r)
                dst[c] = jnp.concatenate([src[j, rows, :] for j in range(reps)], axis=1).astype(BF16)


def _rope(proj, cos_t, sin_t, tm, n_seq=None):
    m = proj.shape[0]
    by_class = n_seq is not None
    blk = lambda c: pl.BlockSpec((tm, DIL_W), lambda i, c=c: (i, c))
    tab = pl.BlockSpec((tm, LANE), lambda i: (i, 0))
    out = pl.BlockSpec((tm, DIL_W), lambda i: (i, 0))
    tiles, first_kept, scratch = 1, 0, []
    if by_class:
        t_seq = m // n_seq
        tiles = t_seq // tm
        window = min(DIL_WINDOW, t_seq)
        first_kept = (t_seq - window) // tm
        out_specs = [out, out, out]
        out_shape = [jax.ShapeDtypeStruct((m, DIL_W), BF16)] * 3
        for r in CLASS_DILATIONS:
            spec = pl.BlockSpec((None, r, tm // r, DIL_W), lambda i: (i // tiles, 0, i % tiles, 0))
            out_specs += [spec] * 3
            out_shape += [jax.ShapeDtypeStruct((n_seq, r, t_seq // r, DIL_W), BF16)] * 3
        kept = pl.BlockSpec((None, DIL_W, tm), lambda i: (i // tiles, 0, jnp.maximum(i % tiles - first_kept, 0)))
        out_specs += [kept] * 2
        out_shape += [jax.ShapeDtypeStruct((n_seq, DIL_W, window), F32)] * 2
        scratch = [pltpu.VMEM((DIL_W // LANE, tm, LANE), F32)] * 3
    else:
        out_specs = [out, out]
        out_shape = [jax.ShapeDtypeStruct((m, DIL_W), F32)] * 2
    return pl.pallas_call(
        functools.partial(_rope_kernel, by_class=by_class, tiles=tiles, first_kept=first_kept),
        grid=(m // tm,),
        in_specs=[blk(0), blk(1), blk(2), tab, tab],
        out_specs=out_specs,
        out_shape=out_shape,
        scratch_shapes=scratch,
        compiler_params=_params(("arbitrary",)),
        name="rope",
    )(proj, proj, proj, cos_t, sin_t)


def _rope_tables(pos):
    half = HEAD_DIM // 2
    inv_freq = ROPE_THETA ** (-2.0 * jnp.arange(half, dtype=F32) / HEAD_DIM)
    ang = pos.astype(F32)[:, None] * inv_freq[None, :]
    cos = jnp.cos(ang)
    sin = jnp.sin(ang)
    cos_t = jnp.concatenate([cos, cos, cos, cos], axis=1)
    sin_t = jnp.concatenate([-sin, sin, -sin, sin], axis=1)
    return cos_t, sin_t


GLA_PAIRS = GLA_HEADS // 2


def _gla_kernel(q_ref, k_ref, v_ref, r_ref, z_ref, wg_ref, bg_ref, gn_ref, s0_ref,
                o_ref, sfin_ref, st_ref, *, chunk, n_chunks, t_valid):
    i = pl.program_id(1)
    pairs = range(GLA_PAIRS)

    @pl.when(i == 0)
    def _():
        for p in pairs:
            st_ref[p] = jnp.concatenate([s0_ref[2 * p].T, s0_ref[2 * p + 1].T], axis=1)

    row = lax.broadcasted_iota(jnp.int32, (chunk, chunk), 0)
    col = lax.broadcasted_iota(jnp.int32, (chunk, chunk), 1)
    tril = (row >= col).astype(BF16)
    t_in = lax.broadcasted_iota(jnp.int32, (chunk, 1), 0)
    both = lambda m: jnp.concatenate([m, m], axis=0)
    levels = []
    s_ = chunk
    while s_ >= 2:
        half = s_ // 2
        pivot = (row // s_) * s_ + (half - 1)
        levels.append(dict(
            size=s_,
            upto=(col <= pivot).astype(BF16),
            pair=both((row // s_ == col // s_) & (row % s_ >= half) & (col % s_ < half)),
            second=(t_in % s_) >= half))
        s_ = half
    diag = both(row == col)
    lo = lax.broadcasted_iota(jnp.int32, (1, LANE), 1) < GLA_DK
    mm = functools.partial(jnp.dot, preferred_element_type=F32)
    nt = functools.partial(lax.dot_general, dimension_numbers=NT_DIMS, preferred_element_type=F32)

    def split(a):
        hi = a.astype(BF16)
        return hi, (a - hi.astype(F32)).astype(BF16)

    def stack(a, p):
        t = a[:, p * LANE:(p + 1) * LANE]
        zero = jnp.zeros_like(t)
        return jnp.concatenate([jnp.where(lo, t, zero), jnp.where(lo, zero, t)], axis=0)

    for c in range(n_chunks):
        sl = pl.ds(c * chunk, chunk)
        z_hi, z_lo = split(z_ref[sl, :])
        x = mm(z_hi, wg_ref[0]) + mm(z_hi, wg_ref[1]) + mm(z_lo, wg_ref[0]) + bg_ref[...]
        log_a = -(jnp.maximum(-x, 0.0) + jnp.log1p(jnp.exp(-jnp.abs(x)))) / GLA_TAU
        t_abs = (i * n_chunks + c) * chunk + t_in
        log_a = jnp.where(t_abs < t_valid, log_a, 0.0)
        a_hi, a_lo = split(log_a)
        cum = mm(tril, a_hi) + mm(tril, a_lo)
        last = cum[chunk - 1:chunk, :]
        q = q_ref[sl, :] * (GLA_DK ** -0.5)
        k = k_ref[sl, :]
        q_dec = (q * jnp.exp(cum)).astype(BF16)
        k_end = (k * jnp.exp(last - cum)).astype(BF16)
        a_end = jnp.exp(last)
        heads = range(GLA_HEADS)
        vs = [slice(h * GLA_DV, (h + 1) * GLA_DV) for h in heads]
        v_f = [v_ref[sl, vs[h]] for h in heads]
        vh = [v_f[h].astype(BF16) for h in heads]
        pivot_cum = []
        for lv in levels:
            if lv["size"] >= SUBLANE:
                blocks = cum.reshape(chunk // lv["size"], lv["size"], cum.shape[1])
                pick = blocks[:, lv["size"] // 2 - 1:lv["size"] // 2, :]
                pivot_cum.append(jnp.broadcast_to(pick, blocks.shape).reshape(cum.shape))
            else:
                pivot_cum.append(mm(lv["upto"], a_hi) + mm(lv["upto"], a_lo))
        q_lv = [(q * jnp.exp(jnp.where(lv["second"], cum - pc, 0.0))).astype(BF16)
                for lv, pc in zip(levels, pivot_cum)]
        k_lv = [(k * jnp.exp(jnp.where(lv["second"], 0.0, pc - cum))).astype(BF16)
                for lv, pc in zip(levels, pivot_cum)]
        q_bf, k_bf = q.astype(BF16), k.astype(BF16)
        lanes = [slice(p * LANE, (p + 1) * LANE) for p in pairs]
        parts = [[nt(stack(ql, p), kl[:, lanes[p]]) for p in pairs] for ql, kl in zip(q_lv, k_lv)]
        scores = [jnp.where(diag, nt(stack(q_bf, p), k_bf[:, lanes[p]]), 0.0) for p in pairs]
        for lv, part in zip(levels, parts):
            scores = [jnp.where(lv["pair"], part[p], scores[p]) for p in pairs]
        scores = [scores[p].astype(BF16) for p in pairs]
        st = [st_ref[p] for p in pairs]
        carried = [nt(stack(q_dec, p), st[p].astype(BF16)) for p in pairs]
        o = [mm(scores[h // 2][(h % 2) * chunk:(h % 2 + 1) * chunk], vh[h])
             + carried[h // 2][(h % 2) * chunk:(h % 2 + 1) * chunk] for h in heads]
        for p in pairs:
            ke = k_end[:, lanes[p]]
            zero = jnp.zeros_like(ke)
            st_ref[p] = (st[p] * a_end[:, lanes[p]]
                         + mm(v_f[2 * p].T.astype(BF16), jnp.where(lo, ke, zero))
                         + mm(v_f[2 * p + 1].T.astype(BF16), jnp.where(lo, zero, ke)))
        ms = [jnp.mean(o[h] * o[h], axis=-1, keepdims=True) for h in heads]
        for h in heads:
            on = o[h] * lax.rsqrt(ms[h] + EPS) * gn_ref[...]
            half_r = 0.5 * r_ref[sl, vs[h]]
            o_ref[sl, vs[h]] = (on * (half_r + half_r * jnp.tanh(half_r))).astype(BF16)

    @pl.when(i == pl.num_programs(1) - 1)
    def _():
        for p in pairs:
            st = st_ref[p]
            sfin_ref[2 * p] = st[:, :GLA_DK].T
            sfin_ref[2 * p + 1] = st[:, GLA_DK:].T


def _gla(src, zsrc, wg, bg, gn, s0, *, chunk, n_chunks, t_valid):
    b, t, _ = src.shape
    tb = chunk * n_chunks
    col = lambda w, c: pl.BlockSpec((None, tb, w), lambda bi, i, c=c: (bi, i, c))
    full = lambda shape: pl.BlockSpec(shape, lambda bi, i: (0,) * len(shape))
    state = pl.BlockSpec((None, GLA_HEADS, GLA_DK, GLA_DV), lambda bi, i: (bi, 0, 0, 0))
    kern = functools.partial(_gla_kernel, chunk=chunk, n_chunks=n_chunks, t_valid=t_valid)
    return pl.pallas_call(
        kern,
        grid=(b, t // tb),
        in_specs=[col(GLA_QK, 0), col(GLA_QK, 1), col(GLA_VW, 1), col(GLA_VW, 2), col(Z_PAD, 0),
                  full((2, Z_PAD, GLA_QK)), full((1, GLA_QK)), full((1, GLA_DV)), state],
        out_specs=[pl.BlockSpec((None, tb, GLA_VW), lambda bi, i: (bi, i, 0)), state],
        out_shape=[jax.ShapeDtypeStruct((b, t, GLA_VW), BF16),
                   jax.ShapeDtypeStruct((b, GLA_HEADS, GLA_DK, GLA_DV), F32)],
        scratch_shapes=[pltpu.VMEM((GLA_PAIRS, GLA_DV, 2 * GLA_DK), F32)],
        compiler_params=_params(("arbitrary", "arbitrary")),
        name="gla",
    )(src, src, src, src, zsrc, wg, bg, gn, s0)


QB = 128
DIL_SUB = 4


def _dil_branch_kernel(*refs, first, final, dilation, sub):
    if first:
        q_ref, kp_ref, kc_ref, vp_ref, vc_ref = refs[:5]
        outs = refs[5:]
    else:
        q_ref, kp_ref, kc_ref, vp_ref, vc_ref, num_ref, m_ref, d_ref = refs[:8]
        outs = refs[8:]
    i = pl.program_id(1)
    c = pl.program_id(2)
    qi = QB + lax.broadcasted_iota(jnp.int32, (QB, 1), 0)
    kj = lax.broadcasted_iota(jnp.int32, (1, 2 * QB), 1)
    off = qi - kj
    in_window = (off >= 0) & (off <= QB)
    lane = lax.broadcasted_iota(jnp.int32, (1, LANE), 1)
    lo = lane < HEAD_DIM
    heads = range(DIL_HEADS)
    cs = [slice(p * LANE, (p + 1) * LANE) for p in range(DIL_HEADS // 2)]
    sel = [lo if h % 2 == 0 else jnp.logical_not(lo) for h in heads]
    for u in range(sub):
        blk = slice(u * QB, (u + 1) * QB)
        rows = pl.ds(dilation * u * QB + c, QB, stride=dilation) if dilation > 1 else blk
        if u == 0:
            k_before, v_before = kp_ref[...], vp_ref[...]
            valid = in_window & ((kj >= QB) | (i > 0))
        else:
            before = slice((u - 1) * QB, u * QB)
            k_before, v_before = kc_ref[before, :], vc_ref[before, :]
            valid = in_window
        k2 = jnp.concatenate([k_before, kc_ref[blk, :]], axis=0)
        v2 = jnp.concatenate([v_before, vc_ref[blk, :]], axis=0)
        q = q_ref[blk, :]
        qm = [jnp.where(sel[h], q[:, cs[h // 2]], jnp.zeros((QB, LANE), BF16)) for h in heads]
        s = [lax.dot_general(qm[h], k2[:, cs[h // 2]], NT_DIMS, preferred_element_type=F32) for h in heads]
        s = [jnp.where(valid, s[h], -jnp.inf) for h in heads]
        m_new = [jnp.max(s[h], axis=-1, keepdims=True) for h in heads]
        if not first:
            m_in = m_ref[rows, :]
            d_in = d_ref[rows, :]
            m_old = [jnp.sum(jnp.where(lane == h, m_in, 0.0), axis=-1, keepdims=True) for h in heads]
            d_old = [jnp.sum(jnp.where(lane == h, d_in, 0.0), axis=-1, keepdims=True) for h in heads]
            m_new = [jnp.maximum(m_new[h], m_old[h]) for h in heads]
        pr = [jnp.exp(s[h] - m_new[h]) for h in heads]
        d_new = [jnp.sum(pr[h], axis=-1, keepdims=True) for h in heads]
        if not first:
            alpha = [jnp.exp(m_old[h] - m_new[h]) for h in heads]
            d_new = [d_new[h] + d_old[h] * alpha[h] for h in heads]
        pv = [jnp.dot(pr[h].astype(BF16), v2[:, cs[h // 2]], preferred_element_type=F32) for h in heads]
        pairs = []
        for p in range(DIL_HEADS // 2):
            num = jnp.where(lo, pv[2 * p], pv[2 * p + 1])
            if not first:
                num = num + num_ref[p, rows, :] * jnp.where(lo, alpha[2 * p], alpha[2 * p + 1])
            if final:
                num = num / jnp.where(lo, d_new[2 * p], d_new[2 * p + 1])
            else:
                outs[0][p, rows, :] = num
            pairs.append(num)
        if final:
            outs[0][rows, :] = jnp.concatenate(pairs, axis=1).astype(BF16)
        else:
            outs[1][rows, :] = functools.reduce(jnp.add, [jnp.where(lane == h, m_new[h], 0.0) for h in heads])
            outs[2][rows, :] = functools.reduce(jnp.add, [jnp.where(lane == h, d_new[h], 0.0) for h in heads])


def _dil_branch(q, k, v, state, *, first, final):
    b, r, ts, _ = q.shape
    t = r * ts
    sub = min(DIL_SUB, ts // QB)
    step = sub * QB
    nq = ts // step
    cur = pl.BlockSpec((None, None, step, DIL_W), lambda bi, i, c: (bi, c, i, 0))
    prev = pl.BlockSpec((None, None, QB, DIL_W), lambda bi, i, c: (bi, c, jnp.maximum(sub * i - 1, 0), 0))
    nat = lambda w: pl.BlockSpec((None, r * step, w), lambda bi, i, c: (bi, i, 0))
    nat_num = pl.BlockSpec((None, DIL_W // LANE, r * step, LANE), lambda bi, i, c: (bi, 0, i, 0))
    args = [q, k, k, v, v]
    in_specs = [cur, prev, cur, prev, cur]
    if not first:
        args += list(state)
        in_specs += [nat_num, nat(LANE), nat(LANE)]
    if final:
        assert r == 1
        out_specs = [nat(DIL_W)]
        out_shape = [jax.ShapeDtypeStruct((b, t, DIL_W), BF16)]
    else:
        out_specs = [nat_num, nat(LANE), nat(LANE)]
        out_shape = [jax.ShapeDtypeStruct((b, DIL_W // LANE, t, LANE), F32),
                     jax.ShapeDtypeStruct((b, t, LANE), F32),
                     jax.ShapeDtypeStruct((b, t, LANE), F32)]
    return pl.pallas_call(
        functools.partial(_dil_branch_kernel, first=first, final=final, dilation=r, sub=sub),
        grid=(b, nq, r),
        in_specs=in_specs,
        out_specs=out_specs,
        out_shape=out_shape,
        compiler_params=_params(("arbitrary", "arbitrary", "arbitrary")),
        name=f"dil_branch_r{r}",
    )(*args)


SAMPLE_HB = 6
SAMPLE_ROWS = 16


def _dil_sample_kernel(q_ref, kn_ref, vn_ref, kc_ref, vc_ref, o_ref, ko_ref, vo_ref, *, t_new, hist):
    new0 = SAMPLE_ROWS - t_new
    t_row = lax.broadcasted_iota(jnp.int32, (SAMPLE_ROWS, 1), 0)
    j = lax.broadcasted_iota(jnp.int32, (1, hist), 1)
    lane = lax.broadcasted_iota(jnp.int32, (1, LANE), 1)
    keep = lane < LANE - t_new

    def weight(off):
        w = jnp.zeros(off.shape, F32)
        for win, dil in DIL_PAIRS:
            w = w + jnp.where((off >= 0) & (off <= win) & (off % dil == 0), 1.0, 0.0)
        return w

    w_c = weight(hist + t_row - j)
    w_n = [weight(t_row - t) for t in range(t_new)]
    heads = range(SAMPLE_HB)
    hs = [slice(h * HEAD_DIM, (h + 1) * HEAD_DIM) for h in heads]
    qs = [q_ref[:, hs[h]] for h in heads]
    s_c = [jnp.dot(qs[h].astype(BF16), kc_ref[h].astype(BF16), preferred_element_type=F32) for h in heads]
    s_c = [jnp.where(w_c > 0, s, -jnp.inf) for s in s_c]
    s_n = [[jnp.where(w_n[t] > 0, jnp.sum(qs[h] * kn_ref[new0 + t:new0 + t + 1, hs[h]], axis=-1, keepdims=True), -jnp.inf)
            for t in range(t_new)] for h in heads]
    m = [functools.reduce(jnp.maximum, s_n[h], jnp.max(s_c[h], axis=-1, keepdims=True)) for h in heads]
    p_c = [w_c * jnp.exp(s_c[h] - m[h]) for h in heads]
    p_n = [[w_n[t] * jnp.exp(s_n[h][t] - m[h]) for t in range(t_new)] for h in heads]
    den = [functools.reduce(jnp.add, p_n[h], jnp.sum(p_c[h], axis=-1, keepdims=True)) for h in heads]
    num = [lax.dot_general(p_c[h].astype(BF16), vc_ref[h].astype(BF16), NT_DIMS, preferred_element_type=F32)
           for h in heads]
    num = [functools.reduce(jnp.add, [p_n[h][t] * vn_ref[new0 + t:new0 + t + 1, hs[h]] for t in range(t_new)], num[h])
           for h in heads]
    o_ref[...] = jnp.concatenate([num[h] / den[h] for h in heads], axis=1)

    blank = jnp.zeros((LANE - SAMPLE_ROWS, HEAD_DIM), F32)
    for h in heads:
        for src, new_ref, dst in ((kc_ref, kn_ref, ko_ref), (vc_ref, vn_ref, vo_ref)):
            new_tile = jnp.concatenate([blank, new_ref[:, hs[h]]], axis=0).T
            nxt = pltpu.roll(src[h, :, 0:LANE], LANE - t_new, axis=1)
            for c in range(hist // LANE):
                cur_t = nxt
                if c + 1 < hist // LANE:
                    nxt = pltpu.roll(src[h, :, (c + 1) * LANE:(c + 2) * LANE], LANE - t_new, axis=1)
                else:
                    nxt = new_tile
                dst[h, :, c * LANE:(c + 1) * LANE] = jnp.where(keep, cur_t, nxt)


def _dil_sample(q, kn, vn, cache_kt, cache_vt, t_new):
    b, nh, _, hist = cache_kt.shape
    w = SAMPLE_HB * HEAD_DIM
    small = pl.BlockSpec((None, SAMPLE_ROWS, w), lambda bi, c: (bi, 0, c))
    big = pl.BlockSpec((None, SAMPLE_HB, HEAD_DIM, hist), lambda bi, c: (bi, c, 0, 0))
    return pl.pallas_call(
        functools.partial(_dil_sample_kernel, t_new=t_new, hist=hist),
        grid=(b, nh // SAMPLE_HB),
        in_specs=[small, small, small, big, big],
        out_specs=[small, big, big],
        out_shape=[jax.ShapeDtypeStruct((b, SAMPLE_ROWS, nh * HEAD_DIM), F32),
                   jax.ShapeDtypeStruct(cache_kt.shape, F32),
                   jax.ShapeDtypeStruct(cache_vt.shape, F32)],
        compiler_params=_params(("arbitrary", "arbitrary")),
        name="dil_sample",
    )(q, kn, vn, cache_kt, cache_vt)


OUT_ROWS = 256

STRIDE_ROWS = 256
STRIDE_GROUPS = STRIDE_ROWS // SUBLANE


def _to_strided(val, stage_ref):
    for j in range(val.shape[1] // LANE):
        for s_ in range(SUBLANE):
            stage_ref[j, pl.ds(s_, STRIDE_GROUPS, stride=SUBLANE), :] = (
                val[s_ * STRIDE_GROUPS:(s_ + 1) * STRIDE_GROUPS, j * LANE:(j + 1) * LANE])
    return jnp.concatenate([stage_ref[j] for j in range(val.shape[1] // LANE)], axis=1)


def _from_strided(val, stage_ref):
    for j in range(val.shape[1] // LANE):
        stage_ref[j] = val[:, j * LANE:(j + 1) * LANE]
    return jnp.concatenate(
        [jnp.concatenate([stage_ref[j, pl.ds(s_, STRIDE_GROUPS, stride=SUBLANE), :]
                          for j in range(val.shape[1] // LANE)], axis=1) for s_ in range(SUBLANE)], axis=0)


def _out_proj_kernel(oa_ref, ob_ref, w_ref, x_ref, g_ref, h_ref, xn_ref, wbf_ref, *stage, strided):
    @pl.when(pl.program_id(0) == 0)
    def _():
        wbf_ref[...] = w_ref[...].astype(BF16)

    for r0 in range(0, x_ref.shape[0], OUT_ROWS):
        rows = slice(r0, min(r0 + OUT_ROWS, x_ref.shape[0]))
        y = (jnp.dot(oa_ref[rows, :], wbf_ref[:GLA_VW, :], preferred_element_type=F32)
             + jnp.dot(ob_ref[rows, :], wbf_ref[GLA_VW:, :], preferred_element_type=F32))
        h = x_ref[rows, :] + y
        h_ref[rows, :] = h
        ms = jnp.mean(h * h, axis=-1, keepdims=True)
        xn = h * lax.rsqrt(ms + EPS) * g_ref[...]
        if strided:
            xn = _to_strided(xn, stage[0])
        xn_ref[rows, :] = xn.astype(BF16)


def _out_proj(oa, ob, w, x, g, tm, strided=False):
    m, d = x.shape
    assert not strided or OUT_ROWS == STRIDE_ROWS
    row = lambda width: pl.BlockSpec((tm, width), lambda i: (i, 0))
    g2 = g.reshape(1, d)
    scratch = [pltpu.VMEM(w.shape, BF16)]
    if strided:
        scratch.append(pltpu.VMEM((d // LANE, STRIDE_ROWS, LANE), F32))
    return pl.pallas_call(
        functools.partial(_out_proj_kernel, strided=strided),
        grid=(m // tm,),
        in_specs=[row(GLA_VW), row(DIL_W),
                  pl.BlockSpec(w.shape, lambda i: (0, 0), pipeline_mode=pl.Buffered(1)),
                  row(d), pl.BlockSpec(g2.shape, lambda i: (0, 0))],
        out_specs=[row(d), row(d)],
        out_shape=[jax.ShapeDtypeStruct((m, d), F32), jax.ShapeDtypeStruct((m, d), BF16)],
        scratch_shapes=scratch,
        compiler_params=_params(("arbitrary",)),
        name="out_proj",
    )(oa, ob, w, x, g2)


FF_ROWS = STRIDE_ROWS
FF_TAIL = 2 * SUBLANE
FF_SUB = FF_TN // LANE
FF_TILES = D_FF // LANE


def _conv_taps(cw, cb, up, prev1, prev2):
    return cb + cw[0:1] * prev2 + cw[1:2] * prev1 + cw[2:3] * up


def _swiglu(u, g):
    h = 0.5 * g
    return ((h + h * jnp.tanh(h)) * u).astype(BF16)


def _stage_up_weights(w_blocks, wbf_ref, j):
    for half in range(2):
        for q in range(FF_SUB):
            blk = w_blocks[half * FF_SUB + q][...]
            if (FF_NT - 1) * FF_SUB + q >= FF_TILES:
                blk = jnp.where(j * FF_SUB + q < FF_TILES, blk, 0.0)
            wbf_ref[half, :, q * LANE:(q + 1) * LANE] = blk.astype(BF16)


def _up_weight_specs(tile_of_step):
    specs = []
    for half in range(2):
        for q in range(FF_SUB):
            def index(*ids, half=half, q=q):
                blk = jnp.minimum(tile_of_step(*ids) * FF_SUB + q, FF_TILES - 1)
                return (0, half * FF_TILES + blk)
            specs.append(pl.BlockSpec((D_MODEL, LANE), index))
    return specs


def _ffn_up_seq_kernel(*refs, tiles_per_seq):
    x_ref = refs[0]
    w_blocks = refs[1:1 + 2 * FF_SUB]
    cwu_ref, cwg_ref, cbu_ref, cbg_ref, act_ref, tu_ref, tg_ref, wbf_ref, carry_ref = refs[1 + 2 * FF_SUB:]
    j = pl.program_id(0)
    i = pl.program_id(1)
    tm = x_ref.shape[0]

    @pl.when(i == 0)
    def _():
        _stage_up_weights(w_blocks, wbf_ref, j)

    @pl.when(i % tiles_per_seq == 0)
    def _():
        carry_ref[...] = jnp.zeros_like(carry_ref)

    carry = [carry_ref[0], carry_ref[1]]
    shift = lambda before, cur: pltpu.roll(jnp.concatenate([before, cur], axis=0), 1, axis=0)[SUBLANE:]
    for r0 in range(0, tm, FF_ROWS):
        x = x_ref[r0:r0 + FF_ROWS, :]
        conv = []
        for idx, (cw_ref, cb_ref) in enumerate(((cwu_ref, cbu_ref), (cwg_ref, cbg_ref))):
            up = jnp.dot(x, wbf_ref[idx], preferred_element_type=F32)
            tail = up[FF_ROWS - FF_TAIL:]
            p_a = shift(carry[idx][:SUBLANE], tail[:SUBLANE])
            p_b = shift(carry[idx][SUBLANE:], tail[SUBLANE:])
            prev1 = jnp.concatenate([p_b, up[:FF_ROWS - SUBLANE]], axis=0)
            prev2 = jnp.concatenate([p_a, p_b, up[:FF_ROWS - 2 * SUBLANE]], axis=0)
            carry[idx] = tail
            conv.append(_conv_taps(cw_ref[...], cb_ref[...], up, prev1, prev2))
        act_ref[r0:r0 + FF_ROWS, :] = _swiglu(*conv)
    for idx, tail_ref in enumerate((tu_ref, tg_ref)):
        carry_ref[idx] = carry[idx]
        tail_ref[...] = carry[idx]


def _ffn_up_seq(xn, w_up, conv_w, conv_b, tm, t_seq):
    m, d = xn.shape
    tiles_per_seq = t_seq // tm
    xs = pl.BlockSpec((tm, d), lambda j, i: (i, 0))
    cu = lambda r: pl.BlockSpec((r, FF_TN), lambda j, i: (0, j))
    cg = lambda r: pl.BlockSpec((r, FF_TN), lambda j, i: (0, FF_NT + j))
    act = pl.BlockSpec((tm, FF_TN), lambda j, i: (i, j))
    tail = pl.BlockSpec((None, FF_TAIL, FF_TN), lambda j, i: (i // tiles_per_seq, 0, j))
    w_specs = _up_weight_specs(lambda j, i: j)
    return pl.pallas_call(
        functools.partial(_ffn_up_seq_kernel, tiles_per_seq=tiles_per_seq),
        grid=(FF_NT, m // tm),
        in_specs=[xs] + w_specs + [cu(CONV_W), cg(CONV_W), cu(1), cg(1)],
        out_specs=[act, tail, tail],
        out_shape=[jax.ShapeDtypeStruct((m, D_FF_P), BF16)]
        + [jax.ShapeDtypeStruct((m // t_seq, FF_TAIL, D_FF_P), F32)] * 2,
        scratch_shapes=[pltpu.VMEM((2, d, FF_TN), BF16), pltpu.VMEM((2, FF_TAIL, FF_TN), F32)],
        compiler_params=_params(("arbitrary", "arbitrary")),
        name="ffn_up",
    )(xn, *([w_up] * len(w_specs)), conv_w, conv_w, conv_b, conv_b)


def _ffn_up_hist_kernel(*refs, n_seq):
    x_ref = refs[0]
    w_blocks = refs[1:1 + 2 * FF_SUB]
    cwu_ref, cwg_ref, cbu_ref, cbg_ref, hu_ref, hg_ref, act_ref, upu_ref, upg_ref, wbf_ref = refs[1 + 2 * FF_SUB:]
    _stage_up_weights(w_blocks, wbf_ref, pl.program_id(0))
    x = x_ref[...]
    m = x.shape[0]
    conv = []
    for idx, (cw_ref, cb_ref, h_ref, up_ref) in enumerate((
            (cwu_ref, cbu_ref, hu_ref, upu_ref), (cwg_ref, cbg_ref, hg_ref, upg_ref))):
        up = jnp.dot(x, wbf_ref[idx], preferred_element_type=F32)
        full = jnp.concatenate([h_ref[...], up], axis=0)
        prev1 = full[n_seq:n_seq + m]
        prev2 = full[:m]
        up_ref[...] = up
        conv.append(_conv_taps(cw_ref[...], cb_ref[...], up, prev1, prev2))
    act_ref[...] = _swiglu(*conv)


def _ffn_up_hist(xn, w_up, conv_w, conv_b, hist, n_seq):
    m, d = xn.shape
    assert n_seq % SUBLANE == 0
    xs = pl.BlockSpec((m, d), lambda j: (0, 0))
    u = lambda r: pl.BlockSpec((r, FF_TN), lambda j: (0, j))
    g = lambda r: pl.BlockSpec((r, FF_TN), lambda j: (0, FF_NT + j))
    w_specs = _up_weight_specs(lambda j: j)
    nh = hist.shape[0]
    return pl.pallas_call(
        functools.partial(_ffn_up_hist_kernel, n_seq=n_seq),
        grid=(FF_NT,),
        in_specs=[xs] + w_specs + [u(CONV_W), g(CONV_W), u(1), g(1), u(nh), g(nh)],
        out_specs=[u(m), u(m), u(m)],
        out_shape=[jax.ShapeDtypeStruct((m, D_FF_P), BF16)] + [jax.ShapeDtypeStruct((m, D_FF_P), F32)] * 2,
        scratch_shapes=[pltpu.VMEM((2, d, FF_TN), BF16)],
        compiler_params=_params(("arbitrary",)),
        name="ffn_up_hist",
    )(xn, *([w_up] * len(w_specs)), conv_w, conv_w, conv_b, conv_b, hist, hist)


DOWN_ROWS = 256


def _ffn_down_kernel(a_ref, w_ref, h_ref, g_ref, y_ref, *stage, strided):
    kdim = w_ref.shape[0]
    for r0 in range(0, a_ref.shape[0], DOWN_ROWS):
        rows = slice(r0, min(r0 + DOWN_ROWS, a_ref.shape[0]))
        f = jnp.dot(a_ref[rows, :kdim], w_ref[...], preferred_element_type=F32)
        if strided:
            f = _from_strided(f, stage[0])
        h = h_ref[rows, :] + f
        ms = jnp.mean(h * h, axis=-1, keepdims=True)
        y_ref[rows, :] = h * lax.rsqrt(ms + EPS) * g_ref[...]


def _ffn_down(act, w, h, g, tm, strided=False):
    m, d = h.shape
    assert not strided or DOWN_ROWS == STRIDE_ROWS
    scratch = [pltpu.VMEM((d // LANE, STRIDE_ROWS, LANE), F32)] if strided else []
    return pl.pallas_call(
        functools.partial(_ffn_down_kernel, strided=strided),
        grid=(m // tm,),
        in_specs=[pl.BlockSpec((tm, act.shape[1]), lambda i: (i, 0)),
                  pl.BlockSpec(w.shape, lambda i: (0, 0), pipeline_mode=pl.Buffered(1)),
                  pl.BlockSpec((tm, d), lambda i: (i, 0)),
                  pl.BlockSpec((1, d), lambda i: (0, 0))],
        out_specs=pl.BlockSpec((tm, d), lambda i: (i, 0)),
        out_shape=jax.ShapeDtypeStruct((m, d), F32),
        scratch_shapes=scratch,
        compiler_params=_params(("arbitrary",)),
        name="ffn_down",
    )(act, w, h, g.reshape(1, d))


def _pad_cols(a, width):
    return jnp.pad(a, ((0, 0), (0, width - a.shape[1])))


def _split_pad_ff(a):
    return jnp.concatenate([_pad_cols(a[:, :D_FF], D_FF_P), _pad_cols(a[:, D_FF:], D_FF_P)], axis=1)


def _unpad_ff(u, g):
    return jnp.concatenate([u[..., :D_FF], g[..., :D_FF]], axis=-1)


def kernel(x_prompt, x_sample, state_gla, cache_dil_k, cache_dil_v, state_ffn_conv, norm_mix, w_in, w_gate_up,
           b_gate, gla_norm, w_out, norm_ffn, w_ffn_up, ffn_conv_w, ffn_conv_b, w_ffn_down, norm_final):
    bp, tp, d = x_prompt.shape
    bs, ts, _ = x_sample.shape
    l = 0

    z0 = 2 * GLA_QK + 2 * GLA_VW
    w_in_t = w_in[l].T
    w_z_t = jnp.pad(w_in_t[z0:z0 + GLA_RANK], ((0, Z_PAD - GLA_RANK), (0, 0)))
    wg_f = jnp.pad(w_gate_up[l], ((0, Z_PAD - GLA_RANK), (0, 0)))
    wg_hi = wg_f.astype(BF16)
    wg_p = jnp.stack([wg_hi, (wg_f - wg_hi.astype(F32)).astype(BF16)])
    bg = b_gate[l].reshape(1, GLA_QK)
    gn = gla_norm[l].reshape(1, GLA_DV)
    cw_p = _split_pad_ff(ffn_conv_w[l])
    cb_p = _split_pad_ff(ffn_conv_b[l].reshape(1, -1))
    w_dn = w_ffn_down[l].astype(BF16)

    mp = bp * tp
    xp = x_prompt.reshape(mp, d)
    xn_mix_p = _rms(xp, norm_mix[l], 512)
    mix_p = _proj(xn_mix_p, w_in_t, 2048, 768, n=z0)
    dil_p = _proj(xn_mix_p, w_in_t, 2048, 768, n=3 * DIL_W, row0=z0 + GLA_RANK)
    z_p = _proj(xn_mix_p, w_z_t, 2048, Z_PAD)
    cos_p, sin_p = _rope_tables(jnp.arange(tp))
    cos_p = jnp.tile(cos_p, (bp, 1))
    sin_p = jnp.tile(sin_p, (bp, 1))
    rope_out = _rope(dil_p, cos_p, sin_p, 512, n_seq=bp)
    by_class = {1: tuple(a.reshape(bp, 1, tp, DIL_W) for a in rope_out[:3])}
    for di, r in enumerate(CLASS_DILATIONS):
        by_class[r] = tuple(rope_out[3 + 3 * di:6 + 3 * di])
    k_kept_t, v_kept_t = rope_out[-2:]

    oa_p, gla_p = _gla(mix_p.reshape(bp, tp, z0), z_p.reshape(bp, tp, Z_PAD), wg_p, bg, gn,
                       jnp.zeros((bp, GLA_HEADS, GLA_DK, GLA_DV), F32),
                       chunk=GLA_ROWS, n_chunks=4, t_valid=tp)

    state = _dil_branch(*by_class[16], None, first=True, final=False)
    state = _dil_branch(*by_class[4], state, first=False, final=False)
    (ob_p,) = _dil_branch(*by_class[1], state, first=False, final=True)

    h_p, xn_p = _out_proj(oa_p.reshape(mp, GLA_VW), ob_p.reshape(mp, DIL_W), w_out[l], xp, norm_ffn[l], 512, strided=True)
    act_p, tail_u, tail_g = _ffn_up_seq(xn_p, w_ffn_up[l], cw_p, cb_p, 2048, tp)
    y_p = _ffn_down(act_p, w_dn, h_p, norm_final, 512, strided=True)

    buf_p = min(DIL_WINDOW, tp)
    y_prompt = y_p.reshape(bp, tp, d)
    new_gla_p = gla_p[None]
    new_k_p = k_kept_t.reshape(bp, DIL_HEADS, HEAD_DIM, buf_p).transpose(0, 3, 1, 2)[None]
    new_v_p = v_kept_t.reshape(bp, DIL_HEADS, HEAD_DIM, buf_p).transpose(0, 3, 1, 2)[None]
    last_two = lambda a: a[:, SUBLANE - 1::SUBLANE]
    new_conv_p = _unpad_ff(last_two(tail_u), last_two(tail_g))[None]

    ms_ = bs * ts
    xs = x_sample.reshape(ms_, d)
    xn_mix_s = _rms(xs, norm_mix[l], ms_)
    mix_s = _proj(xn_mix_s, w_in_t, ms_, 768, n=z0)
    dil_s = _proj(xn_mix_s, w_in_t, ms_, 768, n=3 * DIL_W, row0=z0 + GLA_RANK)
    z_s = _proj(xn_mix_s, w_z_t, ms_, Z_PAD)
    cos_s, sin_s = _rope_tables(PAST_LEN + jnp.arange(ts))
    cos_s = jnp.tile(cos_s, (bs, 1))
    sin_s = jnp.tile(sin_s, (bs, 1))
    q_rs, k_rs = _rope(dil_s, cos_s, sin_s, ms_)

    pad_chunk = lambda a: jnp.pad(a.reshape(bs, ts, -1), ((0, 0), (0, GLA_ROWS_SHORT - ts), (0, 0)))
    oa_s, gla_s = _gla(pad_chunk(mix_s), pad_chunk(z_s), wg_p, bg, gn, state_gla[l],
                       chunk=GLA_ROWS_SHORT, n_chunks=1, t_valid=ts)
    oa_s = oa_s[:, :ts].reshape(ms_, GLA_VW)

    v_s = dil_s[:, 2 * DIL_W:]
    rows_first = lambda a: jnp.pad(a.reshape(bs, ts, DIL_W), ((0, 0), (0, SAMPLE_ROWS - ts), (0, 0)))
    rows_last = lambda a: jnp.pad(a.reshape(bs, ts, DIL_W), ((0, 0), (SAMPLE_ROWS - ts, 0), (0, 0)))
    time_minor = lambda a: a.transpose(0, 2, 3, 1)
    ob_s, k_s_t, v_s_t = _dil_sample(rows_first(q_rs), rows_last(k_rs), rows_last(v_s),
                                     time_minor(cache_dil_k[l]), time_minor(cache_dil_v[l]), ts)
    ob_s = ob_s[:, :ts].astype(BF16)

    h_s, xn_s = _out_proj(oa_s, ob_s.reshape(ms_, DIL_W), w_out[l], xs, norm_ffn[l], ms_)
    time_major = lambda a: a.reshape(bs, -1, a.shape[-1]).transpose(1, 0, 2).reshape(-1, a.shape[-1])
    batch_major = lambda a: a.reshape(-1, bs, a.shape[-1]).transpose(1, 0, 2)
    conv_hist = time_major(_split_pad_ff(state_ffn_conv[l].reshape(bs * (CONV_W - 1), 2 * D_FF)))
    act_s, up_u, up_g = _ffn_up_hist(time_major(xn_s), w_ffn_up[l], cw_p, cb_p, conv_hist, bs)
    act_s = batch_major(act_s).reshape(ms_, D_FF_P)
    y_s = _ffn_down(act_s, w_dn, h_s, norm_final, ms_)

    y_sample = y_s.reshape(bs, ts, d)
    new_gla_s = gla_s[None]
    new_k_s = k_s_t.transpose(0, 3, 1, 2)[None]
    new_v_s = v_s_t.transpose(0, 3, 1, 2)[None]
    new_conv_s = batch_major(_unpad_ff(up_u, up_g))[None, :, ts - (CONV_W - 1):]

    return (y_prompt, y_sample, new_gla_p, new_gla_s, new_k_p, new_k_s, new_v_p, new_v_s,
            new_conv_p, new_conv_s)
```

```python
import functools

import jax
import jax.numpy as jnp
from jax import lax
from jax.experimental import pallas as pl
from jax.experimental.pallas import tpu as pltpu

F32 = jnp.float32
BF16 = jnp.bfloat16

D_MODEL = 2048
HEAD_DIM = 64
GLA_HEADS = 10
GLA_DK = 64
GLA_DV = 128
GLA_RANK = 16
GLA_TAU = 16.0
GLA_ROWS = 128
GLA_ROWS_SHORT = 16
DIL_HEADS = 12
DIL_PAIRS = ((128, 1), (512, 4), (2048, 16))
DIL_WINDOW = 2048
ROPE_THETA = 10000.0
D_FF = 5504
PAST_LEN = 8192
CONV_W = 3
EPS = 1e-6

GLA_QK = GLA_HEADS * GLA_DK
GLA_VW = GLA_HEADS * GLA_DV
DIL_W = DIL_HEADS * HEAD_DIM

LANE = 128
SUBLANE = 8
VMEM_LIMIT = 56 * 1024 * 1024

Z_PAD = LANE
D_FF_P = 5632
FF_TN = 512
FF_NT = D_FF_P // FF_TN

NT_DIMS = (((1,), (1,)), ((), ()))


def _params(sem, vmem=VMEM_LIMIT, flags=None):
    return pltpu.CompilerParams(dimension_semantics=sem, vmem_limit_bytes=vmem, flags=flags)


def _rms_kernel(x_ref, g_ref, o_ref):
    x = x_ref[...]
    ms = jnp.mean(x * x, axis=-1, keepdims=True)
    o_ref[...] = (x * lax.rsqrt(ms + EPS) * g_ref[...]).astype(BF16)


def _rms(x, g, tm):
    m, d = x.shape
    return pl.pallas_call(
        _rms_kernel,
        grid=(m // tm,),
        in_specs=[pl.BlockSpec((tm, d), lambda i: (i, 0)), pl.BlockSpec((1, d), lambda i: (0, 0))],
        out_specs=pl.BlockSpec((tm, d), lambda i: (i, 0)),
        out_shape=jax.ShapeDtypeStruct((m, d), BF16),
        compiler_params=_params(("arbitrary",)),
        name="rms",
    )(x, g.reshape(1, d))


def _proj_kernel(x_ref, w_ref, o_ref, wbf_ref):
    @pl.when(pl.program_id(1) == 0)
    def _():
        wbf_ref[...] = w_ref[...].astype(BF16)

    o_ref[...] = lax.dot_general(x_ref[...], wbf_ref[...], NT_DIMS, preferred_element_type=F32)


def _proj(xn, wt, tm, tn, n=None, row0=0):
    m, k = xn.shape
    n = wt.shape[0] if n is None else n
    if row0 % tn == 0:
        w_spec = pl.BlockSpec((tn, k), lambda j, i: (row0 // tn + j, 0))
    else:
        assert row0 % SUBLANE == 0 and tn % SUBLANE == 0
        w_spec = pl.BlockSpec((pl.Element(tn), pl.Element(k)),
                              lambda j, i: (pl.multiple_of(row0 + j * tn, SUBLANE), 0))
    return pl.pallas_call(
        _proj_kernel,
        grid=(n // tn, m // tm),
        in_specs=[pl.BlockSpec((tm, k), lambda j, i: (i, 0)), w_spec],
        out_specs=pl.BlockSpec((tm, tn), lambda j, i: (i, j)),
        out_shape=jax.ShapeDtypeStruct((m, n), F32),
        scratch_shapes=[pltpu.VMEM((tn, k), BF16)],
        compiler_params=_params(("arbitrary", "arbitrary")),
        name="proj",
    )(xn, wt)


CLASS_DILATIONS = tuple(r for _, r in DIL_PAIRS if r > 1)


ROPE_ROWS = 128


def _rope_kernel(q_ref, k_ref, v_ref, cos_ref, sin_ref, *rest, by_class, tiles, first_kept):
    reps = DIL_W // LANE
    lane = lax.broadcasted_iota(jnp.int32, (1, DIL_W), 1)
    first_half = (lane % HEAD_DIM) < (HEAD_DIM // 2)
    tm = q_ref.shape[0]
    if by_class:
        n_class = 3 * len(CLASS_DILATIONS)
        qb_ref, kb_ref, vb_ref = rest[:3]
        class_refs = rest[3:3 + n_class]
        kt_ref, vt_ref, sq_ref, sk_ref, sv_ref = rest[3 + n_class:]
        kept = pl.program_id(0) % tiles >= first_kept
    else:
        qo_ref, ko_ref = rest

    for r0 in range(0, tm, ROPE_ROWS):
        rows = slice(r0, min(r0 + ROPE_ROWS, tm))
        cos = jnp.concatenate([cos_ref[rows, :]] * reps, axis=1)
        sin = jnp.concatenate([sin_ref[rows, :]] * reps, axis=1)

        def rot(x):
            partner = jnp.where(first_half,
                                pltpu.roll(x, DIL_W - HEAD_DIM // 2, axis=1),
                                pltpu.roll(x, HEAD_DIM // 2, axis=1))
            return x * cos + partner * sin

        q = rot(q_ref[rows, :]) * (HEAD_DIM ** -0.5)
        k = rot(k_ref[rows, :])
        if not by_class:
            qo_ref[rows, :] = q
            ko_ref[rows, :] = k
            continue
        v = v_ref[rows, :]
        qb_ref[rows, :] = q.astype(BF16)
        kb_ref[rows, :] = k.astype(BF16)
        vb_ref[rows, :] = v.astype(BF16)

        @pl.when(kept)
        def _():
            kt_ref[:, rows] = k.T
            vt_ref[:, rows] = v.T

        for src, val in ((sq_ref, q), (sk_ref, k), (sv_ref, v)):
            for j in range(reps):
                src[j, rows, :] = val[:, j * LANE:(j + 1) * LANE]
    if not by_class:
        return
    for di, r in enumerate(CLASS_DILATIONS):
        for src, dst in zip((sq_ref, sk_ref, sv_ref), class_refs[3 * di:3 * di + 3]):
            for c in range(r):
                rows = pl.ds(c, tm // r, stride=r)
                dst[c] = jnp.concatenate([src[j, rows, :] for j in range(reps)], axis=1).astype(BF16)


def _rope(proj, cos_t, sin_t, tm, n_seq=None):
    m = proj.shape[0]
    by_class = n_seq is not None
    blk = lambda c: pl.BlockSpec((tm, DIL_W), lambda i, c=c: (i, c))
    tab = pl.BlockSpec((tm, LANE), lambda i: (i, 0))
    out = pl.BlockSpec((tm, DIL_W), lambda i: (i, 0))
    tiles, first_kept, scratch = 1, 0, []
    if by_class:
        t_seq = m // n_seq
        tiles = t_seq // tm
        window = min(DIL_WINDOW, t_seq)
        first_kept = (t_seq - window) // tm
        out_specs = [out, out, out]
        out_shape = [jax.ShapeDtypeStruct((m, DIL_W), BF16)] * 3
        for r in CLASS_DILATIONS:
            spec = pl.BlockSpec((None, r, tm // r, DIL_W), lambda i: (i // tiles, 0, i % tiles, 0))
            out_specs += [spec] * 3
            out_shape += [jax.ShapeDtypeStruct((n_seq, r, t_seq // r, DIL_W), BF16)] * 3
        kept = pl.BlockSpec((None, DIL_W, tm), lambda i: (i // tiles, 0, jnp.maximum(i % tiles - first_kept, 0)))
        out_specs += [kept] * 2
        out_shape += [jax.ShapeDtypeStruct((n_seq, DIL_W, window), F32)] * 2
        scratch = [pltpu.VMEM((DIL_W // LANE, tm, LANE), F32)] * 3
    else:
        out_specs = [out, out]
        out_shape = [jax.ShapeDtypeStruct((m, DIL_W), F32)] * 2
    return pl.pallas_call(
        functools.partial(_rope_kernel, by_class=by_class, tiles=tiles, first_kept=first_kept),
        grid=(m // tm,),
        in_specs=[blk(0), blk(1), blk(2), tab, tab],
        out_specs=out_specs,
        out_shape=out_shape,
        scratch_shapes=scratch,
        compiler_params=_params(("arbitrary",)),
        name="rope",
    )(proj, proj, proj, cos_t, sin_t)


def _rope_tables(pos):
    half = HEAD_DIM // 2
    inv_freq = ROPE_THETA ** (-2.0 * jnp.arange(half, dtype=F32) / HEAD_DIM)
    ang = pos.astype(F32)[:, None] * inv_freq[None, :]
    cos = jnp.cos(ang)
    sin = jnp.sin(ang)
    cos_t = jnp.concatenate([cos, cos, cos, cos], axis=1)
    sin_t = jnp.concatenate([-sin, sin, -sin, sin], axis=1)
    return cos_t, sin_t


GLA_PAIRS = GLA_HEADS // 2


def _gla_kernel(q_ref, k_ref, v_ref, r_ref, z_ref, wg_ref, bg_ref, gn_ref, s0_ref,
                o_ref, sfin_ref, st_ref, *, chunk, n_chunks, t_valid):
    i = pl.program_id(1)
    pairs = range(GLA_PAIRS)

    @pl.when(i == 0)
    def _():
        for p in pairs:
            st_ref[p] = jnp.concatenate([s0_ref[2 * p].T, s0_ref[2 * p + 1].T], axis=1)

    row = lax.broadcasted_iota(jnp.int32, (chunk, chunk), 0)
    col = lax.broadcasted_iota(jnp.int32, (chunk, chunk), 1)
    tril = (row >= col).astype(BF16)
    t_in = lax.broadcasted_iota(jnp.int32, (chunk, 1), 0)
    both = lambda m: jnp.concatenate([m, m], axis=0)
    levels = []
    s_ = chunk
    while s_ >= 2:
        half = s_ // 2
        pivot = (row // s_) * s_ + (half - 1)
        levels.append(dict(
            size=s_,
            upto=(col <= pivot).astype(BF16),
            pair=both((row // s_ == col // s_) & (row % s_ >= half) & (col % s_ < half)),
            second=(t_in % s_) >= half))
        s_ = half
    diag = both(row == col)
    lo = lax.broadcasted_iota(jnp.int32, (1, LANE), 1) < GLA_DK
    mm = functools.partial(jnp.dot, preferred_element_type=F32)
    nt = functools.partial(lax.dot_general, dimension_numbers=NT_DIMS, preferred_element_type=F32)

    def split(a):
        hi = a.astype(BF16)
        return hi, (a - hi.astype(F32)).astype(BF16)

    def stack(a, p):
        t = a[:, p * LANE:(p + 1) * LANE]
        zero = jnp.zeros_like(t)
        return jnp.concatenate([jnp.where(lo, t, zero), jnp.where(lo, zero, t)], axis=0)

    for c in range(n_chunks):
        sl = pl.ds(c * chunk, chunk)
        z_hi, z_lo = split(z_ref[sl, :])
        x = mm(z_hi, wg_ref[0]) + mm(z_hi, wg_ref[1]) + mm(z_lo, wg_ref[0]) + bg_ref[...]
        log_a = -(jnp.maximum(-x, 0.0) + jnp.log1p(jnp.exp(-jnp.abs(x)))) / GLA_TAU
        t_abs = (i * n_chunks + c) * chunk + t_in
        log_a = jnp.where(t_abs < t_valid, log_a, 0.0)
        a_hi, a_lo = split(log_a)
        cum = mm(tril, a_hi) + mm(tril, a_lo)
        last = cum[chunk - 1:chunk, :]
        q = q_ref[sl, :] * (GLA_DK ** -0.5)
        k = k_ref[sl, :]
        q_dec = (q * jnp.exp(cum)).astype(BF16)
        k_end = (k * jnp.exp(last - cum)).astype(BF16)
        a_end = jnp.exp(last)
        heads = range(GLA_HEADS)
        vs = [slice(h * GLA_DV, (h + 1) * GLA_DV) for h in heads]
        v_f = [v_ref[sl, vs[h]] for h in heads]
        vh = [v_f[h].astype(BF16) for h in heads]
        pivot_cum = []
        for lv in levels:
            if lv["size"] >= SUBLANE:
                blocks = cum.reshape(chunk // lv["size"], lv["size"], cum.shape[1])
                pick = blocks[:, lv["size"] // 2 - 1:lv["size"] // 2, :]
                pivot_cum.append(jnp.broadcast_to(pick, blocks.shape).reshape(cum.shape))
            else:
                pivot_cum.append(mm(lv["upto"], a_hi) + mm(lv["upto"], a_lo))
        q_lv = [(q * jnp.exp(jnp.where(lv["second"], cum - pc, 0.0))).astype(BF16)
                for lv, pc in zip(levels, pivot_cum)]
        k_lv = [(k * jnp.exp(jnp.where(lv["second"], 0.0, pc - cum))).astype(BF16)
                for lv, pc in zip(levels, pivot_cum)]
        q_bf, k_bf = q.astype(BF16), k.astype(BF16)
        lanes = [slice(p * LANE, (p + 1) * LANE) for p in pairs]
        parts = [[nt(stack(ql, p), kl[:, lanes[p]]) for p in pairs] for ql, kl in zip(q_lv, k_lv)]
        scores = [jnp.where(diag, nt(stack(q_bf, p), k_bf[:, lanes[p]]), 0.0) for p in pairs]
        for lv, part in zip(levels, parts):
            scores = [jnp.where(lv["pair"], part[p], scores[p]) for p in pairs]
        scores = [scores[p].astype(BF16) for p in pairs]
        st = [st_ref[p] for p in pairs]
        carried = [nt(stack(q_dec, p), st[p].astype(BF16)) for p in pairs]
        o = [mm(scores[h // 2][(h % 2) * chunk:(h % 2 + 1) * chunk], vh[h])
             + carried[h // 2][(h % 2) * chunk:(h % 2 + 1) * chunk] for h in heads]
        for p in pairs:
            ke = k_end[:, lanes[p]]
            zero = jnp.zeros_like(ke)
            st_ref[p] = (st[p] * a_end[:, lanes[p]]
                         + mm(v_f[2 * p].T.astype(BF16), jnp.where(lo, ke, zero))
                         + mm(v_f[2 * p + 1].T.astype(BF16), jnp.where(lo, zero, ke)))
        ms = [jnp.mean(o[h] * o[h], axis=-1, keepdims=True) for h in heads]
        for h in heads:
            on = o[h] * lax.rsqrt(ms[h] + EPS) * gn_ref[...]
            half_r = 0.5 * r_ref[sl, vs[h]]
            o_ref[sl, vs[h]] = (on * (half_r + half_r * jnp.tanh(half_r))).astype(BF16)

    @pl.when(i == pl.num_programs(1) - 1)
    def _():
        for p in pairs:
            st = st_ref[p]
            sfin_ref[2 * p] = st[:, :GLA_DK].T
            sfin_ref[2 * p + 1] = st[:, GLA_DK:].T


def _gla(src, zsrc, wg, bg, gn, s0, *, chunk, n_chunks, t_valid):
    b, t, _ = src.shape
    tb = chunk * n_chunks
    col = lambda w, c: pl.BlockSpec((None, tb, w), lambda bi, i, c=c: (bi, i, c))
    full = lambda shape: pl.BlockSpec(shape, lambda bi, i: (0,) * len(shape))
    state = pl.BlockSpec((None, GLA_HEADS, GLA_DK, GLA_DV), lambda bi, i: (bi, 0, 0, 0))
    kern = functools.partial(_gla_kernel, chunk=chunk, n_chunks=n_chunks, t_valid=t_valid)
    return pl.pallas_call(
        kern,
        grid=(b, t // tb),
        in_specs=[col(GLA_QK, 0), col(GLA_QK, 1), col(GLA_VW, 1), col(GLA_VW, 2), col(Z_PAD, 0),
                  full((2, Z_PAD, GLA_QK)), full((1, GLA_QK)), full((1, GLA_DV)), state],
        out_specs=[pl.BlockSpec((None, tb, GLA_VW), lambda bi, i: (bi, i, 0)), state],
        out_shape=[jax.ShapeDtypeStruct((b, t, GLA_VW), BF16),
                   jax.ShapeDtypeStruct((b, GLA_HEADS, GLA_DK, GLA_DV), F32)],
        scratch_shapes=[pltpu.VMEM((GLA_PAIRS, GLA_DV, 2 * GLA_DK), F32)],
        compiler_params=_params(("arbitrary", "arbitrary")),
        name="gla",
    )(src, src, src, src, zsrc, wg, bg, gn, s0)


QB = 128
DIL_SUB = 4


def _dil_branch_kernel(*refs, first, final, dilation, sub):
    if first:
        q_ref, kp_ref, kc_ref, vp_ref, vc_ref = refs[:5]
        outs = refs[5:]
    else:
        q_ref, kp_ref, kc_ref, vp_ref, vc_ref, num_ref, m_ref, d_ref = refs[:8]
        outs = refs[8:]
    i = pl.program_id(1)
    c = pl.program_id(2)
    qi = QB + lax.broadcasted_iota(jnp.int32, (QB, 1), 0)
    kj = lax.broadcasted_iota(jnp.int32, (1, 2 * QB), 1)
    off = qi - kj
    in_window = (off >= 0) & (off <= QB)
    lane = lax.broadcasted_iota(jnp.int32, (1, LANE), 1)
    lo = lane < HEAD_DIM
    heads = range(DIL_HEADS)
    cs = [slice(p * LANE, (p + 1) * LANE) for p in range(DIL_HEADS // 2)]
    sel = [lo if h % 2 == 0 else jnp.logical_not(lo) for h in heads]
    for u in range(sub):
        blk = slice(u * QB, (u + 1) * QB)
        rows = pl.ds(dilation * u * QB + c, QB, stride=dilation) if dilation > 1 else blk
        if u == 0:
            k_before, v_before = kp_ref[...], vp_ref[...]
            valid = in_window & ((kj >= QB) | (i > 0))
        else:
            before = slice((u - 1) * QB, u * QB)
            k_before, v_before = kc_ref[before, :], vc_ref[before, :]
            valid = in_window
        k2 = jnp.concatenate([k_before, kc_ref[blk, :]], axis=0)
        v2 = jnp.concatenate([v_before, vc_ref[blk, :]], axis=0)
        q = q_ref[blk, :]
        qm = [jnp.where(sel[h], q[:, cs[h // 2]], jnp.zeros((QB, LANE), BF16)) for h in heads]
        s = [lax.dot_general(qm[h], k2[:, cs[h // 2]], NT_DIMS, preferred_element_type=F32) for h in heads]
        s = [jnp.where(valid, s[h], -jnp.inf) for h in heads]
        m_new = [jnp.max(s[h], axis=-1, keepdims=True) for h in heads]
        if not first:
            m_in = m_ref[rows, :]
            d_in = d_ref[rows, :]
            m_old = [jnp.sum(jnp.where(lane == h, m_in, 0.0), axis=-1, keepdims=True) for h in heads]
            d_old = [jnp.sum(jnp.where(lane == h, d_in, 0.0), axis=-1, keepdims=True) for h in heads]
            m_new = [jnp.maximum(m_new[h], m_old[h]) for h in heads]
        pr = [jnp.exp(s[h] - m_new[h]) for h in heads]
        d_new = [jnp.sum(pr[h], axis=-1, keepdims=True) for h in heads]
        if not first:
            alpha = [jnp.exp(m_old[h] - m_new[h]) for h in heads]
            d_new = [d_new[h] + d_old[h] * alpha[h] for h in heads]
        pv = [jnp.dot(pr[h].astype(BF16), v2[:, cs[h // 2]], preferred_element_type=F32) for h in heads]
        pairs = []
        for p in range(DIL_HEADS // 2):
            num = jnp.where(lo, pv[2 * p], pv[2 * p + 1])
            if not first:
                num = num + num_ref[p, rows, :] * jnp.where(lo, alpha[2 * p], alpha[2 * p + 1])
            if final:
                num = num / jnp.where(lo, d_new[2 * p], d_new[2 * p + 1])
            else:
                outs[0][p, rows, :] = num
            pairs.append(num)
        if final:
            outs[0][rows, :] = jnp.concatenate(pairs, axis=1).astype(BF16)
        else:
            outs[1][rows, :] = functools.reduce(jnp.add, [jnp.where(lane == h, m_new[h], 0.0) for h in heads])
            outs[2][rows, :] = functools.reduce(jnp.add, [jnp.where(lane == h, d_new[h], 0.0) for h in heads])


def _dil_branch(q, k, v, state, *, first, final):
    b, r, ts, _ = q.shape
    t = r * ts
    sub = min(DIL_SUB, ts // QB)
    step = sub * QB
    nq = ts // step
    cur = pl.BlockSpec((None, None, step, DIL_W), lambda bi, i, c: (bi, c, i, 0))
    prev = pl.BlockSpec((None, None, QB, DIL_W), lambda bi, i, c: (bi, c, jnp.maximum(sub * i - 1, 0), 0))
    nat = lambda w: pl.BlockSpec((None, r * step, w), lambda bi, i, c: (bi, i, 0))
    nat_num = pl.BlockSpec((None, DIL_W // LANE, r * step, LANE), lambda bi, i, c: (bi, 0, i, 0))
    args = [q, k, k, v, v]
    in_specs = [cur, prev, cur, prev, cur]
    if not first:
        args += list(state)
        in_specs += [nat_num, nat(LANE), nat(LANE)]
    if final:
        assert r == 1
        out_specs = [nat(DIL_W)]
        out_shape = [jax.ShapeDtypeStruct((b, t, DIL_W), BF16)]
    else:
        out_specs = [nat_num, nat(LANE), nat(LANE)]
        out_shape = [jax.ShapeDtypeStruct((b, DIL_W // LANE, t, LANE), F32),
                     jax.ShapeDtypeStruct((b, t, LANE), F32),
                     jax.ShapeDtypeStruct((b, t, LANE), F32)]
    return pl.pallas_call(
        functools.partial(_dil_branch_kernel, first=first, final=final, dilation=r, sub=sub),
        grid=(b, nq, r),
        in_specs=in_specs,
        out_specs=out_specs,
        out_shape=out_shape,
        compiler_params=_params(("arbitrary", "arbitrary", "arbitrary")),
        name=f"dil_branch_r{r}",
    )(*args)


SAMPLE_HB = 12
SAMPLE_ROWS = 16


def _dil_sample_kernel(q_ref, kn_ref, vn_ref, kc_ref, vc_ref, o_ref, ko_ref, vo_ref, *, t_new, hist):
    new0 = SAMPLE_ROWS - t_new
    t_row = lax.broadcasted_iota(jnp.int32, (SAMPLE_ROWS, 1), 0)
    j = lax.broadcasted_iota(jnp.int32, (1, hist), 1)
    lane = lax.broadcasted_iota(jnp.int32, (1, LANE), 1)
    keep = lane < LANE - t_new

    def weight(off):
        w = jnp.zeros(off.shape, F32)
        for win, dil in DIL_PAIRS:
            w = w + jnp.where((off >= 0) & (off <= win) & (off % dil == 0), 1.0, 0.0)
        return w

    w_c = weight(hist + t_row - j)
    w_n = [weight(t_row - t) for t in range(t_new)]
    heads = range(SAMPLE_HB)
    hs = [slice(h * HEAD_DIM, (h + 1) * HEAD_DIM) for h in heads]
    qs = [q_ref[:, hs[h]] for h in heads]
    s_c = [jnp.dot(qs[h].astype(BF16), kc_ref[h].astype(BF16), preferred_element_type=F32) for h in heads]
    s_c = [jnp.where(w_c > 0, s, -jnp.inf) for s in s_c]
    s_n = [[jnp.where(w_n[t] > 0, jnp.sum(qs[h] * kn_ref[new0 + t:new0 + t + 1, hs[h]], axis=-1, keepdims=True), -jnp.inf)
            for t in range(t_new)] for h in heads]
    m = [functools.reduce(jnp.maximum, s_n[h], jnp.max(s_c[h], axis=-1, keepdims=True)) for h in heads]
    p_c = [w_c * jnp.exp(s_c[h] - m[h]) for h in heads]
    p_n = [[w_n[t] * jnp.exp(s_n[h][t] - m[h]) for t in range(t_new)] for h in heads]
    den = [functools.reduce(jnp.add, p_n[h], jnp.sum(p_c[h], axis=-1, keepdims=True)) for h in heads]
    num = [lax.dot_general(p_c[h].astype(BF16), vc_ref[h].astype(BF16), NT_DIMS, preferred_element_type=F32)
           for h in heads]
    num = [functools.reduce(jnp.add, [p_n[h][t] * vn_ref[new0 + t:new0 + t + 1, hs[h]] for t in range(t_new)], num[h])
           for h in heads]
    o_ref[...] = jnp.concatenate([num[h] / den[h] for h in heads], axis=1)

    blank = jnp.zeros((LANE - SAMPLE_ROWS, HEAD_DIM), F32)
    for h in heads:
        for src, new_ref, dst in ((kc_ref, kn_ref, ko_ref), (vc_ref, vn_ref, vo_ref)):
            new_tile = jnp.concatenate([blank, new_ref[:, hs[h]]], axis=0).T
            nxt = pltpu.roll(src[h, :, 0:LANE], LANE - t_new, axis=1)
            for c in range(hist // LANE):
                cur_t = nxt
                if c + 1 < hist // LANE:
                    nxt = pltpu.roll(src[h, :, (c + 1) * LANE:(c + 2) * LANE], LANE - t_new, axis=1)
                else:
                    nxt = new_tile
                dst[h, :, c * LANE:(c + 1) * LANE] = jnp.where(keep, cur_t, nxt)


def _dil_sample(q, kn, vn, cache_kt, cache_vt, t_new):
    b, nh, _, hist = cache_kt.shape
    w = SAMPLE_HB * HEAD_DIM
    small = pl.BlockSpec((None, SAMPLE_ROWS, w), lambda bi, c: (bi, 0, c))
    big = pl.BlockSpec((None, SAMPLE_HB, HEAD_DIM, hist), lambda bi, c: (bi, c, 0, 0))
    return pl.pallas_call(
        functools.partial(_dil_sample_kernel, t_new=t_new, hist=hist),
        grid=(b, nh // SAMPLE_HB),
        in_specs=[small, small, small, big, big],
        out_specs=[small, big, big],
        out_shape=[jax.ShapeDtypeStruct((b, SAMPLE_ROWS, nh * HEAD_DIM), F32),
                   jax.ShapeDtypeStruct(cache_kt.shape, F32),
                   jax.ShapeDtypeStruct(cache_vt.shape, F32)],
        compiler_params=_params(("arbitrary", "arbitrary"), vmem=62 * 1024 * 1024),
        name="dil_sample",
    )(q, kn, vn, cache_kt, cache_vt)


OUT_ROWS = 256

STRIDE_ROWS = 256
STRIDE_GROUPS = STRIDE_ROWS // SUBLANE


def _to_strided(val, stage_ref):
    for j in range(val.shape[1] // LANE):
        for s_ in range(SUBLANE):
            stage_ref[j, pl.ds(s_, STRIDE_GROUPS, stride=SUBLANE), :] = (
                val[s_ * STRIDE_GROUPS:(s_ + 1) * STRIDE_GROUPS, j * LANE:(j + 1) * LANE])
    return jnp.concatenate([stage_ref[j] for j in range(val.shape[1] // LANE)], axis=1)


def _from_strided(val, stage_ref):
    for j in range(val.shape[1] // LANE):
        stage_ref[j] = val[:, j * LANE:(j + 1) * LANE]
    return jnp.concatenate(
        [jnp.concatenate([stage_ref[j, pl.ds(s_, STRIDE_GROUPS, stride=SUBLANE), :]
                          for j in range(val.shape[1] // LANE)], axis=1) for s_ in range(SUBLANE)], axis=0)


def _out_proj_kernel(oa_ref, ob_ref, w_ref, x_ref, g_ref, h_ref, xn_ref, wbf_ref, *stage, strided):
    @pl.when(pl.program_id(0) == 0)
    def _():
        wbf_ref[...] = w_ref[...].astype(BF16)

    for r0 in range(0, x_ref.shape[0], OUT_ROWS):
        rows = slice(r0, min(r0 + OUT_ROWS, x_ref.shape[0]))
        y = (jnp.dot(oa_ref[rows, :], wbf_ref[:GLA_VW, :], preferred_element_type=F32)
             + jnp.dot(ob_ref[rows, :], wbf_ref[GLA_VW:, :], preferred_element_type=F32))
        h = x_ref[rows, :] + y
        h_ref[rows, :] = h
        ms = jnp.mean(h * h, axis=-1, keepdims=True)
        xn = h * lax.rsqrt(ms + EPS) * g_ref[...]
        if strided:
            xn = _to_strided(xn, stage[0])
        xn_ref[rows, :] = xn.astype(BF16)


def _out_proj(oa, ob, w, x, g, tm, strided=False):
    m, d = x.shape
    assert not strided or OUT_ROWS == STRIDE_ROWS
    row = lambda width: pl.BlockSpec((tm, width), lambda i: (i, 0))
    g2 = g.reshape(1, d)
    scratch = [pltpu.VMEM(w.shape, BF16)]
    if strided:
        scratch.append(pltpu.VMEM((d // LANE, STRIDE_ROWS, LANE), F32))
    return pl.pallas_call(
        functools.partial(_out_proj_kernel, strided=strided),
        grid=(m // tm,),
        in_specs=[row(GLA_VW), row(DIL_W),
                  pl.BlockSpec(w.shape, lambda i: (0, 0), pipeline_mode=pl.Buffered(1)),
                  row(d), pl.BlockSpec(g2.shape, lambda i: (0, 0))],
        out_specs=[row(d), row(d)],
        out_shape=[jax.ShapeDtypeStruct((m, d), F32), jax.ShapeDtypeStruct((m, d), BF16)],
        scratch_shapes=scratch,
        compiler_params=_params(("arbitrary",)),
        name="out_proj",
    )(oa, ob, w, x, g2)


FF_ROWS = STRIDE_ROWS
FF_TAIL = 2 * SUBLANE
FF_SUB = FF_TN // LANE
FF_TILES = D_FF // LANE


def _conv_taps(cw, cb, up, prev1, prev2):
    return cb + cw[0:1] * prev2 + cw[1:2] * prev1 + cw[2:3] * up


def _swiglu(u, g):
    h = 0.5 * g
    return ((h + h * jnp.tanh(h)) * u).astype(BF16)


def _stage_up_weights(w_blocks, wbf_ref, j):
    for half in range(2):
        for q in range(FF_SUB):
            blk = w_blocks[half * FF_SUB + q][...]
            if (FF_NT - 1) * FF_SUB + q >= FF_TILES:
                blk = jnp.where(j * FF_SUB + q < FF_TILES, blk, 0.0)
            wbf_ref[half, :, q * LANE:(q + 1) * LANE] = blk.astype(BF16)


def _up_weight_specs(tile_of_step):
    specs = []
    for half in range(2):
        for q in range(FF_SUB):
            def index(*ids, half=half, q=q):
                blk = jnp.minimum(tile_of_step(*ids) * FF_SUB + q, FF_TILES - 1)
                return (0, half * FF_TILES + blk)
            specs.append(pl.BlockSpec((D_MODEL, LANE), index))
    return specs


def _ffn_up_seq_kernel(*refs, tiles_per_seq):
    x_ref = refs[0]
    w_blocks = refs[1:1 + 2 * FF_SUB]
    cwu_ref, cwg_ref, cbu_ref, cbg_ref, act_ref, tu_ref, tg_ref, wbf_ref, carry_ref = refs[1 + 2 * FF_SUB:]
    j = pl.program_id(0)
    i = pl.program_id(1)
    tm = x_ref.shape[0]

    @pl.when(i == 0)
    def _():
        _stage_up_weights(w_blocks, wbf_ref, j)

    @pl.when(i % tiles_per_seq == 0)
    def _():
        carry_ref[...] = jnp.zeros_like(carry_ref)

    carry = [carry_ref[0], carry_ref[1]]
    shift = lambda before, cur: pltpu.roll(jnp.concatenate([before, cur], axis=0), 1, axis=0)[SUBLANE:]
    for r0 in range(0, tm, FF_ROWS):
        x = x_ref[r0:r0 + FF_ROWS, :]
        conv = []
        for idx, (cw_ref, cb_ref) in enumerate(((cwu_ref, cbu_ref), (cwg_ref, cbg_ref))):
            up = jnp.dot(x, wbf_ref[idx], preferred_element_type=F32)
            tail = up[FF_ROWS - FF_TAIL:]
            p_a = shift(carry[idx][:SUBLANE], tail[:SUBLANE])
            p_b = shift(carry[idx][SUBLANE:], tail[SUBLANE:])
            prev1 = jnp.concatenate([p_b, up[:FF_ROWS - SUBLANE]], axis=0)
            prev2 = jnp.concatenate([p_a, p_b, up[:FF_ROWS - 2 * SUBLANE]], axis=0)
            carry[idx] = tail
            conv.append(_conv_taps(cw_ref[...], cb_ref[...], up, prev1, prev2))
        act_ref[r0:r0 + FF_ROWS, :] = _swiglu(*conv)
    for idx, tail_ref in enumerate((tu_ref, tg_ref)):
        carry_ref[idx] = carry[idx]
        tail_ref[...] = carry[idx]


def _ffn_up_seq(xn, w_up, conv_w, conv_b, tm, t_seq):
    m, d = xn.shape
    tiles_per_seq = t_seq // tm
    xs = pl.BlockSpec((tm, d), lambda j, i: (i, 0))
    cu = lambda r: pl.BlockSpec((r, FF_TN), lambda j, i: (0, j))
    cg = lambda r: pl.BlockSpec((r, FF_TN), lambda j, i: (0, FF_NT + j))
    act = pl.BlockSpec((tm, FF_TN), lambda j, i: (i, j))
    tail = pl.BlockSpec((None, FF_TAIL, FF_TN), lambda j, i: (i // tiles_per_seq, 0, j))
    w_specs = _up_weight_specs(lambda j, i: j)
    return pl.pallas_call(
        functools.partial(_ffn_up_seq_kernel, tiles_per_seq=tiles_per_seq),
        grid=(FF_NT, m // tm),
        in_specs=[xs] + w_specs + [cu(CONV_W), cg(CONV_W), cu(1), cg(1)],
        out_specs=[act, tail, tail],
        out_shape=[jax.ShapeDtypeStruct((m, D_FF_P), BF16)]
        + [jax.ShapeDtypeStruct((m // t_seq, FF_TAIL, D_FF_P), F32)] * 2,
        scratch_shapes=[pltpu.VMEM((2, d, FF_TN), BF16), pltpu.VMEM((2, FF_TAIL, FF_TN), F32)],
        compiler_params=_params(("arbitrary", "arbitrary")),
        name="ffn_up",
    )(xn, *([w_up] * len(w_specs)), conv_w, conv_w, conv_b, conv_b)


def _ffn_up_hist_kernel(*refs, n_seq):
    x_ref = refs[0]
    w_blocks = refs[1:1 + 2 * FF_SUB]
    cwu_ref, cwg_ref, cbu_ref, cbg_ref, hu_ref, hg_ref, act_ref, upu_ref, upg_ref, wbf_ref = refs[1 + 2 * FF_SUB:]
    _stage_up_weights(w_blocks, wbf_ref, pl.program_id(0))
    x = x_ref[...]
    m = x.shape[0]
    conv = []
    for idx, (cw_ref, cb_ref, h_ref, up_ref) in enumerate((
            (cwu_ref, cbu_ref, hu_ref, upu_ref), (cwg_ref, cbg_ref, hg_ref, upg_ref))):
        up = jnp.dot(x, wbf_ref[idx], preferred_element_type=F32)
        full = jnp.concatenate([h_ref[...], up], axis=0)
        prev1 = full[n_seq:n_seq + m]
        prev2 = full[:m]
        up_ref[...] = up
        conv.append(_conv_taps(cw_ref[...], cb_ref[...], up, prev1, prev2))
    act_ref[...] = _swiglu(*conv)


def _ffn_up_hist(xn, w_up, conv_w, conv_b, hist, n_seq):
    m, d = xn.shape
    assert n_seq % SUBLANE == 0
    xs = pl.BlockSpec((m, d), lambda j: (0, 0))
    u = lambda r: pl.BlockSpec((r, FF_TN), lambda j: (0, j))
    g = lambda r: pl.BlockSpec((r, FF_TN), lambda j: (0, FF_NT + j))
    w_specs = _up_weight_specs(lambda j: j)
    nh = hist.shape[0]
    return pl.pallas_call(
        functools.partial(_ffn_up_hist_kernel, n_seq=n_seq),
        grid=(FF_NT,),
        in_specs=[xs] + w_specs + [u(CONV_W), g(CONV_W), u(1), g(1), u(nh), g(nh)],
        out_specs=[u(m), u(m), u(m)],
        out_shape=[jax.ShapeDtypeStruct((m, D_FF_P), BF16)] + [jax.ShapeDtypeStruct((m, D_FF_P), F32)] * 2,
        scratch_shapes=[pltpu.VMEM((2, d, FF_TN), BF16)],
        compiler_params=_params(("arbitrary",)),
        name="ffn_up_hist",
    )(xn, *([w_up] * len(w_specs)), conv_w, conv_w, conv_b, conv_b, hist, hist)


DOWN_ROWS = 256


def _ffn_down_kernel(a_ref, w_ref, h_ref, g_ref, y_ref, *stage, strided):
    kdim = w_ref.shape[0]
    for r0 in range(0, a_ref.shape[0], DOWN_ROWS):
        rows = slice(r0, min(r0 + DOWN_ROWS, a_ref.shape[0]))
        f = jnp.dot(a_ref[rows, :kdim], w_ref[...], preferred_element_type=F32)
        if strided:
            f = _from_strided(f, stage[0])
        h = h_ref[rows, :] + f
        ms = jnp.mean(h * h, axis=-1, keepdims=True)
        y_ref[rows, :] = h * lax.rsqrt(ms + EPS) * g_ref[...]


def _ffn_down(act, w, h, g, tm, strided=False):
    m, d = h.shape
    assert not strided or DOWN_ROWS == STRIDE_ROWS
    scratch = [pltpu.VMEM((d // LANE, STRIDE_ROWS, LANE), F32)] if strided else []
    return pl.pallas_call(
        functools.partial(_ffn_down_kernel, strided=strided),
        grid=(m // tm,),
        in_specs=[pl.BlockSpec((tm, act.shape[1]), lambda i: (i, 0)),
                  pl.BlockSpec(w.shape, lambda i: (0, 0), pipeline_mode=pl.Buffered(1)),
                  pl.BlockSpec((tm, d), lambda i: (i, 0)),
                  pl.BlockSpec((1, d), lambda i: (0, 0))],
        out_specs=pl.BlockSpec((tm, d), lambda i: (i, 0)),
        out_shape=jax.ShapeDtypeStruct((m, d), F32),
        scratch_shapes=scratch,
        compiler_params=_params(("arbitrary",)),
        name="ffn_down",
    )(act, w, h, g.reshape(1, d))


def _pad_cols(a, width):
    return jnp.pad(a, ((0, 0), (0, width - a.shape[1])))


def _split_pad_ff(a):
    return jnp.concatenate([_pad_cols(a[:, :D_FF], D_FF_P), _pad_cols(a[:, D_FF:], D_FF_P)], axis=1)


def _unpad_ff(u, g):
    return jnp.concatenate([u[..., :D_FF], g[..., :D_FF]], axis=-1)


def kernel(x_prompt, x_sample, state_gla, cache_dil_k, cache_dil_v, state_ffn_conv, norm_mix, w_in, w_gate_up,
           b_gate, gla_norm, w_out, norm_ffn, w_ffn_up, ffn_conv_w, ffn_conv_b, w_ffn_down, norm_final):
    bp, tp, d = x_prompt.shape
    bs, ts, _ = x_sample.shape
    l = 0

    z0 = 2 * GLA_QK + 2 * GLA_VW
    w_in_t = w_in[l].T
    w_z_t = jnp.pad(w_in_t[z0:z0 + GLA_RANK], ((0, Z_PAD - GLA_RANK), (0, 0)))
    wg_f = jnp.pad(w_gate_up[l], ((0, Z_PAD - GLA_RANK), (0, 0)))
    wg_hi = wg_f.astype(BF16)
    wg_p = jnp.stack([wg_hi, (wg_f - wg_hi.astype(F32)).astype(BF16)])
    bg = b_gate[l].reshape(1, GLA_QK)
    gn = gla_norm[l].reshape(1, GLA_DV)
    cw_p = _split_pad_ff(ffn_conv_w[l])
    cb_p = _split_pad_ff(ffn_conv_b[l].reshape(1, -1))
    w_dn = w_ffn_down[l].astype(BF16)

    mp = bp * tp
    xp = x_prompt.reshape(mp, d)
    xn_mix_p = _rms(xp, norm_mix[l], 512)
    mix_p = _proj(xn_mix_p, w_in_t, 2048, 768, n=z0)
    dil_p = _proj(xn_mix_p, w_in_t, 2048, 768, n=3 * DIL_W, row0=z0 + GLA_RANK)
    z_p = _proj(xn_mix_p, w_z_t, 2048, Z_PAD)
    cos_p, sin_p = _rope_tables(jnp.arange(tp))
    cos_p = jnp.tile(cos_p, (bp, 1))
    sin_p = jnp.tile(sin_p, (bp, 1))
    rope_out = _rope(dil_p, cos_p, sin_p, 512, n_seq=bp)
    by_class = {1: tuple(a.reshape(bp, 1, tp, DIL_W) for a in rope_out[:3])}
    for di, r in enumerate(CLASS_DILATIONS):
        by_class[r] = tuple(rope_out[3 + 3 * di:6 + 3 * di])
    k_kept_t, v_kept_t = rope_out[-2:]

    oa_p, gla_p = _gla(mix_p.reshape(bp, tp, z0), z_p.reshape(bp, tp, Z_PAD), wg_p, bg, gn,
                       jnp.zeros((bp, GLA_HEADS, GLA_DK, GLA_DV), F32),
                       chunk=GLA_ROWS, n_chunks=4, t_valid=tp)

    state = _dil_branch(*by_class[16], None, first=True, final=False)
    state = _dil_branch(*by_class[4], state, first=False, final=False)
    (ob_p,) = _dil_branch(*by_class[1], state, first=False, final=True)

    h_p, xn_p = _out_proj(oa_p.reshape(mp, GLA_VW), ob_p.reshape(mp, DIL_W), w_out[l], xp, norm_ffn[l], 512, strided=True)
    act_p, tail_u, tail_g = _ffn_up_seq(xn_p, w_ffn_up[l], cw_p, cb_p, 2048, tp)
    y_p = _ffn_down(act_p, w_dn, h_p, norm_final, 512, strided=True)

    buf_p = min(DIL_WINDOW, tp)
    y_prompt = y_p.reshape(bp, tp, d)
    new_gla_p = gla_p[None]
    new_k_p = k_kept_t.reshape(bp, DIL_HEADS, HEAD_DIM, buf_p).transpose(0, 3, 1, 2)[None]
    new_v_p = v_kept_t.reshape(bp, DIL_HEADS, HEAD_DIM, buf_p).transpose(0, 3, 1, 2)[None]
    last_two = lambda a: a[:, SUBLANE - 1::SUBLANE]
    new_conv_p = _unpad_ff(last_two(tail_u), last_two(tail_g))[None]

    ms_ = bs * ts
    xs = x_sample.reshape(ms_, d)
    xn_mix_s = _rms(xs, norm_mix[l], ms_)
    mix_s = _proj(xn_mix_s, w_in_t, ms_, 768, n=z0)
    dil_s = _proj(xn_mix_s, w_in_t, ms_, 768, n=3 * DIL_W, row0=z0 + GLA_RANK)
    z_s = _proj(xn_mix_s, w_z_t, ms_, Z_PAD)
    cos_s, sin_s = _rope_tables(PAST_LEN + jnp.arange(ts))
    cos_s = jnp.tile(cos_s, (bs, 1))
    sin_s = jnp.tile(sin_s, (bs, 1))
    q_rs, k_rs = _rope(dil_s, cos_s, sin_s, ms_)

    pad_chunk = lambda a: jnp.pad(a.reshape(bs, ts, -1), ((0, 0), (0, GLA_ROWS_SHORT - ts), (0, 0)))
    oa_s, gla_s = _gla(pad_chunk(mix_s), pad_chunk(z_s), wg_p, bg, gn, state_gla[l],
                       chunk=GLA_ROWS_SHORT, n_chunks=1, t_valid=ts)
    oa_s = oa_s[:, :ts].reshape(ms_, GLA_VW)

    v_s = dil_s[:, 2 * DIL_W:]
    rows_first = lambda a: jnp.pad(a.reshape(bs, ts, DIL_W), ((0, 0), (0, SAMPLE_ROWS - ts), (0, 0)))
    rows_last = lambda a: jnp.pad(a.reshape(bs, ts, DIL_W), ((0, 0), (SAMPLE_ROWS - ts, 0), (0, 0)))
    time_minor = lambda a: a.transpose(0, 2, 3, 1)
    ob_s, k_s_t, v_s_t = _dil_sample(rows_first(q_rs), rows_last(k_rs), rows_last(v_s),
                                     time_minor(cache_dil_k[l]), time_minor(cache_dil_v[l]), ts)
    ob_s = ob_s[:, :ts].astype(BF16)

    h_s, xn_s = _out_proj(oa_s, ob_s.reshape(ms_, DIL_W), w_out[l], xs, norm_ffn[l], ms_)
    time_major = lambda a: a.reshape(bs, -1, a.shape[-1]).transpose(1, 0, 2).reshape(-1, a.shape[-1])
    batch_major = lambda a: a.reshape(-1, bs, a.shape[-1]).transpose(1, 0, 2)
    conv_hist = time_major(_split_pad_ff(state_ffn_conv[l].reshape(bs * (CONV_W - 1), 2 * D_FF)))
    act_s, up_u, up_g = _ffn_up_hist(time_major(xn_s), w_ffn_up[l], cw_p, cb_p, conv_hist, bs)
    act_s = batch_major(act_s).reshape(ms_, D_FF_P)
    y_s = _ffn_down(act_s, w_dn, h_s, norm_final, ms_)

    y_sample = y_s.reshape(bs, ts, d)
    new_gla_s = gla_s[None]
    new_k_s = k_s_t.transpose(0, 3, 1, 2)[None]
    new_v_s = v_s_t.transpose(0, 3, 1, 2)[None]
    new_conv_s = batch_major(_unpad_ff(up_u, up_g))[None, :, ts - (CONV_W - 1):]

    return (y_prompt, y_sample, new_gla_p, new_gla_s, new_k_p, new_k_s, new_v_p, new_v_s,
            new_conv_p, new_conv_s)
```
